```python
import math
import jax, jax.numpy as jnp
from jax import lax
import numpy as np

D_MODEL = 2048
BATCH = 1
SEQ = 8192
DEPTH = 2
DEC_BATCH = 8
DEC_SEQ = 16
PAST_LEN = 1024

CHUNK = 64
EPS = 1e-6
GLA_HEADS = 4
GLA_DK = D_MODEL // (2 * GLA_HEADS)
GLA_DV = D_MODEL // GLA_HEADS
GLA_GATE_RANK = 16
GLA_GATE_TAU = 16.0
SSM_D_INNER = 2 * D_MODEL
SSM_HEAD_DIM = 64
SSM_HEADS = SSM_D_INNER // SSM_HEAD_DIM
SSM_GROUPS = 8
SSM_HPG = SSM_HEADS // SSM_GROUPS
SSM_D_STATE = 128
SSM_CONV = 4
SSM_CONV_DIM = SSM_D_INNER + 2 * SSM_GROUPS * SSM_D_STATE
MOE_GROUPS = 4
MOE_PER_GROUP = 4
MOE_EXPERTS = MOE_GROUPS * MOE_PER_GROUP
MOE_TOP_K = 2
MOE_D_FF = D_MODEL // 4
IN_SIZES = (GLA_HEADS * GLA_DK, GLA_HEADS * GLA_DK, GLA_HEADS * GLA_DV, GLA_HEADS * GLA_DV, GLA_GATE_RANK,
            SSM_D_INNER, SSM_CONV_DIM, SSM_HEADS, 2 * D_MODEL)
N_IN = sum(IN_SIZES)

kernel_name = 'hybrid_gla_ssd_hmoe_stream_step'


def _rmsnorm(x, g):
    xf = x.astype(jnp.float32)
    y = xf * lax.rsqrt(jnp.mean(xf * xf, axis=-1, keepdims=True) + EPS)
    return (y * g.astype(jnp.float32)).astype(x.dtype)


def _to_chunks(a, L):
    b, t = a.shape[:2]
    return jnp.moveaxis(a.reshape((b, t // L, L) + a.shape[2:]), 1, 0)


def _from_chunks(a):
    a = jnp.moveaxis(a, 0, 1)
    return a.reshape((a.shape[0], a.shape[1] * a.shape[2]) + a.shape[3:])


def _causal_mask(L):
    return jnp.tril(jnp.ones((L, L), dtype=bool))[None, :, :, None, None]


def _gla_chunked(q, k, v, g, s0):
    L = min(CHUNK, q.shape[1])
    mask = _causal_mask(L)

    def step(s, c):
        qc, kc, vc, gc = c
        b = jnp.cumsum(gc, axis=1)
        decay = jnp.exp(jnp.where(mask, b[:, :, None] - b[:, None, :], -jnp.inf))
        att = jnp.einsum('bihk,bjhk,bijhk->bhij', qc, kc, decay)
        o = jnp.einsum('bhij,bjhv->bihv', att, vc) + jnp.einsum('bihk,bhkv->bihv', qc * jnp.exp(b), s)
        b_last = b[:, -1]
        s = jnp.exp(b_last)[..., None] * s + jnp.einsum('bjhk,bjhv->bhkv', kc * jnp.exp(b_last[:, None] - b), vc)
        return s, o

    s, o = lax.scan(step, s0, (_to_chunks(q, L), _to_chunks(k, L), _to_chunks(v, L), _to_chunks(g, L)))
    return _from_chunks(o), s


def _ssd_chunked(x, bm, cm, dt, a_neg, h0):
    L = min(CHUNK, x.shape[1])
    mask = _causal_mask(L)
    la = dt * a_neg

    def step(h, c):
        xc, bc, cc, dtc, lac = c
        cum = jnp.cumsum(lac, axis=1)
        seg = jnp.exp(jnp.where(mask, cum[:, :, None] - cum[:, None, :], -jnp.inf))
        cb = jnp.einsum('bign,bjgn->bijg', cc, bc)
        m = cb[..., None] * seg * dtc[:, None]
        y = (jnp.einsum('bijgh,bjghp->bighp', m, xc)
             + jnp.einsum('bign,bghpn->bighp', cc, h) * jnp.exp(cum)[..., None])
        last = cum[:, -1]
        w = jnp.exp(last[:, None] - cum) * dtc
        h = jnp.exp(last)[..., None, None] * h + jnp.einsum('bjgn,bjgh,bjghp->bghpn', bc, w, xc)
        return h, y

    h, y = lax.scan(step, h0, (_to_chunks(x, L), _to_chunks(bm, L), _to_chunks(cm, L),
                               _to_chunks(dt, L), _to_chunks(la, L)))
    return _from_chunks(y), h


def _causal_conv(xbc, conv_state, w, b):
    xp = jnp.concatenate([conv_state.astype(xbc.dtype), xbc], axis=1)
    t = xbc.shape[1]
    out = b
    for i in range(SSM_CONV):
        out = out + xp[:, i:i + t] * w[i]
    return out, xp[:, -(SSM_CONV - 1):]


def _mixer(u, s_gla, s_ssm, s_conv, w_in, gla_w_gate2, gla_b_gate, gla_norm, gla_w_out,
           ssm_conv_w, ssm_conv_b, ssm_dt_bias, ssm_a_log, ssm_d, ssm_norm, ssm_w_out, w_o):
    f32 = jnp.float32
    nb, t = u.shape[:2]
    proj = u @ w_in
    offs = np.cumsum(IN_SIZES)[:-1].tolist()
    q, k, v, r, glr, z, xbc, dt_raw, mg = jnp.split(proj, offs, axis=-1)
    gk = jax.nn.log_sigmoid((glr @ gla_w_gate2 + gla_b_gate).astype(f32)) / GLA_GATE_TAU
    q = q.astype(f32).reshape(nb, t, GLA_HEADS, GLA_DK) * (GLA_DK ** -0.5)
    k = k.astype(f32).reshape(nb, t, GLA_HEADS, GLA_DK)
    v = v.astype(f32).reshape(nb, t, GLA_HEADS, GLA_DV)
    o, s_gla_new = _gla_chunked(q, k, v, gk.reshape(nb, t, GLA_HEADS, GLA_DK), s_gla.astype(f32))
    o = o * lax.rsqrt(jnp.mean(o * o, axis=-1, keepdims=True) + EPS) * gla_norm.astype(f32)
    o = o * jax.nn.silu(r.astype(f32)).reshape(nb, t, GLA_HEADS, GLA_DV)
    y_a = o.reshape(nb, t, GLA_HEADS * GLA_DV).astype(u.dtype) @ gla_w_out
    xbc, s_conv_new = _causal_conv(xbc, s_conv, ssm_conv_w, ssm_conv_b)
    xbc = jax.nn.silu(xbc.astype(f32))
    xs, bm, cm = jnp.split(xbc, [SSM_D_INNER, SSM_D_INNER + SSM_GROUPS * SSM_D_STATE], axis=-1)
    xs = xs.reshape(nb, t, SSM_GROUPS, SSM_HPG, SSM_HEAD_DIM)
    bm = bm.reshape(nb, t, SSM_GROUPS, SSM_D_STATE)
    cm = cm.reshape(nb, t, SSM_GROUPS, SSM_D_STATE)
    dt = jax.nn.softplus(dt_raw.astype(f32) + ssm_dt_bias.astype(f32)).reshape(nb, t, SSM_GROUPS, SSM_HPG)
    a_neg = -jnp.exp(ssm_a_log.astype(f32)).reshape(SSM_GROUPS, SSM_HPG)
    h0 = s_ssm.astype(f32).reshape(nb, SSM_GROUPS, SSM_HPG, SSM_HEAD_DIM, SSM_D_STATE)
    y, h = _ssd_chunked(xs, bm, cm, dt, a_neg, h0)
    y = y + ssm_d.astype(f32).reshape(SSM_GROUPS, SSM_HPG)[..., None] * xs
    y = (y.reshape(nb, t, SSM_D_INNER) * jax.nn.silu(z.astype(f32))).reshape(nb, t, SSM_GROUPS, -1)
    y = (y * lax.rsqrt(jnp.mean(y * y, axis=-1, keepdims=True) + EPS)).reshape(nb, t, SSM_D_INNER)
    y_b = (y * ssm_norm.astype(f32)).astype(u.dtype) @ ssm_w_out
    gates = jax.nn.sigmoid(mg.astype(f32)).reshape(nb, t, 2, D_MODEL)
    merged = gates[:, :, 0] * y_a.astype(f32) + gates[:, :, 1] * y_b.astype(f32)
    out = merged.astype(u.dtype) @ w_o
    return (out, s_gla_new.astype(s_gla.dtype),
            h.reshape(nb, SSM_HEADS, SSM_HEAD_DIM, SSM_D_STATE).astype(s_ssm.dtype),
            s_conv_new.astype(s_conv.dtype))


def _hier_moe(u, router_w1, router_b1, router_w2, router_b2, exp_w_gate, exp_w_up, exp_w_down):
    f32 = jnp.float32
    nb, t = u.shape[:2]
    uf = u.astype(f32)
    lg = uf @ router_w1.astype(f32) + router_b1.astype(f32)
    pg = jax.nn.softmax(lg, axis=-1)
    _, gsel = lax.top_k(lg, 1)
    p_sel = jnp.take_along_axis(pg, gsel, axis=-1)
    le = (uf @ router_w2.astype(f32) + router_b2.astype(f32)).reshape(nb, t, MOE_GROUPS, MOE_PER_GROUP)
    le = jnp.take_along_axis(le, gsel[..., None], axis=-2)[..., 0, :]
    tv, ti = lax.top_k(le, MOE_TOP_K)
    wk = jax.nn.softmax(tv, axis=-1) * p_sel
    eid = gsel * MOE_PER_GROUP + ti
    cw = jnp.einsum('btk,btke->bte', wk, jax.nn.one_hot(eid, MOE_EXPERTS, dtype=f32))
    hid = jax.nn.silu(jnp.einsum('btd,edf->btef', u, exp_w_gate)) * jnp.einsum('btd,edf->btef', u, exp_w_up)
    hid = hid * cw[..., None].astype(hid.dtype)
    return jnp.einsum('btef,efd->btd', hid, exp_w_down)


def _layer(x, s_gla, s_ssm, s_conv, norm_mix, w_in, gla_w_gate2, gla_b_gate, gla_norm, gla_w_out,
           ssm_conv_w, ssm_conv_b, ssm_dt_bias, ssm_a_log, ssm_d, ssm_norm, ssm_w_out, w_o,
           norm_ffn, router_w1, router_b1, router_w2, router_b2, exp_w_gate, exp_w_up, exp_w_down):
    mix, s_gla, s_ssm, s_conv = _mixer(_rmsnorm(x, norm_mix), s_gla, s_ssm, s_conv, w_in, gla_w_gate2,
                                       gla_b_gate, gla_norm, gla_w_out, ssm_conv_w, ssm_conv_b,
                                       ssm_dt_bias, ssm_a_log, ssm_d, ssm_norm, ssm_w_out, w_o)
    x = x + mix
    x = x + _hier_moe(_rmsnorm(x, norm_ffn), router_w1, router_b1, router_w2, router_b2,
                      exp_w_gate, exp_w_up, exp_w_down)
    return x, s_gla, s_ssm, s_conv


def setup_inputs(seed: int = 0) -> dict:
    key = jax.random.key(seed)
    ks = jax.random.split(key, 32)
    f32 = jnp.float32

    def nrm(k, shape, scale):
        return jax.random.normal(k, shape, f32) * scale

    dt0 = jnp.exp(jax.random.uniform(ks[15], (DEPTH, SSM_HEADS), f32, minval=math.log(1e-3), maxval=math.log(1e-1)))
    return {
        'x_prompt': nrm(ks[0], (BATCH, SEQ, D_MODEL), 1.0),
        'x_sample': nrm(ks[1], (DEC_BATCH, DEC_SEQ, D_MODEL), 1.0),
        'state_gla': nrm(ks[2], (DEPTH, DEC_BATCH, GLA_HEADS, GLA_DK, GLA_DV), 1.0),
        'state_ssm': nrm(ks[3], (DEPTH, DEC_BATCH, SSM_HEADS, SSM_HEAD_DIM, SSM_D_STATE), 0.1),
        'state_conv': nrm(ks[4], (DEPTH, DEC_BATCH, SSM_CONV - 1, SSM_CONV_DIM), 1.0),
        'norm_mix': 1.0 + nrm(ks[5], (DEPTH, D_MODEL), 0.01),
        'w_in': nrm(ks[6], (DEPTH, D_MODEL, N_IN), D_MODEL ** -0.5),
        'gla_w_gate2': nrm(ks[7], (DEPTH, GLA_GATE_RANK, GLA_HEADS * GLA_DK), GLA_GATE_RANK ** -0.5),
        'gla_b_gate': nrm(ks[8], (DEPTH, GLA_HEADS * GLA_DK), 0.1),
        'gla_norm': 1.0 + nrm(ks[9], (DEPTH, GLA_DV), 0.01),
        'gla_w_out': nrm(ks[10], (DEPTH, GLA_HEADS * GLA_DV, D_MODEL), (GLA_HEADS * GLA_DV) ** -0.5),
        'ssm_conv_w': nrm(ks[11], (DEPTH, SSM_CONV, SSM_CONV_DIM), 0.5),
        'ssm_conv_b': nrm(ks[12], (DEPTH, SSM_CONV_DIM), 0.02),
        'ssm_dt_bias': dt0 + jnp.log(-jnp.expm1(-dt0)),
        'ssm_a_log': jnp.log(jax.random.uniform(ks[13], (DEPTH, SSM_HEADS), f32, minval=1.0, maxval=16.0)),
        'ssm_d': 1.0 + nrm(ks[14], (DEPTH, SSM_HEADS), 0.1),
        'ssm_norm': 1.0 + nrm(ks[16], (DEPTH, SSM_D_INNER), 0.01),
        'ssm_w_out': nrm(ks[17], (DEPTH, SSM_D_INNER, D_MODEL), SSM_D_INNER ** -0.5),
        'w_o': nrm(ks[18], (DEPTH, D_MODEL, D_MODEL), D_MODEL ** -0.5),
        'norm_ffn': 1.0 + nrm(ks[19], (DEPTH, D_MODEL), 0.01),
        'router_w1': nrm(ks[20], (DEPTH, D_MODEL, MOE_GROUPS), D_MODEL ** -0.5),
        'router_b1': nrm(ks[21], (DEPTH, MOE_GROUPS), 0.01),
        'router_w2': nrm(ks[22], (DEPTH, D_MODEL, MOE_EXPERTS), D_MODEL ** -0.5),
        'router_b2': nrm(ks[23], (DEPTH, MOE_EXPERTS), 0.01),
        'exp_w_gate': nrm(ks[24], (DEPTH, MOE_EXPERTS, D_MODEL, MOE_D_FF), D_MODEL ** -0.5),
        'exp_w_up': nrm(ks[25], (DEPTH, MOE_EXPERTS, D_MODEL, MOE_D_FF), D_MODEL ** -0.5),
        'exp_w_down': nrm(ks[26], (DEPTH, MOE_EXPERTS, MOE_D_FF, D_MODEL), MOE_D_FF ** -0.5),
        'norm_final': 1.0 + nrm(ks[27], (D_MODEL,), 0.01),
    }


def reference(x_prompt, x_sample, state_gla, state_ssm, state_conv, norm_mix, w_in, gla_w_gate2, gla_b_gate,
              gla_norm, gla_w_out, ssm_conv_w, ssm_conv_b, ssm_dt_bias, ssm_a_log, ssm_d, ssm_norm, ssm_w_out,
              w_o, norm_ffn, router_w1, router_b1, router_w2, router_b2, exp_w_gate, exp_w_up, exp_w_down,
              norm_final):
    xp, xs = x_prompt, x_sample
    nbp = x_prompt.shape[0]
    gla_p, ssm_p, conv_p, gla_s, ssm_s, conv_s = [], [], [], [], [], []
    for l in range(DEPTH):
        lw = (norm_mix[l], w_in[l], gla_w_gate2[l], gla_b_gate[l], gla_norm[l], gla_w_out[l],
              ssm_conv_w[l], ssm_conv_b[l], ssm_dt_bias[l], ssm_a_log[l], ssm_d[l], ssm_norm[l], ssm_w_out[l],
              w_o[l], norm_ffn[l], router_w1[l], router_b1[l], router_w2[l], router_b2[l],
              exp_w_gate[l], exp_w_up[l], exp_w_down[l])
        z_gla = jnp.zeros((nbp, GLA_HEADS, GLA_DK, GLA_DV), x_prompt.dtype)
        z_ssm = jnp.zeros((nbp, SSM_HEADS, SSM_HEAD_DIM, SSM_D_STATE), x_prompt.dtype)
        z_conv = jnp.zeros((nbp, SSM_CONV - 1, SSM_CONV_DIM), x_prompt.dtype)
        xp, g1, s1, c1 = _layer(xp, z_gla, z_ssm, z_conv, *lw)
        xs, g2, s2, c2 = _layer(xs, state_gla[l], state_ssm[l], state_conv[l], *lw)
        gla_p.append(g1); ssm_p.append(s1); conv_p.append(c1)
        gla_s.append(g2); ssm_s.append(s2); conv_s.append(c2)
    y_prompt = _rmsnorm(xp, norm_final)
    y_sample = _rmsnorm(xs, norm_final)
    return (y_prompt, y_sample, jnp.stack(gla_p), jnp.stack(ssm_p), jnp.stack(conv_p),
            jnp.stack(gla_s), jnp.stack(ssm_s), jnp.stack(conv_s))
```

```python
import functools

import jax
import jax.numpy as jnp
from jax import lax
from jax.experimental import pallas as pl
from jax.experimental.pallas import tpu as pltpu

f32 = jnp.float32
bf16 = jnp.bfloat16
HIGHEST = lax.Precision.HIGHEST

D_MODEL = 2048
DEPTH = 2
EPS = 1e-6
GLA_HEADS = 4
GLA_DK = 256
GLA_DV = 512
GLA_GATE_RANK = 16
GLA_GATE_TAU = 16.0
GLA_SUB = 16
SSM_D_INNER = 4096
SSM_HEAD_DIM = 64
SSM_HEADS = 64
SSM_GROUPS = 8
SSM_HPG = 8
SSM_GROUP_W = SSM_HPG * SSM_HEAD_DIM
SSM_D_STATE = 128
SSM_CONV = 4
SSM_CONV_DIM = 6144
MOE_GROUPS = 4
MOE_PER_GROUP = 4
MOE_EXPERTS = 16
MOE_D_FF = 512
LANES = 128
CONV_PAD = 8

W_GLR = 6144
W_Z = 6160
W_DT = 16400
W_MG = 16464
P_Q, P_K, P_V, P_R, P_Z, P_X, P_B, P_C, P_GA, P_GB, P_END = (
    0, 1024, 2048, 4096, 6144, 10240, 14336, 15360, 16384, 18432, 20480)
S_GLR = 64

VMEM_LIMIT = 56 * 1024 * 1024


def _sig(x):
    return 1.0 / (1.0 + jnp.exp(-x))


def _softplus(x):
    return jnp.maximum(x, 0.0) + jnp.log1p(jnp.exp(-jnp.abs(x)))


def _log_sigmoid(x):
    return jnp.minimum(x, 0.0) - jnp.log1p(jnp.exp(-jnp.abs(x)))


def _params(*sem):
    return pltpu.CompilerParams(dimension_semantics=sem, vmem_limit_bytes=VMEM_LIMIT)


def _norm_body(x_ref, g_ref, o_ref):
    x = x_ref[...]
    y = x * lax.rsqrt(jnp.mean(x * x, axis=-1, keepdims=True) + EPS)
    o_ref[...] = (y * g_ref[...]).astype(o_ref.dtype)


def _rmsnorm(x, g, *, rows, row0, tm, out_dtype):
    blk0 = row0 // tm
    return pl.pallas_call(
        _norm_body, grid=(rows // tm,),
        in_specs=[pl.BlockSpec((tm, D_MODEL), lambda m: (blk0 + m, 0)),
                  pl.BlockSpec((1, D_MODEL), lambda m: (0, 0))],
        out_specs=pl.BlockSpec((tm, D_MODEL), lambda m: (m, 0)),
        out_shape=jax.ShapeDtypeStruct((rows, D_MODEL), out_dtype),
        compiler_params=_params("arbitrary"), name="rmsnorm")(x, g.reshape(1, D_MODEL))


def _mm_body(a_ref, w_ref, o_ref):
    o_ref[...] = jnp.dot(a_ref[...], w_ref[...], preferred_element_type=f32).astype(o_ref.dtype)


def _matmul(a, w, *, tm, tn, out_dtype, name):
    m, k = a.shape
    n = w.shape[1]
    return pl.pallas_call(
        _mm_body, grid=(n // tn, m // tm),
        in_specs=[pl.BlockSpec((tm, k), lambda j, i: (i, 0)),
                  pl.BlockSpec((k, tn), lambda j, i: (0, j))],
        out_specs=pl.BlockSpec((tm, tn), lambda j, i: (i, j)),
        out_shape=jax.ShapeDtypeStruct((m, n), out_dtype),
        compiler_params=_params("arbitrary", "arbitrary"), name=name)(a, w)


def _merge_body(oa_ref, ob_ref, ga_ref, gb_ref, wa_ref, wb_ref, o_ref):
    ya = jnp.dot(oa_ref[...], wa_ref[...], preferred_element_type=f32)
    yb = jnp.dot(ob_ref[...], wb_ref[...], preferred_element_type=f32)
    o_ref[...] = (_sig(ga_ref[...]) * ya + _sig(gb_ref[...]) * yb).astype(o_ref.dtype)


def _merge(oa, ob, proj, wa, wb, *, tm, tn):
    m = oa.shape[0]
    ga0, gb0 = P_GA // tn, P_GB // tn
    return pl.pallas_call(
        _merge_body, grid=(D_MODEL // tn, m // tm),
        in_specs=[pl.BlockSpec((tm, GLA_HEADS * GLA_DV), lambda j, i: (i, 0)),
                  pl.BlockSpec((tm, SSM_D_INNER), lambda j, i: (i, 0)),
                  pl.BlockSpec((tm, tn), lambda j, i: (i, ga0 + j)),
                  pl.BlockSpec((tm, tn), lambda j, i: (i, gb0 + j)),
                  pl.BlockSpec((GLA_HEADS * GLA_DV, tn), lambda j, i: (0, j)),
                  pl.BlockSpec((SSM_D_INNER, tn), lambda j, i: (0, j))],
        out_specs=pl.BlockSpec((tm, tn), lambda j, i: (i, j)),
        out_shape=jax.ShapeDtypeStruct((m, D_MODEL), bf16),
        compiler_params=_params("arbitrary", "arbitrary"), name="merge")(oa, ob, proj, proj, wa, wb)


def _route(logits):
    c = lax.broadcasted_iota(jnp.int32, logits.shape, 1)
    big = jnp.int32(1 << 20)
    is_group = c < MOE_GROUPS
    lg = jnp.where(is_group, logits, -jnp.inf)
    top = jnp.max(lg, axis=1, keepdims=True)
    gsel = jnp.min(jnp.where(is_group & (logits == top), c, big), axis=1, keepdims=True)
    p_sel = 1.0 / jnp.sum(jnp.exp(lg - top), axis=1, keepdims=True)
    in_group = ((c >= MOE_GROUPS) & (c < MOE_GROUPS + MOE_EXPERTS)
                & (((c - MOE_GROUPS) // MOE_PER_GROUP) == gsel))
    le = jnp.where(in_group, logits, -jnp.inf)
    tv0 = jnp.max(le, axis=1, keepdims=True)
    i0 = jnp.min(jnp.where(in_group & (logits == tv0), c, big), axis=1, keepdims=True)
    rest = in_group & (c != i0)
    le1 = jnp.where(rest, logits, -jnp.inf)
    tv1 = jnp.max(le1, axis=1, keepdims=True)
    i1 = jnp.min(jnp.where(rest & (logits == tv1), c, big), axis=1, keepdims=True)
    t = jnp.exp(tv1 - tv0)
    w0 = p_sel / (1.0 + t)
    w1 = p_sel * t / (1.0 + t)
    return jnp.where(c == i0, w0, jnp.where(c == i1, w1, 0.0))


def _outproj_body(m_ref, wo_ref, x_ref, g_ref, wr_ref, br_ref, xn_ref, u_ref, cw_ref):
    xn = x_ref[...] + jnp.dot(m_ref[...], wo_ref[...], preferred_element_type=f32)
    xn_ref[...] = xn
    u = xn * lax.rsqrt(jnp.mean(xn * xn, axis=-1, keepdims=True) + EPS) * g_ref[...]
    u_ref[...] = u.astype(bf16)
    logits = jnp.dot(u, wr_ref[...], precision=HIGHEST, preferred_element_type=f32) + br_ref[...]
    cw_ref[...] = _route(logits)


def _outproj(merged, wo, x, g, wr, br, *, tm):
    m = x.shape[0]
    row = lambda i: (i, 0)
    fix = lambda i: (0, 0)
    return pl.pallas_call(
        _outproj_body, grid=(m // tm,),
        in_specs=[pl.BlockSpec((tm, D_MODEL), row), pl.BlockSpec((D_MODEL, D_MODEL), fix),
                  pl.BlockSpec((tm, D_MODEL), row), pl.BlockSpec((1, D_MODEL), fix),
                  pl.BlockSpec((D_MODEL, LANES), fix), pl.BlockSpec((1, LANES), fix)],
        out_specs=[pl.BlockSpec((tm, D_MODEL), row), pl.BlockSpec((tm, D_MODEL), row),
                   pl.BlockSpec((tm, LANES), row)],
        out_shape=[jax.ShapeDtypeStruct((m, D_MODEL), f32), jax.ShapeDtypeStruct((m, D_MODEL), bf16),
                   jax.ShapeDtypeStruct((m, LANES), f32)],
        compiler_params=_params("arbitrary"), name="outproj")(merged, wo, x, g, wr, br)


def _moe_body(u_ref, cw_ref, xn_ref, wg_ref, wu_ref, wd_ref, o_ref):
    e = pl.program_id(1)

    @pl.when(e == 0)
    def _():
        o_ref[...] = xn_ref[...]

    u = u_ref[...]
    hg = jnp.dot(u, wg_ref[0], preferred_element_type=f32)
    hu = jnp.dot(u, wu_ref[0], preferred_element_type=f32)
    cw = cw_ref[...]
    c = lax.broadcasted_iota(jnp.int32, cw.shape, 1)
    cwe = jnp.sum(jnp.where(c == e + MOE_GROUPS, cw, 0.0), axis=1, keepdims=True)
    h = (hg * _sig(hg)) * hu * cwe
    o_ref[...] += jnp.dot(h.astype(bf16), wd_ref[0], preferred_element_type=f32)


def _moe(u, cw, xn, wg, wu, wd, *, tm):
    m = u.shape[0]
    row = lambda i, e: (i, 0)
    ex = lambda i, e: (e, 0, 0)
    return pl.pallas_call(
        _moe_body, grid=(m // tm, MOE_EXPERTS),
        in_specs=[pl.BlockSpec((tm, D_MODEL), row), pl.BlockSpec((tm, LANES), row),
                  pl.BlockSpec((tm, D_MODEL), row),
                  pl.BlockSpec((1, D_MODEL, MOE_D_FF), ex), pl.BlockSpec((1, D_MODEL, MOE_D_FF), ex),
                  pl.BlockSpec((1, MOE_D_FF, D_MODEL), ex)],
        out_specs=pl.BlockSpec((tm, D_MODEL), row),
        out_shape=jax.ShapeDtypeStruct((m, D_MODEL), f32),
        compiler_params=_params("arbitrary", "arbitrary"), name="moe")(u, cw, xn, wg, wu, wd)


def _gla_body(q_ref, k_ref, v_ref, r_ref, sm_ref, w2_ref, bg_ref, gn_ref, s0_ref, o_ref, sout_ref,
              st_ref, b_ref, att_ref, *, chunk, nchunks):
    t = pl.program_id(2)

    @pl.when(t == 0)
    def _():
        st_ref[...] = s0_ref[0, 0]

    q = q_ref[...] * (GLA_DK ** -0.5)
    k = k_ref[...]
    vb = v_ref[...].astype(bf16)
    glr = sm_ref[:, S_GLR:S_GLR + GLA_GATE_RANK].astype(bf16)
    pre = jnp.dot(glr, w2_ref[...].astype(bf16), preferred_element_type=f32) + bg_ref[...]
    g = _log_sigmoid(pre) * (1.0 / GLA_GATE_TAU)
    row = lax.broadcasted_iota(jnp.int32, (chunk, chunk), 0)
    col = lax.broadcasted_iota(jnp.int32, (chunk, chunk), 1)
    b = jnp.dot((row >= col).astype(f32), g, precision=HIGHEST, preferred_element_type=f32)
    b_ref[...] = b
    b_last = b_ref[chunk - 1:chunk, :]

    state = st_ref[...]
    o = jnp.dot((q * jnp.exp(b)).astype(bf16), state.astype(bf16), preferred_element_type=f32)

    att_ref[...] = jnp.zeros((chunk, chunk), f32)
    sub_row = lax.broadcasted_iota(jnp.int32, (GLA_SUB, GLA_DK), 0)
    sub_col = lax.broadcasted_iota(jnp.int32, (GLA_SUB, GLA_SUB), 1)
    for blk in range(chunk // GLA_SUB):
        r0 = blk * GLA_SUB
        q_i = q[r0:r0 + GLA_SUB]
        b_i = b[r0:r0 + GLA_SUB]
        if blk > 0:
            c_i = b_ref[r0 - 1:r0, :]
            q_ref_i = (q_i * jnp.exp(b_i - c_i)).astype(bf16)
            k_prev = (k[:r0] * jnp.exp(c_i - b[:r0])).astype(bf16)
            att_ref[r0:r0 + GLA_SUB, 0:r0] = lax.dot_general(
                q_ref_i, k_prev, (((1,), (1,)), ((), ())), preferred_element_type=f32)
        diag = jnp.zeros((GLA_SUB, GLA_SUB), f32)
        for j in range(GLA_SUB):
            b_j = b_ref[r0 + j:r0 + j + 1, :]
            k_j = k_ref[r0 + j:r0 + j + 1, :]
            decay = jnp.exp(jnp.where(sub_row >= j, b_i - b_j, -jnp.inf))
            a_j = jnp.sum(q_i * decay * k_j, axis=1, keepdims=True)
            diag = jnp.where(sub_col == j, a_j, diag)
        att_ref[r0:r0 + GLA_SUB, r0:r0 + GLA_SUB] = diag

    o = o + jnp.dot(att_ref[...].astype(bf16), vb, preferred_element_type=f32)

    k_dec = (k * jnp.exp(b_last - b)).astype(bf16)
    upd = lax.dot_general(k_dec, vb, (((0,), (0,)), ((), ())), preferred_element_type=f32)
    dec_col = jnp.transpose(jnp.broadcast_to(jnp.exp(b_last), (8, GLA_DK)))[:, 0:1]
    st_ref[...] = state * dec_col + upd

    o = o * lax.rsqrt(jnp.mean(o * o, axis=1, keepdims=True) + EPS) * gn_ref[...]
    r = r_ref[...]
    o_ref[...] = (o * (r * _sig(r))).astype(o_ref.dtype)

    @pl.when(t == nchunks - 1)
    def _():
        sout_ref[0, 0] = st_ref[...]


def _gla(proj, small, w2, bg, gn, s0, *, row0, nseq, seqlen, chunk):
    nchunks = seqlen // chunk
    blk0 = row0 // chunk
    rows = lambda b, h, t: blk0 + b * nchunks + t
    qw, vw = GLA_DK, GLA_DV
    return pl.pallas_call(
        functools.partial(_gla_body, chunk=chunk, nchunks=nchunks),
        grid=(nseq, GLA_HEADS, nchunks),
        in_specs=[pl.BlockSpec((chunk, qw), lambda b, h, t: (rows(b, h, t), P_Q // qw + h)),
                  pl.BlockSpec((chunk, qw), lambda b, h, t: (rows(b, h, t), P_K // qw + h)),
                  pl.BlockSpec((chunk, vw), lambda b, h, t: (rows(b, h, t), P_V // vw + h)),
                  pl.BlockSpec((chunk, vw), lambda b, h, t: (rows(b, h, t), P_R // vw + h)),
                  pl.BlockSpec((chunk, LANES), lambda b, h, t: (rows(b, h, t), 0)),
                  pl.BlockSpec((GLA_GATE_RANK, qw), lambda b, h, t: (0, h)),
                  pl.BlockSpec((1, qw), lambda b, h, t: (0, h)),
                  pl.BlockSpec((1, vw), lambda b, h, t: (0, 0)),
                  pl.BlockSpec((1, 1, qw, vw), lambda b, h, t: (b, h, 0, 0))],
        out_specs=[pl.BlockSpec((chunk, vw), lambda b, h, t: (b * nchunks + t, h)),
                   pl.BlockSpec((1, 1, qw, vw), lambda b, h, t: (b, h, 0, 0))],
        out_shape=[jax.ShapeDtypeStruct((nseq * seqlen, GLA_HEADS * vw), bf16),
                   jax.ShapeDtypeStruct((nseq, GLA_HEADS, qw, vw), f32)],
        scratch_shapes=[pltpu.VMEM((qw, vw), f32), pltpu.VMEM((chunk, qw), f32),
                        pltpu.VMEM((chunk, chunk), f32)],
        compiler_params=_params("arbitrary", "arbitrary", "arbitrary"),
        name="gla")(proj, proj, proj, proj, small, w2, bg, gn, s0)


def _ssd_body(z_ref, x_ref, bm_ref, cm_ref, sm_ref, wx_ref, wb_ref, wc_ref, bx_ref, bb_ref, bc_ref,
              sx_ref, sb_ref, sc_ref, dtb_ref, alog_ref, dskip_ref, nrm_ref, h0_ref, y_ref, hout_ref,
              h_ref, xpx_ref, xpb_ref, xpc_ref, *, chunk, nchunks):
    grp = pl.program_id(1)
    t = pl.program_id(2)
    hist = CONV_PAD - (SSM_CONV - 1)

    @pl.when(t == 0)
    def _():
        h_ref[...] = h0_ref[0, 0]
        xpx_ref[hist:CONV_PAD, :] = sx_ref[0]
        xpb_ref[hist:CONV_PAD, :] = sb_ref[0]
        xpc_ref[hist:CONV_PAD, :] = sc_ref[0]

    def conv_silu(xp_ref, cur_ref, w_ref, bias_ref):
        xp_ref[CONV_PAD:CONV_PAD + chunk, :] = cur_ref[...]
        acc = bias_ref[...] + xp_ref[pl.ds(hist, chunk), :] * w_ref[0:1, :]
        for i in range(1, SSM_CONV):
            acc = acc + xp_ref[pl.ds(hist + i, chunk), :] * w_ref[i:i + 1, :]
        xp_ref[hist:CONV_PAD, :] = xp_ref[chunk + hist:chunk + CONV_PAD, :]
        return acc * _sig(acc)

    xs = conv_silu(xpx_ref, x_ref, wx_ref, bx_ref)
    bmat = conv_silu(xpb_ref, bm_ref, wb_ref, bb_ref).astype(bf16)
    cmat = conv_silu(xpc_ref, cm_ref, wc_ref, bc_ref).astype(bf16)

    src = lax.broadcasted_iota(jnp.int32, (LANES, LANES), 0)
    dst = lax.broadcasted_iota(jnp.int32, (LANES, LANES), 1)
    pick = ((src == grp * SSM_HPG + dst) & (dst < SSM_HPG)).astype(f32)
    dt_all = _softplus(sm_ref[...] + dtb_ref[...])
    a_neg = -jnp.exp(alog_ref[...])
    dt_g = jnp.dot(dt_all, pick, precision=HIGHEST, preferred_element_type=f32)
    la_g = jnp.dot(dt_all * a_neg, pick, precision=HIGHEST, preferred_element_type=f32)
    row = lax.broadcasted_iota(jnp.int32, (chunk, chunk), 0)
    col = lax.broadcasted_iota(jnp.int32, (chunk, chunk), 1)
    causal = row >= col
    cum = jnp.dot(causal.astype(f32), la_g, precision=HIGHEST, preferred_element_type=f32)
    cum_t = cum.T
    dt_t = dt_g.T
    last = cum[chunk - 1:chunk, :]

    head = lax.broadcasted_iota(jnp.int32, (LANES, SSM_GROUP_W), 0)
    chan = lax.broadcasted_iota(jnp.int32, (LANES, SSM_GROUP_W), 1)
    widen = (chan // SSM_HEAD_DIM == head).astype(f32)
    w_full = jnp.dot(jnp.exp(last - cum) * dt_g, widen, precision=HIGHEST, preferred_element_type=f32)
    ecum_full = jnp.dot(jnp.exp(cum), widen, precision=HIGHEST, preferred_element_type=f32)

    nt = (((1,), (1,)), ((), ()))
    cb = lax.dot_general(cmat, bmat, nt, preferred_element_type=f32)
    state = h_ref[...]
    y = lax.dot_general(cmat, state.astype(bf16), nt, preferred_element_type=f32) * ecum_full
    xsb = xs.astype(bf16)
    parts = []
    for j in range(SSM_HPG):
        seg = jnp.exp(jnp.where(causal, cum[:, j:j + 1] - cum_t[j:j + 1, :], -jnp.inf))
        m = (cb * seg * dt_t[j:j + 1, :]).astype(bf16)
        parts.append(jnp.dot(m, xsb[:, j * SSM_HEAD_DIM:(j + 1) * SSM_HEAD_DIM],
                             preferred_element_type=f32))
    y = y + jnp.concatenate(parts, axis=1)

    upd = lax.dot_general((xs * w_full).astype(bf16), bmat, (((0,), (0,)), ((), ())),
                          preferred_element_type=f32)
    for j in range(SSM_HPG):
        rs = slice(j * SSM_HEAD_DIM, (j + 1) * SSM_HEAD_DIM)
        h_ref[rs, :] = state[rs, :] * jnp.exp(last[:, j:j + 1]) + upd[rs, :]

    y = y + dskip_ref[...] * xs
    z = z_ref[...]
    y = y * (z * _sig(z))
    y = y * lax.rsqrt(jnp.mean(y * y, axis=1, keepdims=True) + EPS) * nrm_ref[...]
    y_ref[...] = y.astype(y_ref.dtype)

    @pl.when(t == nchunks - 1)
    def _():
        hout_ref[0, 0] = h_ref[...]


def _ssd(proj, small, conv_w, conv_b, conv_s, dtb, alog, dskip, nrm, h0, *, row0, nseq, seqlen, chunk):
    nchunks = seqlen // chunk
    blk0 = row0 // chunk
    gw, sw = SSM_GROUP_W, SSM_D_STATE
    xb, bb, cb = 0, SSM_D_INNER // sw, (SSM_D_INNER + SSM_GROUPS * sw) // sw
    rows = lambda b, g, t: blk0 + b * nchunks + t
    taps, hist = SSM_CONV, SSM_CONV - 1
    in_specs = [
        pl.BlockSpec((chunk, gw), lambda b, g, t: (rows(b, g, t), P_Z // gw + g)),
        pl.BlockSpec((chunk, gw), lambda b, g, t: (rows(b, g, t), P_X // gw + g)),
        pl.BlockSpec((chunk, sw), lambda b, g, t: (rows(b, g, t), P_B // sw + g)),
        pl.BlockSpec((chunk, sw), lambda b, g, t: (rows(b, g, t), P_C // sw + g)),
        pl.BlockSpec((chunk, LANES), lambda b, g, t: (rows(b, g, t), 0)),
        pl.BlockSpec((taps, gw), lambda b, g, t: (0, xb + g)),
        pl.BlockSpec((taps, sw), lambda b, g, t: (0, bb + g)),
        pl.BlockSpec((taps, sw), lambda b, g, t: (0, cb + g)),
        pl.BlockSpec((1, gw), lambda b, g, t: (0, xb + g)),
        pl.BlockSpec((1, sw), lambda b, g, t: (0, bb + g)),
        pl.BlockSpec((1, sw), lambda b, g, t: (0, cb + g)),
        pl.BlockSpec((1, hist, gw), lambda b, g, t: (b, 0, xb + g)),
        pl.BlockSpec((1, hist, sw), lambda b, g, t: (b, 0, bb + g)),
        pl.BlockSpec((1, hist, sw), lambda b, g, t: (b, 0, cb + g)),
        pl.BlockSpec((1, LANES), lambda b, g, t: (0, 0)),
        pl.BlockSpec((1, LANES), lambda b, g, t: (0, 0)),
        pl.BlockSpec((1, gw), lambda b, g, t: (0, g)),
        pl.BlockSpec((1, gw), lambda b, g, t: (0, g)),
        pl.BlockSpec((1, 1, gw, sw), lambda b, g, t: (b, g, 0, 0)),
    ]
    return pl.pallas_call(
        functools.partial(_ssd_body, chunk=chunk, nchunks=nchunks),
        grid=(nseq, SSM_GROUPS, nchunks),
        in_specs=in_specs,
        out_specs=[pl.BlockSpec((chunk, gw), lambda b, g, t: (b * nchunks + t, g)),
                   pl.BlockSpec((1, 1, gw, sw), lambda b, g, t: (b, g, 0, 0))],
        out_shape=[jax.ShapeDtypeStruct((nseq * seqlen, SSM_D_INNER), bf16),
                   jax.ShapeDtypeStruct((nseq, SSM_GROUPS, gw, sw), f32)],
        scratch_shapes=[pltpu.VMEM((gw, sw), f32), pltpu.VMEM((chunk + CONV_PAD, gw), f32),
                        pltpu.VMEM((chunk + CONV_PAD, sw), f32), pltpu.VMEM((chunk + CONV_PAD, sw), f32)],
        compiler_params=_params("arbitrary", "arbitrary", "arbitrary"),
        name="ssd")(proj, proj, proj, proj, small, conv_w, conv_w, conv_w, conv_b, conv_b, conv_b,
                    conv_s, conv_s, conv_s, dtb, alog, dskip, nrm, h0)


def _pad_lanes(v):
    return jnp.pad(v.reshape(1, -1), ((0, 0), (0, LANES - v.shape[-1])))


def kernel(x_prompt, x_sample, state_gla, state_ssm, state_conv, norm_mix, w_in, gla_w_gate2, gla_b_gate,
           gla_norm, gla_w_out, ssm_conv_w, ssm_conv_b, ssm_dt_bias, ssm_a_log, ssm_d, ssm_norm, ssm_w_out,
           w_o, norm_ffn, router_w1, router_b1, router_w2, router_b2, exp_w_gate, exp_w_up, exp_w_down,
           norm_final):
    n_p, t_p = x_prompt.shape[:2]
    n_s, t_s = x_sample.shape[:2]
    assert n_p == 1
    rows_p, rows_s = n_p * t_p, n_s * t_s
    m = rows_p + rows_s
    tm = 832 if m % 832 == 0 else 128
    gla_chunk_p, ssd_chunk_p = min(64, t_p), min(256, t_p)

    x = jnp.concatenate([x_prompt.reshape(rows_p, D_MODEL), x_sample.reshape(rows_s, D_MODEL)], axis=0)
    zeros_gla = jnp.zeros((n_p, GLA_HEADS, GLA_DK, GLA_DV), f32)
    zeros_ssm = jnp.zeros((n_p, SSM_GROUPS, SSM_GROUP_W, SSM_D_STATE), f32)
    zeros_conv = jnp.zeros((n_p, SSM_CONV - 1, SSM_CONV_DIM), f32)

    outs = {k: [] for k in ("gla_p", "ssm_p", "conv_p", "gla_s", "ssm_s", "conv_s")}
    for l in range(DEPTH):
        wl = w_in[l]
        w_main = jnp.concatenate([wl[:, :W_GLR], wl[:, W_Z:W_DT], wl[:, W_MG:]], axis=1).astype(bf16)
        w_small = jnp.concatenate(
            [wl[:, W_DT:W_MG], wl[:, W_GLR:W_Z],
             jnp.zeros((D_MODEL, LANES - SSM_HEADS - GLA_GATE_RANK), f32)], axis=1).astype(bf16)

        u = _rmsnorm(x, norm_mix[l], rows=m, row0=0, tm=tm, out_dtype=bf16)
        proj = _matmul(u, w_main, tm=tm, tn=1024, out_dtype=f32, name="inproj")
        small = _matmul(u, w_small, tm=tm, tn=LANES, out_dtype=f32, name="inproj_small")

        gla_args = (proj, small, gla_w_gate2[l], gla_b_gate[l].reshape(1, -1), gla_norm[l].reshape(1, -1))
        oa_p, g_p = _gla(*gla_args, zeros_gla, row0=0, nseq=n_p, seqlen=t_p, chunk=gla_chunk_p)
        oa_s, g_s = _gla(*gla_args, state_gla[l], row0=rows_p, nseq=n_s, seqlen=t_s, chunk=t_s)

        ssd_args = (proj, small, ssm_conv_w[l], ssm_conv_b[l].reshape(1, -1))
        ssd_tail = (_pad_lanes(ssm_dt_bias[l]), _pad_lanes(ssm_a_log[l]),
                    jnp.repeat(ssm_d[l], SSM_HEAD_DIM).reshape(1, -1), ssm_norm[l].reshape(1, -1))
        h0_s = state_ssm[l].reshape(n_s, SSM_GROUPS, SSM_GROUP_W, SSM_D_STATE)
        ob_p, h_p = _ssd(*ssd_args, zeros_conv, *ssd_tail, zeros_ssm,
                         row0=0, nseq=n_p, seqlen=t_p, chunk=ssd_chunk_p)
        ob_s, h_s = _ssd(*ssd_args, state_conv[l], *ssd_tail, h0_s,
                         row0=rows_p, nseq=n_s, seqlen=t_s, chunk=t_s)

        oa = jnp.concatenate([oa_p, oa_s], axis=0)
        ob = jnp.concatenate([ob_p, ob_s], axis=0)
        merged = _merge(oa, ob, proj, gla_w_out[l].astype(bf16), ssm_w_out[l].astype(bf16), tm=tm, tn=512)

        wr = jnp.concatenate([router_w1[l], router_w2[l],
                              jnp.zeros((D_MODEL, LANES - MOE_GROUPS - MOE_EXPERTS), f32)], axis=1)
        br = _pad_lanes(jnp.concatenate([router_b1[l], router_b2[l]]))
        xn, u2, cw = _outproj(merged, w_o[l].astype(bf16), x, norm_ffn[l].reshape(1, -1), wr, br, tm=tm)
        x = _moe(u2, cw, xn, exp_w_gate[l].astype(bf16), exp_w_up[l].astype(bf16),
                 exp_w_down[l].astype(bf16), tm=tm)

        xbc = proj[:, P_X:P_X + SSM_CONV_DIM]
        outs["gla_p"].append(g_p)
        outs["gla_s"].append(g_s)
        outs["ssm_p"].append(h_p.reshape(n_p, SSM_HEADS, SSM_HEAD_DIM, SSM_D_STATE))
        outs["ssm_s"].append(h_s.reshape(n_s, SSM_HEADS, SSM_HEAD_DIM, SSM_D_STATE))
        outs["conv_p"].append(xbc[:rows_p].reshape(n_p, t_p, -1)[:, t_p - (SSM_CONV - 1):])
        outs["conv_s"].append(xbc[rows_p:].reshape(n_s, t_s, -1)[:, t_s - (SSM_CONV - 1):])

    tm_s = rows_s if rows_p % rows_s == 0 else 8
    y_p = _rmsnorm(x, norm_final, rows=rows_p, row0=0, tm=tm if rows_p % tm == 0 else 128, out_dtype=f32)
    y_s = _rmsnorm(x, norm_final, rows=rows_s, row0=rows_p, tm=tm_s, out_dtype=f32)
    return (y_p.reshape(n_p, t_p, D_MODEL), y_s.reshape(n_s, t_s, D_MODEL),
            jnp.stack(outs["gla_p"]), jnp.stack(outs["ssm_p"]), jnp.stack(outs["conv_p"]),
            jnp.stack(outs["gla_s"]), jnp.stack(outs["ssm_s"]), jnp.stack(outs["conv_s"]))
```

```python
import functools
import math

import jax
import jax.numpy as jnp
from jax import lax
from jax.experimental import pallas as pl
from jax.experimental.pallas import tpu as pltpu

f32 = jnp.float32
bf16 = jnp.bfloat16
HIGHEST = lax.Precision.HIGHEST
LOG2E = math.log2(math.e)

D_MODEL = 2048
DEPTH = 2
EPS = 1e-6
GLA_HEADS = 4
GLA_DK = 256
GLA_DV = 512
GLA_QK = GLA_HEADS * GLA_DK
GLA_VW = GLA_HEADS * GLA_DV
GLA_GATE_RANK = 16
GLA_GATE_TAU = 16.0
GLA_SUB = 16
SSM_D_INNER = 4096
SSM_HEAD_DIM = 64
SSM_HEADS = 64
SSM_GROUPS = 8
SSM_HPG = 8
SSM_GROUP_W = SSM_HPG * SSM_HEAD_DIM
SSM_D_STATE = 128
SSM_BC = SSM_GROUPS * SSM_D_STATE
SSM_CONV = 4
SSM_CONV_DIM = 6144
MOE_GROUPS = 4
MOE_PER_GROUP = 4
MOE_EXPERTS = 16
MOE_D_FF = 512
LANES = 128
CONV_PAD = 8

W_GLR = 6144
W_Z = 6160
W_DT = 16400
W_MG = 16464
IN_TN = 1024
IN_TILES = 20
IN_ALIGNED_TILES = W_GLR // IN_TN
IN_MG_TILE0 = (W_DT - W_Z + W_GLR) // IN_TN
IN_SHIFT_Z = W_Z - W_GLR
IN_SHIFT_MG = W_MG - IN_MG_TILE0 * IN_TN
P_Z, P_X, P_Q, P_K, P_V, P_R, P_B, P_C, P_GA, P_GB, P_END = (
    0, 4096, 8192, 9216, 10240, 12288, 14336, 15360, 16384, 18432, 20480)
S_GLR = 64

VMEM_LIMIT = 56 * 1024 * 1024


def _sig(x):
    return 1.0 / (1.0 + jnp.exp(-x))


def _softplus(x):
    return jnp.maximum(x, 0.0) + jnp.log1p(jnp.exp(-jnp.abs(x)))


def _log_sigmoid(x):
    return jnp.minimum(x, 0.0) - jnp.log1p(jnp.exp(-jnp.abs(x)))


def _params(*sem):
    return pltpu.CompilerParams(dimension_semantics=sem, vmem_limit_bytes=VMEM_LIMIT)


def _split_bf16(x, pieces):
    out = []
    for _ in range(pieces - 1):
        p = x.astype(bf16)
        out.append(p)
        x = x - p.astype(f32)
    out.append(x.astype(bf16))
    return out


def _dot_exact01(a, b01, pieces, dims=None):
    acc = None
    for p in _split_bf16(a, pieces):
        if dims is None:
            d = jnp.dot(p, b01, preferred_element_type=f32)
        else:
            d = lax.dot_general(p, b01, dims, preferred_element_type=f32)
        acc = d if acc is None else acc + d
    return acc


def _cumsum_rows(x):
    n = x.shape[0]
    row = lax.broadcasted_iota(jnp.int32, (n, n), 0)
    col = lax.broadcasted_iota(jnp.int32, (n, n), 1)
    tri = (row >= col).astype(bf16)
    acc = None
    for p in _split_bf16(x, 3):
        d = jnp.dot(tri, p, preferred_element_type=f32)
        acc = d if acc is None else acc + d
    return acc


def _norm_body(x_ref, g_ref, o_ref):
    x = x_ref[...]
    y = x * lax.rsqrt(jnp.mean(x * x, axis=-1, keepdims=True) + EPS)
    o_ref[...] = (y * g_ref[...]).astype(o_ref.dtype)


def _rmsnorm(x, g, *, rows, row0, tm, out_dtype):
    blk0 = row0 // tm
    return pl.pallas_call(
        _norm_body, grid=(rows // tm,),
        in_specs=[pl.BlockSpec((tm, D_MODEL), lambda m: (blk0 + m, 0)),
                  pl.BlockSpec((1, D_MODEL), lambda m: (0, 0))],
        out_specs=pl.BlockSpec((tm, D_MODEL), lambda m: (m, 0)),
        out_shape=jax.ShapeDtypeStruct((rows, D_MODEL), out_dtype),
        compiler_params=_params("arbitrary"), name="rmsnorm")(x, g.reshape(1, D_MODEL))


def _inproj_body(u_ref, wa_ref, wb_ref, o_ref, wbf_ref):
    n = pl.program_id(0)

    @pl.when(pl.program_id(1) == 0)
    def _():
        def fill(shift):
            def f():
                if shift == 0:
                    wbf_ref[...] = wa_ref[...].astype(bf16)
                else:
                    wbf_ref[:, :IN_TN - shift] = wa_ref[:, shift:].astype(bf16)
                    wbf_ref[:, IN_TN - shift:] = wb_ref[:, :shift].astype(bf16)
            return f
        pl.when(n < IN_ALIGNED_TILES)(fill(0))
        pl.when((n >= IN_ALIGNED_TILES) & (n < IN_MG_TILE0))(fill(IN_SHIFT_Z))
        pl.when(n >= IN_MG_TILE0)(fill(IN_SHIFT_MG))

    o_ref[...] = jnp.dot(u_ref[...], wbf_ref[...], preferred_element_type=f32)


def _inproj_out_tile(n):
    lead = IN_ALIGNED_TILES
    zx = (P_Q - P_Z) // IN_TN
    return jnp.where(n < lead, n + zx, jnp.where(n < lead + zx, n - lead, n))


def _inproj(u, w_in, l, *, tm):
    m = u.shape[0]
    return pl.pallas_call(
        _inproj_body, grid=(IN_TILES, m // tm),
        in_specs=[pl.BlockSpec((tm, D_MODEL), lambda n, i: (i, 0)),
                  pl.BlockSpec((None, D_MODEL, IN_TN), lambda n, i: (l, 0, n)),
                  pl.BlockSpec((None, D_MODEL, LANES), lambda n, i: (l, 0, (n + 1) * (IN_TN // LANES)))],
        out_specs=pl.BlockSpec((tm, IN_TN), lambda n, i: (i, _inproj_out_tile(n))),
        out_shape=jax.ShapeDtypeStruct((m, P_END), f32),
        scratch_shapes=[pltpu.VMEM((D_MODEL, IN_TN), bf16)],
        compiler_params=_params("arbitrary", "arbitrary"), name="inproj")(u, w_in, w_in)


def _mm_body(a_ref, w_ref, o_ref):
    o_ref[...] = jnp.dot(a_ref[...], w_ref[...], preferred_element_type=f32).astype(o_ref.dtype)


def _matmul(a, w, *, tm, tn, out_dtype, name):
    m, k = a.shape
    n = w.shape[1]
    return pl.pallas_call(
        _mm_body, grid=(n // tn, m // tm),
        in_specs=[pl.BlockSpec((tm, k), lambda j, i: (i, 0)),
                  pl.BlockSpec((k, tn), lambda j, i: (0, j))],
        out_specs=pl.BlockSpec((tm, tn), lambda j, i: (i, j)),
        out_shape=jax.ShapeDtypeStruct((m, n), out_dtype),
        compiler_params=_params("arbitrary", "arbitrary"), name=name)(a, w)


def _merge_body(oa_ref, ob_ref, ga_ref, gb_ref, wa_ref, wb_ref, o_ref, wabf_ref, wbbf_ref):
    @pl.when(pl.program_id(1) == 0)
    def _():
        wabf_ref[...] = wa_ref[...].astype(bf16)
        wbbf_ref[...] = wb_ref[...].astype(bf16)

    ya = jnp.dot(oa_ref[...], wabf_ref[...], preferred_element_type=f32)
    yb = jnp.dot(ob_ref[...], wbbf_ref[...], preferred_element_type=f32)
    o_ref[...] = (_sig(ga_ref[...]) * ya + _sig(gb_ref[...]) * yb).astype(o_ref.dtype)


def _merge(oa, ob, proj, wa, wb, l, *, tm, tn):
    m = oa.shape[0]
    ga0, gb0 = P_GA // tn, P_GB // tn
    return pl.pallas_call(
        _merge_body, grid=(D_MODEL // tn, m // tm),
        in_specs=[pl.BlockSpec((tm, GLA_VW), lambda j, i: (i, 0)),
                  pl.BlockSpec((tm, SSM_D_INNER), lambda j, i: (i, 0)),
                  pl.BlockSpec((tm, tn), lambda j, i: (i, ga0 + j)),
                  pl.BlockSpec((tm, tn), lambda j, i: (i, gb0 + j)),
                  pl.BlockSpec((None, GLA_VW, tn), lambda j, i: (l, 0, j)),
                  pl.BlockSpec((None, SSM_D_INNER, tn), lambda j, i: (l, 0, j))],
        out_specs=pl.BlockSpec((tm, tn), lambda j, i: (i, j)),
        out_shape=jax.ShapeDtypeStruct((m, D_MODEL), bf16),
        scratch_shapes=[pltpu.VMEM((GLA_VW, tn), bf16), pltpu.VMEM((SSM_D_INNER, tn), bf16)],
        compiler_params=_params("arbitrary", "arbitrary"), name="merge")(oa, ob, proj, proj, wa, wb)


def _route(logits):
    c = lax.broadcasted_iota(jnp.int32, logits.shape, 1)
    big = jnp.int32(1 << 20)
    is_group = c < MOE_GROUPS
    lg = jnp.where(is_group, logits, -jnp.inf)
    top = jnp.max(lg, axis=1, keepdims=True)
    gsel = jnp.min(jnp.where(is_group & (logits == top), c, big), axis=1, keepdims=True)
    p_sel = 1.0 / jnp.sum(jnp.exp(lg - top), axis=1, keepdims=True)
    in_group = ((c >= MOE_GROUPS) & (c < MOE_GROUPS + MOE_EXPERTS)
                & (((c - MOE_GROUPS) // MOE_PER_GROUP) == gsel))
    le = jnp.where(in_group, logits, -jnp.inf)
    tv0 = jnp.max(le, axis=1, keepdims=True)
    i0 = jnp.min(jnp.where(in_group & (logits == tv0), c, big), axis=1, keepdims=True)
    rest = in_group & (c != i0)
    le1 = jnp.where(rest, logits, -jnp.inf)
    tv1 = jnp.max(le1, axis=1, keepdims=True)
    i1 = jnp.min(jnp.where(rest & (logits == tv1), c, big), axis=1, keepdims=True)
    t = jnp.exp(tv1 - tv0)
    w0 = p_sel / (1.0 + t)
    w1 = p_sel * t / (1.0 + t)
    return jnp.where(c == i0, w0, jnp.where(c == i1, w1, 0.0))


def _outproj_body(m_ref, wo_ref, x_ref, g_ref, wr_ref, br_ref, xn_ref, u_ref, cw_ref, wobf_ref):
    @pl.when(pl.program_id(0) == 0)
    def _():
        wobf_ref[...] = wo_ref[...].astype(bf16)

    xn = x_ref[...] + jnp.dot(m_ref[...], wobf_ref[...], preferred_element_type=f32)
    xn_ref[...] = xn
    u = xn * lax.rsqrt(jnp.mean(xn * xn, axis=-1, keepdims=True) + EPS) * g_ref[...]
    u_ref[...] = u.astype(bf16)
    logits = jnp.dot(u, wr_ref[...], precision=HIGHEST, preferred_element_type=f32) + br_ref[...]
    cw_ref[...] = _route(logits)


def _outproj(merged, wo, l, x, g, wr, br, *, tm):
    m = x.shape[0]
    row = lambda i: (i, 0)
    fix = lambda i: (0, 0)
    return pl.pallas_call(
        _outproj_body, grid=(m // tm,),
        in_specs=[pl.BlockSpec((tm, D_MODEL), row),
                  pl.BlockSpec((None, D_MODEL, D_MODEL), lambda i: (l, 0, 0), pipeline_mode=pl.Buffered(1)),
                  pl.BlockSpec((tm, D_MODEL), row), pl.BlockSpec((1, D_MODEL), fix),
                  pl.BlockSpec((D_MODEL, LANES), fix), pl.BlockSpec((1, LANES), fix)],
        out_specs=[pl.BlockSpec((tm, D_MODEL), row), pl.BlockSpec((tm, D_MODEL), row),
                   pl.BlockSpec((tm, LANES), row)],
        out_shape=[jax.ShapeDtypeStruct((m, D_MODEL), f32), jax.ShapeDtypeStruct((m, D_MODEL), bf16),
                   jax.ShapeDtypeStruct((m, LANES), f32)],
        scratch_shapes=[pltpu.VMEM((D_MODEL, D_MODEL), bf16)],
        compiler_params=_params("arbitrary"), name="outproj")(merged, wo, x, g, wr, br)


def _moe_body(u_ref, cw_ref, xn_ref, wg_ref, wu_ref, wd_ref, o_ref):
    e = pl.program_id(1)

    @pl.when(e == 0)
    def _():
        o_ref[...] = xn_ref[...]

    u = u_ref[...]
    hg = jnp.dot(u, wg_ref[0], preferred_element_type=f32)
    hu = jnp.dot(u, wu_ref[0], preferred_element_type=f32)
    cw = cw_ref[...]
    c = lax.broadcasted_iota(jnp.int32, cw.shape, 1)
    cwe = jnp.sum(jnp.where(c == e + MOE_GROUPS, cw, 0.0), axis=1, keepdims=True)
    h = (hg * _sig(hg)) * hu * cwe
    o_ref[...] += jnp.dot(h.astype(bf16), wd_ref[0], preferred_element_type=f32)


def _moe(u, cw, xn, wg, wu, wd, *, tm):
    m = u.shape[0]
    row = lambda i, e: (i, 0)
    ex = lambda i, e: (e, 0, 0)
    return pl.pallas_call(
        _moe_body, grid=(m // tm, MOE_EXPERTS),
        in_specs=[pl.BlockSpec((tm, D_MODEL), row), pl.BlockSpec((tm, LANES), row),
                  pl.BlockSpec((tm, D_MODEL), row),
                  pl.BlockSpec((1, D_MODEL, MOE_D_FF), ex), pl.BlockSpec((1, D_MODEL, MOE_D_FF), ex),
                  pl.BlockSpec((1, MOE_D_FF, D_MODEL), ex)],
        out_specs=pl.BlockSpec((tm, D_MODEL), row),
        out_shape=jax.ShapeDtypeStruct((m, D_MODEL), f32),
        compiler_params=_params("arbitrary", "arbitrary"), name="moe")(u, cw, xn, wg, wu, wd)


def _gla_body(q_ref, k_ref, v_ref, r_ref, sm_ref, w2_ref, bg_ref, gn_ref, s0_ref, prev_ref,
              o_ref, sout_ref, st_ref, b_ref, att_ref, *, chunk, nchunks):
    del prev_ref
    t = pl.program_id(1)

    @pl.when(t == 0)
    def _():
        st_ref[...] = s0_ref[0]

    glr = sm_ref[:, S_GLR:S_GLR + GLA_GATE_RANK].astype(bf16)
    pre = jnp.dot(glr, w2_ref[...].astype(bf16), preferred_element_type=f32) + bg_ref[...]
    b_ref[...] = _cumsum_rows(_log_sigmoid(pre) * (1.0 / GLA_GATE_TAU))

    sub_row = lax.broadcasted_iota(jnp.int32, (GLA_SUB, GLA_DK), 0)
    sub_col = lax.broadcasted_iota(jnp.int32, (GLA_SUB, GLA_SUB), 1)
    nt = (((1,), (1,)), ((), ()))
    tn = (((0,), (0,)), ((), ()))
    for h in range(GLA_HEADS):
        ks = slice(h * GLA_DK, (h + 1) * GLA_DK)
        vs = slice(h * GLA_DV, (h + 1) * GLA_DV)
        q = q_ref[:, ks] * (GLA_DK ** -0.5)
        k = k_ref[:, ks]
        vb = v_ref[:, vs].astype(bf16)
        b = b_ref[:, ks]
        b_last = b_ref[chunk - 1:chunk, ks]
        state = st_ref[h]
        o = jnp.dot((q * jnp.exp(b)).astype(bf16), state.astype(bf16), preferred_element_type=f32)

        att_ref[h] = jnp.zeros((chunk, chunk), f32)
        for blk in range(chunk // GLA_SUB):
            r0 = blk * GLA_SUB
            q_i = q[r0:r0 + GLA_SUB]
            b_i = b[r0:r0 + GLA_SUB]
            if blk > 0:
                c_i = b_ref[r0 - 1:r0, ks]
                q_dec = (q_i * jnp.exp(b_i - c_i)).astype(bf16)
                k_dec = (k[:r0] * jnp.exp(c_i - b[:r0])).astype(bf16)
                att_ref[h, r0:r0 + GLA_SUB, 0:r0] = lax.dot_general(q_dec, k_dec, nt,
                                                                   preferred_element_type=f32)
            diag = jnp.zeros((GLA_SUB, GLA_SUB), f32)
            for j in range(GLA_SUB):
                b_j = b_ref[r0 + j:r0 + j + 1, ks]
                k_j = k_ref[r0 + j:r0 + j + 1, ks]
                decay = jnp.exp(jnp.where(sub_row >= j, b_i - b_j, -jnp.inf))
                a_j = jnp.sum(q_i * decay * k_j, axis=1, keepdims=True)
                diag = jnp.where(sub_col == j, a_j, diag)
            att_ref[h, r0:r0 + GLA_SUB, r0:r0 + GLA_SUB] = diag

        o = o + jnp.dot(att_ref[h].astype(bf16), vb, preferred_element_type=f32)

        k_end = (k * jnp.exp(b_last - b)).astype(bf16)
        upd = lax.dot_general(k_end, vb, tn, preferred_element_type=f32)
        dec_col = jnp.transpose(jnp.broadcast_to(jnp.exp(b_last), (8, GLA_DK)))[:, 0:1]
        st_ref[h] = state * dec_col + upd

        o = o * lax.rsqrt(jnp.mean(o * o, axis=1, keepdims=True) + EPS) * gn_ref[...]
        r = r_ref[:, vs]
        o_ref[:, vs] = (o * (r * _sig(r))).astype(o_ref.dtype)

    @pl.when(t == nchunks - 1)
    def _():
        sout_ref[0] = st_ref[...]


def _gla(proj, small, w2, bg, gn, l, s0, l_s0, prev, *, row0, nseq, seqlen, chunk):
    nchunks = seqlen // chunk
    blk0 = row0 // chunk
    rows = lambda b, t: blk0 + b * nchunks + t
    qk, vw = GLA_QK, GLA_VW
    hd = (GLA_HEADS, GLA_DK, GLA_DV)
    return pl.pallas_call(
        functools.partial(_gla_body, chunk=chunk, nchunks=nchunks),
        grid=(nseq, nchunks),
        in_specs=[pl.BlockSpec((chunk, qk), lambda b, t: (rows(b, t), P_Q // qk)),
                  pl.BlockSpec((chunk, qk), lambda b, t: (rows(b, t), P_K // qk)),
                  pl.BlockSpec((chunk, vw), lambda b, t: (rows(b, t), P_V // vw)),
                  pl.BlockSpec((chunk, vw), lambda b, t: (rows(b, t), P_R // vw)),
                  pl.BlockSpec((chunk, LANES), lambda b, t: (rows(b, t), 0)),
                  pl.BlockSpec((None, GLA_GATE_RANK, qk), lambda b, t: (l, 0, 0)),
                  pl.BlockSpec((None, 1, qk), lambda b, t: (l, 0, 0)),
                  pl.BlockSpec((None, 1, GLA_DV), lambda b, t: (l, 0, 0)),
                  pl.BlockSpec((None, 1) + hd, lambda b, t: (l_s0, b, 0, 0, 0)),
                  pl.BlockSpec(memory_space=pl.ANY)],
        out_specs=[pl.BlockSpec((chunk, vw), lambda b, t: (rows(b, t), 0)),
                   pl.BlockSpec((1,) + hd, lambda b, t: (b, 0, 0, 0))],
        out_shape=[jax.ShapeDtypeStruct(prev.shape, bf16),
                   jax.ShapeDtypeStruct((nseq,) + hd, f32)],
        scratch_shapes=[pltpu.VMEM(hd, f32), pltpu.VMEM((chunk, qk), f32),
                        pltpu.VMEM((GLA_HEADS, chunk, chunk), f32)],
        input_output_aliases={9: 0},
        compiler_params=_params("arbitrary", "arbitrary"),
        name="gla")(proj, proj, proj, proj, small, w2, bg, gn, s0, prev)


def _ssd_body(z_ref, x_ref, bm_ref, cm_ref, sm_ref, wx_ref, wb_ref, wc_ref, bx_ref, bb_ref, bc_ref,
              sx_ref, sb_ref, sc_ref, dtb_ref, alog_ref, dskip_ref, nrm_ref, h0_ref, prev_ref,
              y_ref, hout_ref, cxo_ref, cbo_ref, cco_ref,
              h_ref, xpx_ref, xpb_ref, xpc_ref, xs_ref, *, chunk, nchunks):
    del prev_ref
    t = pl.program_id(1)
    hist = CONV_PAD - (SSM_CONV - 1)

    @pl.when(t == 0)
    def _():
        h_ref[...] = h0_ref[0]
        xpx_ref[hist:CONV_PAD, :] = sx_ref[0]
        xpb_ref[hist:CONV_PAD, :] = sb_ref[0]
        xpc_ref[hist:CONV_PAD, :] = sc_ref[0]

    def conv_silu(xp_ref, cur_ref, w_ref, bias_ref):
        xp_ref[CONV_PAD:CONV_PAD + chunk, :] = cur_ref[...]
        acc = bias_ref[...] + xp_ref[pl.ds(hist, chunk), :] * w_ref[0:1, :]
        for i in range(1, SSM_CONV):
            acc = acc + xp_ref[pl.ds(hist + i, chunk), :] * w_ref[i:i + 1, :]
        xp_ref[hist:CONV_PAD, :] = xp_ref[chunk + hist:chunk + CONV_PAD, :]
        return acc * _sig(acc)

    xs_ref[...] = conv_silu(xpx_ref, x_ref, wx_ref, bx_ref)
    bmat = conv_silu(xpb_ref, bm_ref, wb_ref, bb_ref).astype(bf16)
    cmat = conv_silu(xpc_ref, cm_ref, wc_ref, bc_ref).astype(bf16)

    @pl.when(t == nchunks - 1)
    def _():
        cxo_ref[0] = xpx_ref[hist:CONV_PAD, :]
        cbo_ref[0] = xpb_ref[hist:CONV_PAD, :]
        cco_ref[0] = xpc_ref[hist:CONV_PAD, :]

    dt = _softplus(sm_ref[...] + dtb_ref[...])
    cum = _cumsum_rows(dt * -jnp.exp(alog_ref[...]))
    last = cum[chunk - 1:chunk, :]
    w_end = jnp.exp(last - cum) * dt
    e_cum = jnp.exp(cum)
    e_last = jnp.exp(last)
    cum2 = cum * LOG2E
    adj2_t = ((cum - jnp.log(dt)) * LOG2E).T

    row = lax.broadcasted_iota(jnp.int32, (chunk, chunk), 0)
    col = lax.broadcasted_iota(jnp.int32, (chunk, chunk), 1)
    causal = row >= col
    head = lax.broadcasted_iota(jnp.int32, (LANES, SSM_GROUP_W), 0)
    chan = lax.broadcasted_iota(jnp.int32, (LANES, SSM_GROUP_W), 1)
    nt = (((1,), (1,)), ((), ()))
    tn = (((0,), (0,)), ((), ()))
    for g in range(SSM_GROUPS):
        cs = slice(g * SSM_GROUP_W, (g + 1) * SSM_GROUP_W)
        ns = slice(g * SSM_D_STATE, (g + 1) * SSM_D_STATE)
        xs = xs_ref[:, cs]
        xsb = xs.astype(bf16)
        b_g = bmat[:, ns]
        c_g = cmat[:, ns]
        widen = (head == g * SSM_HPG + chan // SSM_HEAD_DIM).astype(bf16)
        w_full = _dot_exact01(w_end, widen, 2)
        ecum_full = _dot_exact01(e_cum, widen, 2)

        cb = lax.dot_general(c_g, b_g, nt, preferred_element_type=f32)
        state = h_ref[g]
        y = lax.dot_general(c_g, state.astype(bf16), nt, preferred_element_type=f32) * ecum_full
        parts = []
        for j in range(SSM_HPG):
            hd = g * SSM_HPG + j
            seg = jnp.exp2(jnp.where(causal, cum2[:, hd:hd + 1] - adj2_t[hd:hd + 1, :], -jnp.inf))
            parts.append(jnp.dot((cb * seg).astype(bf16), xsb[:, j * SSM_HEAD_DIM:(j + 1) * SSM_HEAD_DIM],
                                 preferred_element_type=f32))
        y = y + jnp.concatenate(parts, axis=1)

        upd = lax.dot_general((xs * w_full).astype(bf16), b_g, tn, preferred_element_type=f32)
        for j in range(SSM_HPG):
            hd = g * SSM_HPG + j
            rs = slice(j * SSM_HEAD_DIM, (j + 1) * SSM_HEAD_DIM)
            h_ref[g, rs, :] = state[rs, :] * e_last[:, hd:hd + 1] + upd[rs, :]

        y = y + dskip_ref[:, cs] * xs
        z = z_ref[:, cs]
        y = y * (z * _sig(z))
        y = y * lax.rsqrt(jnp.mean(y * y, axis=1, keepdims=True) + EPS) * nrm_ref[:, cs]
        y_ref[:, cs] = y.astype(y_ref.dtype)

    @pl.when(t == nchunks - 1)
    def _():
        hout_ref[0] = h_ref[...]


def _ssd(proj, small, conv_w, conv_b, l, conv_s, l_cs, dtb, alog, dskip, nrm, h0, l_h0, prev,
         *, row0, nseq, seqlen, chunk):
    nchunks = seqlen // chunk
    blk0 = row0 // chunk
    di, bc = SSM_D_INNER, SSM_BC
    rows = lambda b, t: blk0 + b * nchunks + t
    taps, hist = SSM_CONV, SSM_CONV - 1
    xb, bb, cb = 0, di // bc, di // bc + 1
    hs = (SSM_GROUPS, SSM_GROUP_W, SSM_D_STATE)
    fix3 = lambda b, t: (l, 0, 0)
    in_specs = [
        pl.BlockSpec((chunk, di), lambda b, t: (rows(b, t), P_Z // di)),
        pl.BlockSpec((chunk, di), lambda b, t: (rows(b, t), P_X // di)),
        pl.BlockSpec((chunk, bc), lambda b, t: (rows(b, t), P_B // bc)),
        pl.BlockSpec((chunk, bc), lambda b, t: (rows(b, t), P_C // bc)),
        pl.BlockSpec((chunk, LANES), lambda b, t: (rows(b, t), 0)),
        pl.BlockSpec((None, taps, di), lambda b, t: (l, 0, xb)),
        pl.BlockSpec((None, taps, bc), lambda b, t: (l, 0, bb)),
        pl.BlockSpec((None, taps, bc), lambda b, t: (l, 0, cb)),
        pl.BlockSpec((None, 1, di), lambda b, t: (l, 0, xb)),
        pl.BlockSpec((None, 1, bc), lambda b, t: (l, 0, bb)),
        pl.BlockSpec((None, 1, bc), lambda b, t: (l, 0, cb)),
        pl.BlockSpec((None, 1, hist, di), lambda b, t: (l_cs, b, 0, xb)),
        pl.BlockSpec((None, 1, hist, bc), lambda b, t: (l_cs, b, 0, bb)),
        pl.BlockSpec((None, 1, hist, bc), lambda b, t: (l_cs, b, 0, cb)),
        pl.BlockSpec((None, 1, LANES), fix3),
        pl.BlockSpec((None, 1, LANES), fix3),
        pl.BlockSpec((None, 1, di), fix3),
        pl.BlockSpec((None, 1, di), fix3),
        pl.BlockSpec((None, 1) + hs, lambda b, t: (l_h0, b, 0, 0, 0)),
        pl.BlockSpec(memory_space=pl.ANY),
    ]
    return pl.pallas_call(
        functools.partial(_ssd_body, chunk=chunk, nchunks=nchunks),
        grid=(nseq, nchunks),
        in_specs=in_specs,
        out_specs=[pl.BlockSpec((chunk, di), lambda b, t: (rows(b, t), 0)),
                   pl.BlockSpec((1,) + hs, lambda b, t: (b, 0, 0, 0)),
                   pl.BlockSpec((1, hist, di), lambda b, t: (b, 0, 0)),
                   pl.BlockSpec((1, hist, bc), lambda b, t: (b, 0, 0)),
                   pl.BlockSpec((1, hist, bc), lambda b, t: (b, 0, 0))],
        out_shape=[jax.ShapeDtypeStruct(prev.shape, bf16),
                   jax.ShapeDtypeStruct((nseq,) + hs, f32),
                   jax.ShapeDtypeStruct((nseq, hist, di), f32),
                   jax.ShapeDtypeStruct((nseq, hist, bc), f32),
                   jax.ShapeDtypeStruct((nseq, hist, bc), f32)],
        scratch_shapes=[pltpu.VMEM(hs, f32), pltpu.VMEM((chunk + CONV_PAD, di), f32),
                        pltpu.VMEM((chunk + CONV_PAD, bc), f32), pltpu.VMEM((chunk + CONV_PAD, bc), f32),
                        pltpu.VMEM((chunk, di), f32)],
        input_output_aliases={19: 0},
        compiler_params=_params("arbitrary", "arbitrary"),
        name="ssd")(proj, proj, proj, proj, small, conv_w, conv_w, conv_w, conv_b, conv_b, conv_b,
                    conv_s, conv_s, conv_s, dtb, alog, dskip, nrm, h0, prev)


def _pad_lanes(v):
    return jnp.pad(v, ((0, 0), (0, LANES - v.shape[-1])))[:, None, :]


def kernel(x_prompt, x_sample, state_gla, state_ssm, state_conv, norm_mix, w_in, gla_w_gate2, gla_b_gate,
           gla_norm, gla_w_out, ssm_conv_w, ssm_conv_b, ssm_dt_bias, ssm_a_log, ssm_d, ssm_norm, ssm_w_out,
           w_o, norm_ffn, router_w1, router_b1, router_w2, router_b2, exp_w_gate, exp_w_up, exp_w_down,
           norm_final):
    n_p, t_p = x_prompt.shape[:2]
    n_s, t_s = x_sample.shape[:2]
    assert n_p == 1
    rows_p, rows_s = n_p * t_p, n_s * t_s
    m = rows_p + rows_s
    tm = 832 if m % 832 == 0 else 128
    tm_out = 416 if m % 416 == 0 else 128
    gla_chunk_p, ssd_chunk_p = min(64, t_p), min(256, t_p)

    x = jnp.concatenate([x_prompt.reshape(rows_p, D_MODEL), x_sample.reshape(rows_s, D_MODEL)], axis=0)
    zeros_gla = jnp.zeros((1, n_p, GLA_HEADS, GLA_DK, GLA_DV), f32)
    zeros_ssm = jnp.zeros((1, n_p, SSM_GROUPS, SSM_GROUP_W, SSM_D_STATE), f32)
    zeros_conv = jnp.zeros((1, n_p, SSM_CONV - 1, SSM_CONV_DIM), f32)
    state_ssm_g = state_ssm.reshape(DEPTH, n_s, SSM_GROUPS, SSM_GROUP_W, SSM_D_STATE)

    bg3 = gla_b_gate[:, None, :]
    gn3 = gla_norm[:, None, :]
    conv_b3 = ssm_conv_b[:, None, :]
    dtb3 = _pad_lanes(ssm_dt_bias)
    alog3 = _pad_lanes(ssm_a_log)
    dskip3 = jnp.repeat(ssm_d, SSM_HEAD_DIM, axis=1)[:, None, :]
    nrm3 = ssm_norm[:, None, :]

    outs = {k: [] for k in ("gla_p", "ssm_p", "conv_p", "gla_s", "ssm_s", "conv_s")}
    for l in range(DEPTH):
        wl = w_in[l]
        w_small = jnp.concatenate(
            [wl[:, W_DT:W_MG], wl[:, W_GLR:W_Z],
             jnp.zeros((D_MODEL, LANES - SSM_HEADS - GLA_GATE_RANK), f32)], axis=1).astype(bf16)

        u = _rmsnorm(x, norm_mix[l], rows=m, row0=0, tm=tm, out_dtype=bf16)
        proj = _inproj(u, w_in, l, tm=tm)
        small = _matmul(u, w_small, tm=tm, tn=LANES, out_dtype=f32, name="inproj_small")

        gla_args = (proj, small, gla_w_gate2, bg3, gn3, l)
        oa = jnp.zeros((m, GLA_VW), bf16)
        oa, g_p = _gla(*gla_args, zeros_gla, 0, oa, row0=0, nseq=n_p, seqlen=t_p, chunk=gla_chunk_p)
        oa, g_s = _gla(*gla_args, state_gla, l, oa, row0=rows_p, nseq=n_s, seqlen=t_s, chunk=t_s)

        ssd_args = (proj, small, ssm_conv_w, conv_b3, l)
        ssd_tail = (dtb3, alog3, dskip3, nrm3)
        ob = jnp.zeros((m, SSM_D_INNER), bf16)
        ob, h_p, *conv_p = _ssd(*ssd_args, zeros_conv, 0, *ssd_tail, zeros_ssm, 0, ob,
                                row0=0, nseq=n_p, seqlen=t_p, chunk=ssd_chunk_p)
        ob, h_s, *conv_s = _ssd(*ssd_args, state_conv, l, *ssd_tail, state_ssm_g, l, ob,
                                row0=rows_p, nseq=n_s, seqlen=t_s, chunk=t_s)

        merged = _merge(oa, ob, proj, gla_w_out, ssm_w_out, l, tm=tm_out, tn=512)

        wr = jnp.concatenate([router_w1[l], router_w2[l],
                              jnp.zeros((D_MODEL, LANES - MOE_GROUPS - MOE_EXPERTS), f32)], axis=1)
        br = jnp.pad(jnp.concatenate([router_b1[l], router_b2[l]]),
                     (0, LANES - MOE_GROUPS - MOE_EXPERTS)).reshape(1, LANES)
        xn, u2, cw = _outproj(merged, w_o, l, x, norm_ffn[l].reshape(1, -1), wr, br, tm=tm_out)
        x = _moe(u2, cw, xn, exp_w_gate[l].astype(bf16), exp_w_up[l].astype(bf16),
                 exp_w_down[l].astype(bf16), tm=tm)

        outs["gla_p"].append(g_p)
        outs["gla_s"].append(g_s)
        outs["ssm_p"].append(h_p.reshape(n_p, SSM_HEADS, SSM_HEAD_DIM, SSM_D_STATE))
        outs["ssm_s"].append(h_s.reshape(n_s, SSM_HEADS, SSM_HEAD_DIM, SSM_D_STATE))
        outs["conv_p"].append(jnp.concatenate(conv_p, axis=-1))
        outs["conv_s"].append(jnp.concatenate(conv_s, axis=-1))

    tm_p = 512 if rows_p % 512 == 0 else 128
    tm_s = rows_s if rows_p % rows_s == 0 else 8
    y_p = _rmsnorm(x, norm_final, rows=rows_p, row0=0, tm=tm_p, out_dtype=f32)
    y_s = _rmsnorm(x, norm_final, rows=rows_s, row0=rows_p, tm=tm_s, out_dtype=f32)
    return (y_p.reshape(n_p, t_p, D_MODEL), y_s.reshape(n_s, t_s, D_MODEL),
            jnp.stack(outs["gla_p"]), jnp.stack(outs["ssm_p"]), jnp.stack(outs["conv_p"]),
            jnp.stack(outs["gla_s"]), jnp.stack(outs["ssm_s"]), jnp.stack(outs["conv_s"]))
```

```python
import functools
import math

import jax
import jax.numpy as jnp
from jax import lax
from jax.experimental import pallas as pl
from jax.experimental.pallas import tpu as pltpu

f32 = jnp.float32
bf16 = jnp.bfloat16
HIGHEST = lax.Precision.HIGHEST
LOG2E = math.log2(math.e)

D_MODEL = 2048
DEPTH = 2
EPS = 1e-6
GLA_HEADS = 4
GLA_DK = 256
GLA_DV = 512
GLA_QK = GLA_HEADS * GLA_DK
GLA_VW = GLA_HEADS * GLA_DV
GLA_GATE_RANK = 16
GLA_GATE_TAU = 16.0
GLA_SUB = 16
SSM_D_INNER = 4096
SSM_HEAD_DIM = 64
SSM_HEADS = 64
SSM_GROUPS = 8
SSM_HPG = 8
SSM_GROUP_W = SSM_HPG * SSM_HEAD_DIM
SSM_D_STATE = 128
SSM_BC = SSM_GROUPS * SSM_D_STATE
SSM_CONV = 4
SSM_CONV_DIM = 6144
MOE_GROUPS = 4
MOE_PER_GROUP = 4
MOE_EXPERTS = 16
MOE_D_FF = 512
MOE_TOP_K = 2
MOE_LANE0 = MOE_GROUPS
MOE_TILE = 256
LANES = 128
CONV_PAD = 8

W_GLR = 6144
W_Z = 6160
W_DT = 16400
W_MG = 16464
IN_TN = 1024
IN_TILES = 20
IN_ALIGNED_TILES = W_GLR // IN_TN
IN_MG_TILE0 = (W_DT - W_Z + W_GLR) // IN_TN
IN_SHIFT_Z = W_Z - W_GLR
IN_SHIFT_MG = W_MG - IN_MG_TILE0 * IN_TN
P_Z, P_X, P_Q, P_K, P_V, P_R, P_B, P_C, P_GA, P_GB, P_END = (
    0, 4096, 8192, 9216, 10240, 12288, 14336, 15360, 16384, 18432, 20480)
S_GLR = 64

VMEM_LIMIT = 56 * 1024 * 1024


def _sig(x):
    return 1.0 / (1.0 + jnp.exp(-x))


def _softplus(x):
    return jnp.maximum(x, 0.0) + jnp.log1p(jnp.exp(-jnp.abs(x)))


def _log_sigmoid(x):
    return jnp.minimum(x, 0.0) - jnp.log1p(jnp.exp(-jnp.abs(x)))


def _params(*sem):
    return pltpu.CompilerParams(dimension_semantics=sem, vmem_limit_bytes=VMEM_LIMIT)


def _split_bf16(x, pieces):
    out = []
    for _ in range(pieces - 1):
        p = x.astype(bf16)
        out.append(p)
        x = x - p.astype(f32)
    out.append(x.astype(bf16))
    return out


def _dot_exact01(a, b01, pieces, dims=None):
    acc = None
    for p in _split_bf16(a, pieces):
        if dims is None:
            d = jnp.dot(p, b01, preferred_element_type=f32)
        else:
            d = lax.dot_general(p, b01, dims, preferred_element_type=f32)
        acc = d if acc is None else acc + d
    return acc


def _cumsum_rows(x):
    n = x.shape[0]
    row = lax.broadcasted_iota(jnp.int32, (n, n), 0)
    col = lax.broadcasted_iota(jnp.int32, (n, n), 1)
    tri = (row >= col).astype(bf16)
    acc = None
    for p in _split_bf16(x, 3):
        d = jnp.dot(tri, p, preferred_element_type=f32)
        acc = d if acc is None else acc + d
    return acc


def _norm_body(x_ref, g_ref, o_ref):
    x = x_ref[...]
    y = x * lax.rsqrt(jnp.mean(x * x, axis=-1, keepdims=True) + EPS)
    o_ref[...] = (y * g_ref[...]).astype(o_ref.dtype)


def _rmsnorm(x, g, *, rows, row0, tm, out_dtype):
    blk0 = row0 // tm
    return pl.pallas_call(
        _norm_body, grid=(rows // tm,),
        in_specs=[pl.BlockSpec((tm, D_MODEL), lambda m: (blk0 + m, 0)),
                  pl.BlockSpec((1, D_MODEL), lambda m: (0, 0))],
        out_specs=pl.BlockSpec((tm, D_MODEL), lambda m: (m, 0)),
        out_shape=jax.ShapeDtypeStruct((rows, D_MODEL), out_dtype),
        compiler_params=_params("arbitrary"), name="rmsnorm")(x, g.reshape(1, D_MODEL))


def _inproj_body(u_ref, wa_ref, wb_ref, o_ref, wbf_ref):
    n = pl.program_id(0)

    @pl.when(pl.program_id(1) == 0)
    def _():
        def fill(shift):
            def f():
                if shift == 0:
                    wbf_ref[...] = wa_ref[...].astype(bf16)
                else:
                    wbf_ref[:IN_TN - shift, :] = wa_ref[shift:, :].astype(bf16)
                    wbf_ref[IN_TN - shift:, :] = wb_ref[:shift, :].astype(bf16)
            return f
        pl.when(n < IN_ALIGNED_TILES)(fill(0))
        pl.when((n >= IN_ALIGNED_TILES) & (n < IN_MG_TILE0))(fill(IN_SHIFT_Z))
        pl.when(n >= IN_MG_TILE0)(fill(IN_SHIFT_MG))

    o_ref[...] = lax.dot_general(u_ref[...], wbf_ref[...], (((1,), (1,)), ((), ())),
                                 preferred_element_type=f32)


def _inproj_out_tile(n):
    lead = IN_ALIGNED_TILES
    zx = (P_Q - P_Z) // IN_TN
    return jnp.where(n < lead, n + zx, jnp.where(n < lead + zx, n - lead, n))


def _inproj(u, w_in_t, l, *, tm):
    m = u.shape[0]
    return pl.pallas_call(
        _inproj_body, grid=(IN_TILES, m // tm),
        in_specs=[pl.BlockSpec((tm, D_MODEL), lambda n, i: (i, 0)),
                  pl.BlockSpec((None, IN_TN, D_MODEL), lambda n, i: (l, n, 0)),
                  pl.BlockSpec((None, LANES, D_MODEL), lambda n, i: (l, (n + 1) * (IN_TN // LANES), 0))],
        out_specs=pl.BlockSpec((tm, IN_TN), lambda n, i: (i, _inproj_out_tile(n))),
        out_shape=jax.ShapeDtypeStruct((m, P_END), f32),
        scratch_shapes=[pltpu.VMEM((IN_TN, D_MODEL), bf16)],
        compiler_params=_params("arbitrary", "arbitrary"), name="inproj")(u, w_in_t, w_in_t)


def _mm_nt_body(a_ref, w_ref, o_ref):
    o_ref[...] = lax.dot_general(a_ref[...], w_ref[...].astype(bf16), (((1,), (1,)), ((), ())),
                                 preferred_element_type=f32)


def _matmul_nt(a, w_t, *, tm, name):
    m, k = a.shape
    n = w_t.shape[0]
    return pl.pallas_call(
        _mm_nt_body, grid=(m // tm,),
        in_specs=[pl.BlockSpec((tm, k), lambda i: (i, 0)),
                  pl.BlockSpec((n, k), lambda i: (0, 0))],
        out_specs=pl.BlockSpec((tm, n), lambda i: (i, 0)),
        out_shape=jax.ShapeDtypeStruct((m, n), f32),
        compiler_params=_params("arbitrary"), name=name)(a, w_t)


def _merge_body(oa_ref, ob_ref, ga_ref, gb_ref, wa_ref, wb_ref, o_ref, wabf_ref, wbbf_ref):
    @pl.when(pl.program_id(1) == 0)
    def _():
        wabf_ref[...] = wa_ref[...].astype(bf16)
        wbbf_ref[...] = wb_ref[...].astype(bf16)

    ya = jnp.dot(oa_ref[...], wabf_ref[...], preferred_element_type=f32)
    yb = jnp.dot(ob_ref[...], wbbf_ref[...], preferred_element_type=f32)
    o_ref[...] = (_sig(ga_ref[...]) * ya + _sig(gb_ref[...]) * yb).astype(o_ref.dtype)


def _merge(oa, ob, proj, wa, wb, l, *, tm, tn):
    m = oa.shape[0]
    ga0, gb0 = P_GA // tn, P_GB // tn
    return pl.pallas_call(
        _merge_body, grid=(D_MODEL // tn, m // tm),
        in_specs=[pl.BlockSpec((tm, GLA_VW), lambda j, i: (i, 0)),
                  pl.BlockSpec((tm, SSM_D_INNER), lambda j, i: (i, 0)),
                  pl.BlockSpec((tm, tn), lambda j, i: (i, ga0 + j)),
                  pl.BlockSpec((tm, tn), lambda j, i: (i, gb0 + j)),
                  pl.BlockSpec((None, GLA_VW, tn), lambda j, i: (l, 0, j)),
                  pl.BlockSpec((None, SSM_D_INNER, tn), lambda j, i: (l, 0, j))],
        out_specs=pl.BlockSpec((tm, tn), lambda j, i: (i, j)),
        out_shape=jax.ShapeDtypeStruct((m, D_MODEL), bf16),
        scratch_shapes=[pltpu.VMEM((GLA_VW, tn), bf16), pltpu.VMEM((SSM_D_INNER, tn), bf16)],
        compiler_params=_params("arbitrary", "arbitrary"), name="merge")(oa, ob, proj, proj, wa, wb)


def _route(logits):
    c = lax.broadcasted_iota(jnp.int32, logits.shape, 1)
    big = jnp.int32(1 << 20)
    is_group = c < MOE_GROUPS
    lg = jnp.where(is_group, logits, -jnp.inf)
    top = jnp.max(lg, axis=1, keepdims=True)
    gsel = jnp.min(jnp.where(is_group & (logits == top), c, big), axis=1, keepdims=True)
    p_sel = 1.0 / jnp.sum(jnp.exp(lg - top), axis=1, keepdims=True)
    in_group = ((c >= MOE_LANE0) & (c < MOE_LANE0 + MOE_EXPERTS)
                & (((c - MOE_LANE0) // MOE_PER_GROUP) == gsel))
    le = jnp.where(in_group, logits, -jnp.inf)
    tv0 = jnp.max(le, axis=1, keepdims=True)
    i0 = jnp.min(jnp.where(in_group & (logits == tv0), c, big), axis=1, keepdims=True)
    rest = in_group & (c != i0)
    le1 = jnp.where(rest, logits, -jnp.inf)
    tv1 = jnp.max(le1, axis=1, keepdims=True)
    i1 = jnp.min(jnp.where(rest & (logits == tv1), c, big), axis=1, keepdims=True)
    t = jnp.exp(tv1 - tv0)
    w0 = p_sel / (1.0 + t)
    w1 = p_sel * t / (1.0 + t)
    cw = jnp.where(c == i0, w0, jnp.where(c == i1, w1, 0.0))
    return cw, ((c == i0) | (c == i1)).astype(f32)


def _outproj_body(m_ref, wo_ref, x_ref, g_ref, wr_ref, br_ref, xn_ref, u_ref, cw_ref, oh_ref, wobf_ref):
    @pl.when(pl.program_id(0) == 0)
    def _():
        wobf_ref[...] = wo_ref[...].astype(bf16)

    xn = x_ref[...] + jnp.dot(m_ref[...], wobf_ref[...], preferred_element_type=f32)
    xn_ref[...] = xn
    u = xn * lax.rsqrt(jnp.mean(xn * xn, axis=-1, keepdims=True) + EPS) * g_ref[...]
    u_ref[...] = u
    logits = jnp.dot(u, wr_ref[...], precision=HIGHEST, preferred_element_type=f32) + br_ref[...]
    cw, oh = _route(logits)
    cw_ref[...] = cw
    oh_ref[...] = oh


def _outproj(merged, wo, l, x, g, wr, br, *, tm):
    m = x.shape[0]
    row = lambda i: (i, 0)
    fix = lambda i: (0, 0)
    return pl.pallas_call(
        _outproj_body, grid=(m // tm,),
        in_specs=[pl.BlockSpec((tm, D_MODEL), row),
                  pl.BlockSpec((None, D_MODEL, D_MODEL), lambda i: (l, 0, 0), pipeline_mode=pl.Buffered(1)),
                  pl.BlockSpec((tm, D_MODEL), row), pl.BlockSpec((1, D_MODEL), fix),
                  pl.BlockSpec((D_MODEL, LANES), fix), pl.BlockSpec((1, LANES), fix)],
        out_specs=[pl.BlockSpec((tm, D_MODEL), row), pl.BlockSpec((tm, D_MODEL), row),
                   pl.BlockSpec((tm, LANES), row), pl.BlockSpec((tm, LANES), row)],
        out_shape=[jax.ShapeDtypeStruct((m, D_MODEL), f32), jax.ShapeDtypeStruct((m, D_MODEL), f32),
                   jax.ShapeDtypeStruct((m, LANES), f32), jax.ShapeDtypeStruct((m, LANES), f32)],
        scratch_shapes=[pltpu.VMEM((D_MODEL, D_MODEL), bf16)],
        compiler_params=_params("arbitrary"), name="outproj")(merged, wo, x, g, wr, br)


def _count_body(oh_ref, cnt_ref):
    @pl.when(pl.program_id(0) == 0)
    def _():
        cnt_ref[...] = jnp.zeros_like(cnt_ref)

    cnt_ref[...] += jnp.sum(oh_ref[...], axis=0, keepdims=True)


def _plan_body(oh_ref, cw_ref, cnt_ref, dest_ref, wts_ref, te_ref, pad_ref, base_ref, carry_ref, *, rows):
    lane = lax.broadcasted_iota(jnp.int32, (8, LANES), 1)
    is_expert = (lane >= MOE_LANE0) & (lane < MOE_LANE0 + MOE_EXPERTS)

    @pl.when(pl.program_id(0) == 0)
    def _():
        cnt = jnp.broadcast_to(cnt_ref[...], (8, LANES))
        padded = jnp.floor((cnt + (MOE_TILE - 1)) * (1.0 / MOE_TILE)) * MOE_TILE
        src = lax.broadcasted_iota(jnp.int32, (LANES, LANES), 0)
        dst = lax.broadcasted_iota(jnp.int32, (LANES, LANES), 1)
        base = _dot_exact01(padded, (src < dst).astype(bf16), 3)
        base_ref[...] = base[0:1]
        carry_ref[...] = jnp.zeros_like(carry_ref)
        end = base + padded
        total = jnp.max(jnp.where(is_expert, end, 0.0), axis=1, keepdims=True)
        end_col = jnp.transpose(jnp.where(is_expert, end, jnp.inf))[:, 0:1]
        tile_start = (lax.broadcasted_iota(jnp.int32, (LANES, LANES), 1) * MOE_TILE).astype(f32)
        tile_exp = jnp.sum((end_col <= tile_start).astype(f32), axis=0, keepdims=True)
        tile_exp = jnp.minimum(tile_exp, MOE_EXPERTS - 1.0)
        te_ref[0:1, :] = tile_exp.astype(jnp.int32)
        te_ref[1:2, :] = jnp.broadcast_to(total[0:1] * (1.0 / MOE_TILE), (1, LANES)).astype(jnp.int32)
        shift = (src == dst + MOE_LANE0).astype(bf16)
        is_tail = lane[0:1] == MOE_EXPERTS
        pad_lo = jnp.where(is_tail, total[0:1], _dot_exact01(base + cnt, shift, 3)[0:1])
        pad_hi = jnp.where(is_tail, float(rows), _dot_exact01(end, shift, 3)[0:1])
        pad_ref[0:1, :] = pad_lo.astype(jnp.int32)
        pad_ref[1:2, :] = pad_hi.astype(jnp.int32)

    oh = oh_ref[...]
    n = oh.shape[0]
    row = lax.broadcasted_iota(jnp.int32, (n, n), 0)
    col = lax.broadcasted_iota(jnp.int32, (n, n), 1)
    rank = jnp.dot((row > col).astype(bf16), oh.astype(bf16), preferred_element_type=f32)
    pos = base_ref[...] + carry_ref[...] + rank
    carry_ref[...] += jnp.sum(oh, axis=0, keepdims=True)
    chosen = oh > 0.0
    d_lo = jnp.min(jnp.where(chosen, pos, jnp.inf), axis=1, keepdims=True)
    d_hi = jnp.max(jnp.where(chosen, pos, -1.0), axis=1, keepdims=True)
    cw = cw_ref[...]
    w_lo = jnp.sum(jnp.where(chosen & (pos == d_lo), cw, 0.0), axis=1, keepdims=True)
    w_hi = jnp.sum(jnp.where(chosen & (pos == d_hi), cw, 0.0), axis=1, keepdims=True)
    lanes = lax.broadcasted_iota(jnp.int32, oh.shape, 1)
    wts_ref[...] = jnp.where(lanes == 0, w_lo, jnp.where(lanes == 1, w_hi, 0.0))
    dests = jnp.where(lanes == 0, d_lo, jnp.where(lanes == 1, d_hi, 0.0))
    dest_ref[0] = jnp.transpose(dests)[0:2, :].astype(jnp.int32)


def _moe_plan(oh, cw, *, blk, rows):
    m = oh.shape[0]
    nblk = m // blk
    cnt = pl.pallas_call(
        _count_body, grid=(nblk,),
        in_specs=[pl.BlockSpec((blk, LANES), lambda i: (i, 0))],
        out_specs=pl.BlockSpec((1, LANES), lambda i: (0, 0)),
        out_shape=jax.ShapeDtypeStruct((1, LANES), f32),
        compiler_params=_params("arbitrary"), name="moe_count")(oh)
    fix = lambda i: (0, 0)
    return pl.pallas_call(
        functools.partial(_plan_body, rows=rows), grid=(nblk,),
        in_specs=[pl.BlockSpec((blk, LANES), lambda i: (i, 0)), pl.BlockSpec((blk, LANES), lambda i: (i, 0)),
                  pl.BlockSpec((1, LANES), fix)],
        out_specs=[pl.BlockSpec((1, 2, blk), lambda i: (i, 0, 0)), pl.BlockSpec((blk, LANES), lambda i: (i, 0)),
                   pl.BlockSpec((2, LANES), fix), pl.BlockSpec((2, LANES), fix)],
        out_shape=[jax.ShapeDtypeStruct((nblk, 2, blk), jnp.int32), jax.ShapeDtypeStruct((m, LANES), f32),
                   jax.ShapeDtypeStruct((2, LANES), jnp.int32), jax.ShapeDtypeStruct((2, LANES), jnp.int32)],
        scratch_shapes=[pltpu.VMEM((1, LANES), f32), pltpu.VMEM((1, LANES), f32)],
        compiler_params=_params("arbitrary"), name="moe_plan")(oh, cw, cnt)


def _row_copy(src_ref, s, dst_ref, d, sem):
    return pltpu.make_async_copy(src_ref.at[pl.ds(s, 1), :], dst_ref.at[pl.ds(d, 1), :], sem)


def _dispatch_body(dest_ref, pad_ref, u_ref, xs_ref, zero_ref, sem, *, blk, nblk):
    i = pl.program_id(0)

    def wait_rows(count, which):
        pltpu.make_async_copy(xs_ref.at[pl.ds(0, count), :], xs_ref.at[pl.ds(0, count), :], sem.at[which]).wait()

    @pl.when(i == 0)
    def _():
        zero_ref[...] = jnp.zeros_like(zero_ref)
        group = zero_ref.shape[0]

        def fill_row(r, c):
            _row_copy(zero_ref, 0, xs_ref, r, sem.at[1]).start()
            return c

        def fill_group(lo8):
            def f(r, c):
                row0 = pl.multiple_of(lo8 + r * group, group)
                pltpu.make_async_copy(zero_ref, xs_ref.at[pl.ds(row0, group), :], sem.at[1]).start()
                return c
            return f

        nrow = jnp.int32(0)
        ngroup = jnp.int32(0)
        for e in range(MOE_EXPERTS + 1):
            lo = pad_ref[0, e]
            hi = pad_ref[1, e]
            lo8 = jnp.minimum((lo + group - 1) // group * group, hi)
            lax.fori_loop(lo, lo8, fill_row, 0)
            lax.fori_loop(0, (hi - lo8) // group, fill_group(lo8), 0)
            nrow = nrow + (lo8 - lo)
            ngroup = ngroup + (hi - lo8) // group

        def wait_row(r, c):
            _row_copy(zero_ref, 0, xs_ref, 0, sem.at[1]).wait()
            return c
        lax.fori_loop(0, nrow, wait_row, 0)

        @pl.when(ngroup > 0)
        def _():
            wait_rows(pl.multiple_of(ngroup * group, group), 1)

    def send(j, c):
        t = i * blk + j
        _row_copy(u_ref, t, xs_ref, dest_ref[0, 0, j], sem.at[0]).start()
        _row_copy(u_ref, t, xs_ref, dest_ref[0, 1, j], sem.at[0]).start()
        return c
    lax.fori_loop(0, blk, send, 0)

    @pl.when(i > 0)
    def _():
        wait_rows(2 * blk, 0)

    @pl.when(i == nblk - 1)
    def _():
        wait_rows(2 * blk, 0)


def _moe_dispatch(u, dest, pad, *, rows, blk):
    m = u.shape[0]
    nblk = m // blk
    return pl.pallas_call(
        functools.partial(_dispatch_body, blk=blk, nblk=nblk), grid=(nblk,),
        in_specs=[pl.BlockSpec((1, 2, blk), lambda i: (i, 0, 0), memory_space=pltpu.SMEM),
                  pl.BlockSpec(memory_space=pltpu.SMEM),
                  pl.BlockSpec(memory_space=pl.ANY)],
        out_specs=pl.BlockSpec(memory_space=pl.ANY),
        out_shape=jax.ShapeDtypeStruct((rows, D_MODEL), f32),
        scratch_shapes=[pltpu.VMEM((8, D_MODEL), f32), pltpu.SemaphoreType.DMA((2,))],
        compiler_params=_params("arbitrary"), name="moe_dispatch")(dest, pad, u)


def _gmm_body(te_ref, xs_ref, wg_ref, wu_ref, wd_ref, ys_ref, wgb_ref, wub_ref, wdb_ref):
    i = pl.program_id(0)
    n_used = te_ref[1, 0]

    @pl.when(i < n_used)
    def _():
        @pl.when((i == 0) | (te_ref[0, i] != te_ref[0, jnp.maximum(i - 1, 0)]))
        def _():
            wgb_ref[...] = wg_ref[...].astype(bf16)
            wub_ref[...] = wu_ref[...].astype(bf16)
            wdb_ref[...] = wd_ref[...].astype(bf16)

        x = xs_ref[...].astype(bf16)
        hg = jnp.dot(x, wgb_ref[...], preferred_element_type=f32)
        hu = jnp.dot(x, wub_ref[...], preferred_element_type=f32)
        h = ((hg * _sig(hg)) * hu).astype(bf16)
        ys_ref[...] = jnp.dot(h, wdb_ref[...], preferred_element_type=f32)

    @pl.when(i >= n_used)
    def _():
        ys_ref[...] = jnp.zeros_like(ys_ref)


def _moe_gmm(te, xs, wg, wu, wd, l):
    rows = xs.shape[0]
    ntiles = rows // MOE_TILE
    tile = lambda i, te: (jnp.minimum(i, te[1, 0] - 1), 0)
    up = lambda i, te: (l, te[0, i], 0, 0)
    return pl.pallas_call(
        _gmm_body,
        grid_spec=pltpu.PrefetchScalarGridSpec(
            num_scalar_prefetch=1, grid=(ntiles,),
            in_specs=[pl.BlockSpec((MOE_TILE, D_MODEL), tile),
                      pl.BlockSpec((None, None, D_MODEL, MOE_D_FF), up),
                      pl.BlockSpec((None, None, D_MODEL, MOE_D_FF), up),
                      pl.BlockSpec((None, None, MOE_D_FF, D_MODEL), up)],
            out_specs=pl.BlockSpec((MOE_TILE, D_MODEL), lambda i, te: (i, 0)),
            scratch_shapes=[pltpu.VMEM((D_MODEL, MOE_D_FF), bf16), pltpu.VMEM((D_MODEL, MOE_D_FF), bf16),
                            pltpu.VMEM((MOE_D_FF, D_MODEL), bf16)]),
        out_shape=jax.ShapeDtypeStruct((rows, D_MODEL), f32),
        compiler_params=_params("arbitrary"), name="moe_gmm")(te, xs, wg, wu, wd)


def _combine_body(dcur_ref, dnext_ref, xn_ref, wts_ref, ys_ref, o_ref, ya_ref, yb_ref, sem, *, blk, nblk):
    i = pl.program_id(0)
    slot = i % 2

    def gather(dref, s):
        def one(j, c):
            _row_copy(ys_ref, dref[0, 0, j], ya_ref.at[s], j, sem.at[s]).start()
            _row_copy(ys_ref, dref[0, 1, j], yb_ref.at[s], j, sem.at[s]).start()
            return c
        lax.fori_loop(0, blk, one, 0)

    @pl.when(i == 0)
    def _():
        gather(dcur_ref, 0)

    @pl.when(i + 1 < nblk)
    def _():
        gather(dnext_ref, 1 - slot)

    pltpu.make_async_copy(ys_ref.at[pl.ds(0, blk), :], ya_ref.at[slot], sem.at[slot]).wait()
    pltpu.make_async_copy(ys_ref.at[pl.ds(0, blk), :], yb_ref.at[slot], sem.at[slot]).wait()
    w = wts_ref[...]
    o_ref[...] = xn_ref[...] + w[:, 0:1] * ya_ref[slot] + w[:, 1:2] * yb_ref[slot]


def _moe_combine(dest, xn, wts, ys, *, blk):
    m = xn.shape[0]
    nblk = m // blk
    return pl.pallas_call(
        functools.partial(_combine_body, blk=blk, nblk=nblk), grid=(nblk,),
        in_specs=[pl.BlockSpec((1, 2, blk), lambda i: (i, 0, 0), memory_space=pltpu.SMEM),
                  pl.BlockSpec((1, 2, blk), lambda i: (jnp.minimum(i + 1, nblk - 1), 0, 0),
                               memory_space=pltpu.SMEM),
                  pl.BlockSpec((blk, D_MODEL), lambda i: (i, 0)),
                  pl.BlockSpec((blk, LANES), lambda i: (i, 0)),
                  pl.BlockSpec(memory_space=pl.ANY)],
        out_specs=pl.BlockSpec((blk, D_MODEL), lambda i: (i, 0)),
        out_shape=jax.ShapeDtypeStruct((m, D_MODEL), f32),
        scratch_shapes=[pltpu.VMEM((2, blk, D_MODEL), f32), pltpu.VMEM((2, blk, D_MODEL), f32),
                        pltpu.SemaphoreType.DMA((2,))],
        compiler_params=_params("arbitrary"), name="moe_combine")(dest, dest, xn, wts, ys)


def _moe(u, cw, oh, xn, wg, wu, wd, l):
    m = u.shape[0]
    blk = LANES
    rows = -(-(MOE_TOP_K * m + MOE_EXPERTS * (MOE_TILE - 1)) // MOE_TILE) * MOE_TILE
    assert rows // MOE_TILE <= LANES
    dest, wts, te, pad = _moe_plan(oh, cw, blk=blk, rows=rows)
    xs = _moe_dispatch(u, dest, pad, rows=rows, blk=blk)
    ys = _moe_gmm(te, xs, wg, wu, wd, l)
    return _moe_combine(dest, xn, wts, ys, blk=blk)


def _gla_body(q_ref, k_ref, v_ref, r_ref, sm_ref, w2_ref, bg_ref, gn_ref, s0_ref, prev_ref,
              o_ref, sout_ref, st_ref, b_ref, att_ref, *, chunk, nchunks):
    del prev_ref
    t = pl.program_id(1)

    @pl.when(t == 0)
    def _():
        st_ref[...] = s0_ref[0]

    glr = sm_ref[:, S_GLR:S_GLR + GLA_GATE_RANK].astype(bf16)
    pre = jnp.dot(glr, w2_ref[...].astype(bf16), preferred_element_type=f32) + bg_ref[...]
    b_ref[...] = _cumsum_rows(_log_sigmoid(pre) * (1.0 / GLA_GATE_TAU))

    sub_row = lax.broadcasted_iota(jnp.int32, (GLA_SUB, GLA_DK), 0)
    sub_col = lax.broadcasted_iota(jnp.int32, (GLA_SUB, GLA_SUB), 1)
    nt = (((1,), (1,)), ((), ()))
    tn = (((0,), (0,)), ((), ()))
    for h in range(GLA_HEADS):
        ks = slice(h * GLA_DK, (h + 1) * GLA_DK)
        vs = slice(h * GLA_DV, (h + 1) * GLA_DV)
        q = q_ref[:, ks] * (GLA_DK ** -0.5)
        k = k_ref[:, ks]
        vb = v_ref[:, vs].astype(bf16)
        b = b_ref[:, ks]
        b_last = b_ref[chunk - 1:chunk, ks]
        state = st_ref[h]
        o = jnp.dot((q * jnp.exp(b)).astype(bf16), state.astype(bf16), preferred_element_type=f32)

        att_ref[h] = jnp.zeros((chunk, chunk), f32)
        for blk in range(chunk // GLA_SUB):
            r0 = blk * GLA_SUB
            q_i = q[r0:r0 + GLA_SUB]
            b_i = b[r0:r0 + GLA_SUB]
            if blk > 0:
                c_i = b_ref[r0 - 1:r0, ks]
                q_dec = (q_i * jnp.exp(b_i - c_i)).astype(bf16)
                k_dec = (k[:r0] * jnp.exp(c_i - b[:r0])).astype(bf16)
                att_ref[h, r0:r0 + GLA_SUB, 0:r0] = lax.dot_general(q_dec, k_dec, nt,
                                                                   preferred_element_type=f32)
            diag = jnp.zeros((GLA_SUB, GLA_SUB), f32)
            for j in range(GLA_SUB):
                b_j = b_ref[r0 + j:r0 + j + 1, ks]
                k_j = k_ref[r0 + j:r0 + j + 1, ks]
                decay = jnp.exp(jnp.where(sub_row >= j, b_i - b_j, -jnp.inf))
                a_j = jnp.sum(q_i * decay * k_j, axis=1, keepdims=True)
                diag = jnp.where(sub_col == j, a_j, diag)
            att_ref[h, r0:r0 + GLA_SUB, r0:r0 + GLA_SUB] = diag

        o = o + jnp.dot(att_ref[h].astype(bf16), vb, preferred_element_type=f32)

        k_end = (k * jnp.exp(b_last - b)).astype(bf16)
        upd = lax.dot_general(k_end, vb, tn, preferred_element_type=f32)
        dec_col = jnp.transpose(jnp.broadcast_to(jnp.exp(b_last), (8, GLA_DK)))[:, 0:1]
        st_ref[h] = state * dec_col + upd

        o = o * lax.rsqrt(jnp.mean(o * o, axis=1, keepdims=True) + EPS) * gn_ref[...]
        r = r_ref[:, vs]
        o_ref[:, vs] = (o * (r * _sig(r))).astype(o_ref.dtype)

    @pl.when(t == nchunks - 1)
    def _():
        sout_ref[0] = st_ref[...]


def _gla(proj, small, w2, bg, gn, l, s0, l_s0, prev, *, row0, nseq, seqlen, chunk):
    nchunks = seqlen // chunk
    blk0 = row0 // chunk
    rows = lambda b, t: blk0 + b * nchunks + t
    qk, vw = GLA_QK, GLA_VW
    hd = (GLA_HEADS, GLA_DK, GLA_DV)
    return pl.pallas_call(
        functools.partial(_gla_body, chunk=chunk, nchunks=nchunks),
        grid=(nseq, nchunks),
        in_specs=[pl.BlockSpec((chunk, qk), lambda b, t: (rows(b, t), P_Q // qk)),
                  pl.BlockSpec((chunk, qk), lambda b, t: (rows(b, t), P_K // qk)),
                  pl.BlockSpec((chunk, vw), lambda b, t: (rows(b, t), P_V // vw)),
                  pl.BlockSpec((chunk, vw), lambda b, t: (rows(b, t), P_R // vw)),
                  pl.BlockSpec((chunk, LANES), lambda b, t: (rows(b, t), 0)),
                  pl.BlockSpec((None, GLA_GATE_RANK, qk), lambda b, t: (l, 0, 0)),
                  pl.BlockSpec((None, 1, qk), lambda b, t: (l, 0, 0)),
                  pl.BlockSpec((None, 1, GLA_DV), lambda b, t: (l, 0, 0)),
                  pl.BlockSpec((None, 1) + hd, lambda b, t: (l_s0, b, 0, 0, 0)),
                  pl.BlockSpec(memory_space=pl.ANY)],
        out_specs=[pl.BlockSpec((chunk, vw), lambda b, t: (rows(b, t), 0)),
                   pl.BlockSpec((1,) + hd, lambda b, t: (b, 0, 0, 0))],
        out_shape=[jax.ShapeDtypeStruct(prev.shape, bf16),
                   jax.ShapeDtypeStruct((nseq,) + hd, f32)],
        scratch_shapes=[pltpu.VMEM(hd, f32), pltpu.VMEM((chunk, qk), f32),
                        pltpu.VMEM((GLA_HEADS, chunk, chunk), f32)],
        input_output_aliases={9: 0},
        compiler_params=_params("arbitrary", "arbitrary"),
        name="gla")(proj, proj, proj, proj, small, w2, bg, gn, s0, prev)


def _ssd_body(z_ref, x_ref, bm_ref, cm_ref, sm_ref, wx_ref, wb_ref, wc_ref, bx_ref, bb_ref, bc_ref,
              sx_ref, sb_ref, sc_ref, dtb_ref, alog_ref, dskip_ref, nrm_ref, h0_ref, prev_ref,
              y_ref, hout_ref, cxo_ref, cbo_ref, cco_ref,
              h_ref, xpx_ref, xpb_ref, xpc_ref, xs_ref, *, chunk, nchunks):
    del prev_ref
    t = pl.program_id(1)
    hist = CONV_PAD - (SSM_CONV - 1)

    @pl.when(t == 0)
    def _():
        h_ref[...] = h0_ref[0]
        xpx_ref[hist:CONV_PAD, :] = sx_ref[0]
        xpb_ref[hist:CONV_PAD, :] = sb_ref[0]
        xpc_ref[hist:CONV_PAD, :] = sc_ref[0]

    def conv_silu(xp_ref, cur_ref, w_ref, bias_ref):
        xp_ref[CONV_PAD:CONV_PAD + chunk, :] = cur_ref[...]
        acc = bias_ref[...] + xp_ref[pl.ds(hist, chunk), :] * w_ref[0:1, :]
        for i in range(1, SSM_CONV):
            acc = acc + xp_ref[pl.ds(hist + i, chunk), :] * w_ref[i:i + 1, :]
        xp_ref[hist:CONV_PAD, :] = xp_ref[chunk + hist:chunk + CONV_PAD, :]
        return acc * _sig(acc)

    xs_ref[...] = conv_silu(xpx_ref, x_ref, wx_ref, bx_ref)
    bmat = conv_silu(xpb_ref, bm_ref, wb_ref, bb_ref).astype(bf16)
    cmat = conv_silu(xpc_ref, cm_ref, wc_ref, bc_ref).astype(bf16)

    @pl.when(t == nchunks - 1)
    def _():
        cxo_ref[0] = xpx_ref[hist:CONV_PAD, :]
        cbo_ref[0] = xpb_ref[hist:CONV_PAD, :]
        cco_ref[0] = xpc_ref[hist:CONV_PAD, :]

    dt = _softplus(sm_ref[...] + dtb_ref[...])
    cum = _cumsum_rows(dt * -jnp.exp(alog_ref[...]))
    last = cum[chunk - 1:chunk, :]
    w_end = jnp.exp(last - cum) * dt
    e_cum = jnp.exp(cum)
    e_last = jnp.exp(last)
    cum2 = cum * LOG2E
    adj2_t = ((cum - jnp.log(dt)) * LOG2E).T

    row = lax.broadcasted_iota(jnp.int32, (chunk, chunk), 0)
    col = lax.broadcasted_iota(jnp.int32, (chunk, chunk), 1)
    causal = row >= col
    head = lax.broadcasted_iota(jnp.int32, (LANES, SSM_GROUP_W), 0)
    chan = lax.broadcasted_iota(jnp.int32, (LANES, SSM_GROUP_W), 1)
    nt = (((1,), (1,)), ((), ()))
    tn = (((0,), (0,)), ((), ()))
    for g in range(SSM_GROUPS):
        cs = slice(g * SSM_GROUP_W, (g + 1) * SSM_GROUP_W)
        ns = slice(g * SSM_D_STATE, (g + 1) * SSM_D_STATE)
        xs = xs_ref[:, cs]
        xsb = xs.astype(bf16)
        b_g = bmat[:, ns]
        c_g = cmat[:, ns]
        widen = (head == g * SSM_HPG + chan // SSM_HEAD_DIM).astype(bf16)
        w_full = _dot_exact01(w_end, widen, 2)
        ecum_full = _dot_exact01(e_cum, widen, 2)

        cb = lax.dot_general(c_g, b_g, nt, preferred_element_type=f32)
        state = h_ref[g]
        y = lax.dot_general(c_g, state.astype(bf16), nt, preferred_element_type=f32) * ecum_full
        parts = []
        for j in range(SSM_HPG):
            hd = g * SSM_HPG + j
            seg = jnp.exp2(jnp.where(causal, cum2[:, hd:hd + 1] - adj2_t[hd:hd + 1, :], -jnp.inf))
            parts.append(jnp.dot((cb * seg).astype(bf16), xsb[:, j * SSM_HEAD_DIM:(j + 1) * SSM_HEAD_DIM],
                                 preferred_element_type=f32))
        y = y + jnp.concatenate(parts, axis=1)

        upd = lax.dot_general((xs * w_full).astype(bf16), b_g, tn, preferred_element_type=f32)
        for j in range(SSM_HPG):
            hd = g * SSM_HPG + j
            rs = slice(j * SSM_HEAD_DIM, (j + 1) * SSM_HEAD_DIM)
            h_ref[g, rs, :] = state[rs, :] * e_last[:, hd:hd + 1] + upd[rs, :]

        y = y + dskip_ref[:, cs] * xs
        z = z_ref[:, cs]
        y = y * (z * _sig(z))
        y = y * lax.rsqrt(jnp.mean(y * y, axis=1, keepdims=True) + EPS) * nrm_ref[:, cs]
        y_ref[:, cs] = y.astype(y_ref.dtype)

    @pl.when(t == nchunks - 1)
    def _():
        hout_ref[0] = h_ref[...]


def _ssd(proj, small, conv_w, conv_b, l, conv_s, l_cs, dtb, alog, dskip, nrm, h0, l_h0, prev,
         *, row0, nseq, seqlen, chunk):
    nchunks = seqlen // chunk
    blk0 = row0 // chunk
    di, bc = SSM_D_INNER, SSM_BC
    rows = lambda b, t: blk0 + b * nchunks + t
    taps, hist = SSM_CONV, SSM_CONV - 1
    xb, bb, cb = 0, di // bc, di // bc + 1
    hs = (SSM_GROUPS, SSM_GROUP_W, SSM_D_STATE)
    fix3 = lambda b, t: (l, 0, 0)
    in_specs = [
        pl.BlockSpec((chunk, di), lambda b, t: (rows(b, t), P_Z // di)),
        pl.BlockSpec((chunk, di), lambda b, t: (rows(b, t), P_X // di)),
        pl.BlockSpec((chunk, bc), lambda b, t: (rows(b, t), P_B // bc)),
        pl.BlockSpec((chunk, bc), lambda b, t: (rows(b, t), P_C // bc)),
        pl.BlockSpec((chunk, LANES), lambda b, t: (rows(b, t), 0)),
        pl.BlockSpec((None, taps, di), lambda b, t: (l, 0, xb)),
        pl.BlockSpec((None, taps, bc), lambda b, t: (l, 0, bb)),
        pl.BlockSpec((None, taps, bc), lambda b, t: (l, 0, cb)),
        pl.BlockSpec((None, 1, di), lambda b, t: (l, 0, xb)),
        pl.BlockSpec((None, 1, bc), lambda b, t: (l, 0, bb)),
        pl.BlockSpec((None, 1, bc), lambda b, t: (l, 0, cb)),
        pl.BlockSpec((None, 1, hist, di), lambda b, t: (l_cs, b, 0, xb)),
        pl.BlockSpec((None, 1, hist, bc), lambda b, t: (l_cs, b, 0, bb)),
        pl.BlockSpec((None, 1, hist, bc), lambda b, t: (l_cs, b, 0, cb)),
        pl.BlockSpec((None, 1, LANES), fix3),
        pl.BlockSpec((None, 1, LANES), fix3),
        pl.BlockSpec((None, 1, di), fix3),
        pl.BlockSpec((None, 1, di), fix3),
        pl.BlockSpec((None, 1) + hs, lambda b, t: (l_h0, b, 0, 0, 0)),
        pl.BlockSpec(memory_space=pl.ANY),
    ]
    return pl.pallas_call(
        functools.partial(_ssd_body, chunk=chunk, nchunks=nchunks),
        grid=(nseq, nchunks),
        in_specs=in_specs,
        out_specs=[pl.BlockSpec((chunk, di), lambda b, t: (rows(b, t), 0)),
                   pl.BlockSpec((1,) + hs, lambda b, t: (b, 0, 0, 0)),
                   pl.BlockSpec((1, hist, di), lambda b, t: (b, 0, 0)),
                   pl.BlockSpec((1, hist, bc), lambda b, t: (b, 0, 0)),
                   pl.BlockSpec((1, hist, bc), lambda b, t: (b, 0, 0))],
        out_shape=[jax.ShapeDtypeStruct(prev.shape, bf16),
                   jax.ShapeDtypeStruct((nseq,) + hs, f32),
                   jax.ShapeDtypeStruct((nseq, hist, di), f32),
                   jax.ShapeDtypeStruct((nseq, hist, bc), f32),
                   jax.ShapeDtypeStruct((nseq, hist, bc), f32)],
        scratch_shapes=[pltpu.VMEM(hs, f32), pltpu.VMEM((chunk + CONV_PAD, di), f32),
                        pltpu.VMEM((chunk + CONV_PAD, bc), f32), pltpu.VMEM((chunk + CONV_PAD, bc), f32),
                        pltpu.VMEM((chunk, di), f32)],
        input_output_aliases={19: 0},
        compiler_params=_params("arbitrary", "arbitrary"),
        name="ssd")(proj, proj, proj, proj, small, conv_w, conv_w, conv_w, conv_b, conv_b, conv_b,
                    conv_s, conv_s, conv_s, dtb, alog, dskip, nrm, h0, prev)


def _pad_lanes(v):
    return jnp.pad(v, ((0, 0), (0, LANES - v.shape[-1])))[:, None, :]


def kernel(x_prompt, x_sample, state_gla, state_ssm, state_conv, norm_mix, w_in, gla_w_gate2, gla_b_gate,
           gla_norm, gla_w_out, ssm_conv_w, ssm_conv_b, ssm_dt_bias, ssm_a_log, ssm_d, ssm_norm, ssm_w_out,
           w_o, norm_ffn, router_w1, router_b1, router_w2, router_b2, exp_w_gate, exp_w_up, exp_w_down,
           norm_final):
    n_p, t_p = x_prompt.shape[:2]
    n_s, t_s = x_sample.shape[:2]
    assert n_p == 1
    rows_p, rows_s = n_p * t_p, n_s * t_s
    m = rows_p + rows_s
    tm = 832 if m % 832 == 0 else 128
    tm_out = 416 if m % 416 == 0 else 128
    gla_chunk_p, ssd_chunk_p = min(64, t_p), min(256, t_p)

    x = jnp.concatenate([x_prompt.reshape(rows_p, D_MODEL), x_sample.reshape(rows_s, D_MODEL)], axis=0)
    zeros_gla = jnp.zeros((1, n_p, GLA_HEADS, GLA_DK, GLA_DV), f32)
    zeros_ssm = jnp.zeros((1, n_p, SSM_GROUPS, SSM_GROUP_W, SSM_D_STATE), f32)
    zeros_conv = jnp.zeros((1, n_p, SSM_CONV - 1, SSM_CONV_DIM), f32)
    state_ssm_g = state_ssm.reshape(DEPTH, n_s, SSM_GROUPS, SSM_GROUP_W, SSM_D_STATE)

    bg3 = gla_b_gate[:, None, :]
    gn3 = gla_norm[:, None, :]
    conv_b3 = ssm_conv_b[:, None, :]
    dtb3 = _pad_lanes(ssm_dt_bias)
    alog3 = _pad_lanes(ssm_a_log)
    dskip3 = jnp.repeat(ssm_d, SSM_HEAD_DIM, axis=1)[:, None, :]
    nrm3 = ssm_norm[:, None, :]

    w_in_t = jnp.swapaxes(w_in, 1, 2)

    outs = {k: [] for k in ("gla_p", "ssm_p", "conv_p", "gla_s", "ssm_s", "conv_s")}
    for l in range(DEPTH):
        w_small_t = jnp.concatenate(
            [w_in_t[l, W_DT:W_MG], w_in_t[l, W_GLR:W_Z],
             jnp.zeros((LANES - SSM_HEADS - GLA_GATE_RANK, D_MODEL), f32)], axis=0)

        u = _rmsnorm(x, norm_mix[l], rows=m, row0=0, tm=tm, out_dtype=bf16)
        proj = _inproj(u, w_in_t, l, tm=tm)
        small = _matmul_nt(u, w_small_t, tm=tm, name="inproj_small")

        gla_args = (proj, small, gla_w_gate2, bg3, gn3, l)
        oa = jnp.zeros((m, GLA_VW), bf16)
        oa, g_p = _gla(*gla_args, zeros_gla, 0, oa, row0=0, nseq=n_p, seqlen=t_p, chunk=gla_chunk_p)
        oa, g_s = _gla(*gla_args, state_gla, l, oa, row0=rows_p, nseq=n_s, seqlen=t_s, chunk=t_s)

        ssd_args = (proj, small, ssm_conv_w, conv_b3, l)
        ssd_tail = (dtb3, alog3, dskip3, nrm3)
        ob = jnp.zeros((m, SSM_D_INNER), bf16)
        ob, h_p, *conv_p = _ssd(*ssd_args, zeros_conv, 0, *ssd_tail, zeros_ssm, 0, ob,
                                row0=0, nseq=n_p, seqlen=t_p, chunk=ssd_chunk_p)
        ob, h_s, *conv_s = _ssd(*ssd_args, state_conv, l, *ssd_tail, state_ssm_g, l, ob,
                                row0=rows_p, nseq=n_s, seqlen=t_s, chunk=t_s)

        merged = _merge(oa, ob, proj, gla_w_out, ssm_w_out, l, tm=tm_out, tn=512)

        wr = jnp.concatenate([router_w1[l], router_w2[l],
                              jnp.zeros((D_MODEL, LANES - MOE_GROUPS - MOE_EXPERTS), f32)], axis=1)
        br = jnp.pad(jnp.concatenate([router_b1[l], router_b2[l]]),
                     (0, LANES - MOE_GROUPS - MOE_EXPERTS)).reshape(1, LANES)
        xn, u2, cw, oh = _outproj(merged, w_o, l, x, norm_ffn[l].reshape(1, -1), wr, br, tm=tm_out)
        x = _moe(u2, cw, oh, xn, exp_w_gate, exp_w_up, exp_w_down, l)

        outs["gla_p"].append(g_p)
        outs["gla_s"].append(g_s)
        outs["ssm_p"].append(h_p.reshape(n_p, SSM_HEADS, SSM_HEAD_DIM, SSM_D_STATE))
        outs["ssm_s"].append(h_s.reshape(n_s, SSM_HEADS, SSM_HEAD_DIM, SSM_D_STATE))
        outs["conv_p"].append(jnp.concatenate(conv_p, axis=-1))
        outs["conv_s"].append(jnp.concatenate(conv_s, axis=-1))

    tm_p = 512 if rows_p % 512 == 0 else 128
    tm_s = rows_s if rows_p % rows_s == 0 else 8
    y_p = _rmsnorm(x, norm_final, rows=rows_p, row0=0, tm=tm_p, out_dtype=f32)
    y_s = _rmsnorm(x, norm_final, rows=rows_s, row0=rows_p, tm=tm_s, out_dtype=f32)
    return (y_p.reshape(n_p, t_p, D_MODEL), y_s.reshape(n_s, t_s, D_MODEL),
            jnp.stack(outs["gla_p"]), jnp.stack(outs["ssm_p"]), jnp.stack(outs["conv_p"]),
            jnp.stack(outs["gla_s"]), jnp.stack(outs["ssm_s"]), jnp.stack(outs["conv_s"]))
```

```python
import functools
import math

import jax
import jax.numpy as jnp
from jax import lax
from jax.experimental import pallas as pl
from jax.experimental.pallas import tpu as pltpu

f32 = jnp.float32
bf16 = jnp.bfloat16
HIGHEST = lax.Precision.HIGHEST
LOG2E = math.log2(math.e)

D_MODEL = 2048
DEPTH = 2
EPS = 1e-6
GLA_HEADS = 4
GLA_DK = 256
GLA_DV = 512
GLA_QK = GLA_HEADS * GLA_DK
GLA_VW = GLA_HEADS * GLA_DV
GLA_GATE_RANK = 16
GLA_GATE_TAU = 16.0
GLA_SUB = 16
SSM_D_INNER = 4096
SSM_HEAD_DIM = 64
SSM_HEADS = 64
SSM_GROUPS = 8
SSM_HPG = 8
SSM_GROUP_W = SSM_HPG * SSM_HEAD_DIM
SSM_D_STATE = 128
SSM_BC = SSM_GROUPS * SSM_D_STATE
SSM_CONV = 4
SSM_CONV_DIM = 6144
MOE_GROUPS = 4
MOE_PER_GROUP = 4
MOE_EXPERTS = 16
MOE_D_FF = 512
MOE_TOP_K = 2
MOE_LANE0 = MOE_GROUPS
MOE_TILE = 256
LANES = 128
CONV_PAD = 8

W_GLR = 6144
W_Z = 6160
W_DT = 16400
W_MG = 16464
IN_TN = 1024
IN_TILES = 20
IN_ALIGNED_TILES = W_GLR // IN_TN
IN_MG_TILE0 = (W_DT - W_Z + W_GLR) // IN_TN
IN_SHIFT_Z = W_Z - W_GLR
IN_SHIFT_MG = W_MG - IN_MG_TILE0 * IN_TN
P_Z, P_X, P_Q, P_K, P_V, P_R, P_B, P_C, P_GA, P_GB, P_END = (
    0, 4096, 8192, 9216, 10240, 12288, 14336, 15360, 16384, 18432, 20480)
S_GLR = 64

VMEM_LIMIT = 56 * 1024 * 1024


def _sig(x):
    return 1.0 / (1.0 + jnp.exp(-x))


def _softplus(x):
    return jnp.maximum(x, 0.0) + jnp.log1p(jnp.exp(-jnp.abs(x)))


def _log_sigmoid(x):
    return jnp.minimum(x, 0.0) - jnp.log1p(jnp.exp(-jnp.abs(x)))


def _params(*sem):
    return pltpu.CompilerParams(dimension_semantics=sem, vmem_limit_bytes=VMEM_LIMIT)


def _split_bf16(x, pieces):
    out = []
    for _ in range(pieces - 1):
        p = x.astype(bf16)
        out.append(p)
        x = x - p.astype(f32)
    out.append(x.astype(bf16))
    return out


def _dot_exact01(a, b01, pieces, dims=None):
    acc = None
    for p in _split_bf16(a, pieces):
        if dims is None:
            d = jnp.dot(p, b01, preferred_element_type=f32)
        else:
            d = lax.dot_general(p, b01, dims, preferred_element_type=f32)
        acc = d if acc is None else acc + d
    return acc


def _cumsum_rows(x):
    n = x.shape[0]
    row = lax.broadcasted_iota(jnp.int32, (n, n), 0)
    col = lax.broadcasted_iota(jnp.int32, (n, n), 1)
    tri = (row >= col).astype(bf16)
    acc = None
    for p in _split_bf16(x, 3):
        d = jnp.dot(tri, p, preferred_element_type=f32)
        acc = d if acc is None else acc + d
    return acc


def _norm_body(x_ref, g_ref, o_ref):
    x = x_ref[...]
    y = x * lax.rsqrt(jnp.mean(x * x, axis=-1, keepdims=True) + EPS)
    o_ref[...] = (y * g_ref[...]).astype(o_ref.dtype)


def _rmsnorm(x, g, *, rows, row0, tm, out_dtype):
    blk0 = row0 // tm
    return pl.pallas_call(
        _norm_body, grid=(rows // tm,),
        in_specs=[pl.BlockSpec((tm, D_MODEL), lambda m: (blk0 + m, 0)),
                  pl.BlockSpec((1, D_MODEL), lambda m: (0, 0))],
        out_specs=pl.BlockSpec((tm, D_MODEL), lambda m: (m, 0)),
        out_shape=jax.ShapeDtypeStruct((rows, D_MODEL), out_dtype),
        compiler_params=_params("arbitrary"), name="rmsnorm")(x, g.reshape(1, D_MODEL))


def _inproj_body(u_ref, wa_ref, wb_ref, o_ref, wbf_ref):
    n = pl.program_id(0)

    @pl.when(pl.program_id(1) == 0)
    def _():
        def fill(shift):
            def f():
                if shift == 0:
                    wbf_ref[...] = wa_ref[...].astype(bf16)
                else:
                    wbf_ref[:IN_TN - shift, :] = wa_ref[shift:, :].astype(bf16)
                    wbf_ref[IN_TN - shift:, :] = wb_ref[:shift, :].astype(bf16)
            return f
        pl.when(n < IN_ALIGNED_TILES)(fill(0))
        pl.when((n >= IN_ALIGNED_TILES) & (n < IN_MG_TILE0))(fill(IN_SHIFT_Z))
        pl.when(n >= IN_MG_TILE0)(fill(IN_SHIFT_MG))

    o_ref[...] = lax.dot_general(u_ref[...], wbf_ref[...], (((1,), (1,)), ((), ())),
                                 preferred_element_type=f32)


def _inproj_out_tile(n):
    lead = IN_ALIGNED_TILES
    zx = (P_Q - P_Z) // IN_TN
    return jnp.where(n < lead, n + zx, jnp.where(n < lead + zx, n - lead, n))


def _inproj(u, w_in_t, l, *, tm):
    m = u.shape[0]
    return pl.pallas_call(
        _inproj_body, grid=(IN_TILES, m // tm),
        in_specs=[pl.BlockSpec((tm, D_MODEL), lambda n, i: (i, 0)),
                  pl.BlockSpec((None, IN_TN, D_MODEL), lambda n, i: (l, n, 0)),
                  pl.BlockSpec((None, LANES, D_MODEL), lambda n, i: (l, (n + 1) * (IN_TN // LANES), 0))],
        out_specs=pl.BlockSpec((tm, IN_TN), lambda n, i: (i, _inproj_out_tile(n))),
        out_shape=jax.ShapeDtypeStruct((m, P_END), f32),
        scratch_shapes=[pltpu.VMEM((IN_TN, D_MODEL), bf16)],
        compiler_params=_params("arbitrary", "arbitrary"), name="inproj")(u, w_in_t, w_in_t)


def _mm_nt_body(a_ref, w_ref, o_ref):
    o_ref[...] = lax.dot_general(a_ref[...], w_ref[...].astype(bf16), (((1,), (1,)), ((), ())),
                                 preferred_element_type=f32)


def _matmul_nt(a, w_t, *, tm, name):
    m, k = a.shape
    n = w_t.shape[0]
    return pl.pallas_call(
        _mm_nt_body, grid=(m // tm,),
        in_specs=[pl.BlockSpec((tm, k), lambda i: (i, 0)),
                  pl.BlockSpec((n, k), lambda i: (0, 0))],
        out_specs=pl.BlockSpec((tm, n), lambda i: (i, 0)),
        out_shape=jax.ShapeDtypeStruct((m, n), f32),
        compiler_params=_params("arbitrary"), name=name)(a, w_t)


def _merge_body(oa_ref, ob_ref, ga_ref, gb_ref, wa_ref, wb_ref, o_ref, wabf_ref, wbbf_ref):
    @pl.when(pl.program_id(1) == 0)
    def _():
        wabf_ref[...] = wa_ref[...].astype(bf16)
        wbbf_ref[...] = wb_ref[...].astype(bf16)

    ya = jnp.dot(oa_ref[...], wabf_ref[...], preferred_element_type=f32)
    yb = jnp.dot(ob_ref[...], wbbf_ref[...], preferred_element_type=f32)
    o_ref[...] = (_sig(ga_ref[...]) * ya + _sig(gb_ref[...]) * yb).astype(o_ref.dtype)


def _merge(oa, ob, proj, wa, wb, l, *, tm, tn):
    m = oa.shape[0]
    ga0, gb0 = P_GA // tn, P_GB // tn
    return pl.pallas_call(
        _merge_body, grid=(D_MODEL // tn, m // tm),
        in_specs=[pl.BlockSpec((tm, GLA_VW), lambda j, i: (i, 0)),
                  pl.BlockSpec((tm, SSM_D_INNER), lambda j, i: (i, 0)),
                  pl.BlockSpec((tm, tn), lambda j, i: (i, ga0 + j)),
                  pl.BlockSpec((tm, tn), lambda j, i: (i, gb0 + j)),
                  pl.BlockSpec((None, GLA_VW, tn), lambda j, i: (l, 0, j)),
                  pl.BlockSpec((None, SSM_D_INNER, tn), lambda j, i: (l, 0, j))],
        out_specs=pl.BlockSpec((tm, tn), lambda j, i: (i, j)),
        out_shape=jax.ShapeDtypeStruct((m, D_MODEL), bf16),
        scratch_shapes=[pltpu.VMEM((GLA_VW, tn), bf16), pltpu.VMEM((SSM_D_INNER, tn), bf16)],
        compiler_params=_params("arbitrary", "arbitrary"), name="merge")(oa, ob, proj, proj, wa, wb)


def _route(logits):
    c = lax.broadcasted_iota(jnp.int32, logits.shape, 1)
    big = jnp.int32(1 << 20)
    is_group = c < MOE_GROUPS
    lg = jnp.where(is_group, logits, -jnp.inf)
    top = jnp.max(lg, axis=1, keepdims=True)
    gsel = jnp.min(jnp.where(is_group & (logits == top), c, big), axis=1, keepdims=True)
    p_sel = 1.0 / jnp.sum(jnp.exp(lg - top), axis=1, keepdims=True)
    in_group = ((c >= MOE_LANE0) & (c < MOE_LANE0 + MOE_EXPERTS)
                & (((c - MOE_LANE0) // MOE_PER_GROUP) == gsel))
    le = jnp.where(in_group, logits, -jnp.inf)
    tv0 = jnp.max(le, axis=1, keepdims=True)
    i0 = jnp.min(jnp.where(in_group & (logits == tv0), c, big), axis=1, keepdims=True)
    rest = in_group & (c != i0)
    le1 = jnp.where(rest, logits, -jnp.inf)
    tv1 = jnp.max(le1, axis=1, keepdims=True)
    i1 = jnp.min(jnp.where(rest & (logits == tv1), c, big), axis=1, keepdims=True)
    t = jnp.exp(tv1 - tv0)
    w0 = p_sel / (1.0 + t)
    w1 = p_sel * t / (1.0 + t)
    cw = jnp.where(c == i0, w0, jnp.where(c == i1, w1, 0.0))
    return cw, ((c == i0) | (c == i1)).astype(f32)


def _outproj_body(m_ref, wo_ref, x_ref, g_ref, wr_ref, br_ref, xn_ref, u_ref, cw_ref, oh_ref, cnt_ref,
                  wobf_ref):
    @pl.when(pl.program_id(0) == 0)
    def _():
        wobf_ref[...] = wo_ref[...].astype(bf16)
        cnt_ref[...] = jnp.zeros_like(cnt_ref)

    w_hi, w_lo = _split_bf16(wr_ref[...], 2)
    tm = x_ref.shape[0]
    parts = 2 if tm % 32 == 0 else 1
    for p in range(parts):
        rs = slice(p * (tm // parts), (p + 1) * (tm // parts))
        xn = x_ref[rs, :] + jnp.dot(m_ref[rs, :], wobf_ref[...], preferred_element_type=f32)
        xn_ref[rs, :] = xn
        u = xn * lax.rsqrt(jnp.mean(xn * xn, axis=-1, keepdims=True) + EPS) * g_ref[...]
        u_ref[rs, :] = u
        u_hi, u_lo = _split_bf16(u, 2)
        logits = (jnp.dot(u_hi, w_hi, preferred_element_type=f32)
                  + (jnp.dot(u_hi, w_lo, preferred_element_type=f32)
                     + jnp.dot(u_lo, w_hi, preferred_element_type=f32))) + br_ref[...]
        cw, oh = _route(logits)
        cw_ref[rs, :] = cw
        oh_ref[rs, :] = oh
        cnt_ref[...] += jnp.sum(oh, axis=0, keepdims=True)


def _outproj(merged, wo, l, x, g, wr, br, *, tm):
    m = x.shape[0]
    row = lambda i: (i, 0)
    fix = lambda i: (0, 0)
    return pl.pallas_call(
        _outproj_body, grid=(m // tm,),
        in_specs=[pl.BlockSpec((tm, D_MODEL), row),
                  pl.BlockSpec((None, D_MODEL, D_MODEL), lambda i: (l, 0, 0), pipeline_mode=pl.Buffered(1)),
                  pl.BlockSpec((tm, D_MODEL), row), pl.BlockSpec((1, D_MODEL), fix),
                  pl.BlockSpec((D_MODEL, LANES), fix), pl.BlockSpec((1, LANES), fix)],
        out_specs=[pl.BlockSpec((tm, D_MODEL), row), pl.BlockSpec((tm, D_MODEL), row),
                   pl.BlockSpec((tm, LANES), row), pl.BlockSpec((tm, LANES), row), pl.BlockSpec((1, LANES), fix)],
        out_shape=[jax.ShapeDtypeStruct((m, D_MODEL), f32), jax.ShapeDtypeStruct((m, D_MODEL), f32),
                   jax.ShapeDtypeStruct((m, LANES), f32), jax.ShapeDtypeStruct((m, LANES), f32),
                   jax.ShapeDtypeStruct((1, LANES), f32)],
        scratch_shapes=[pltpu.VMEM((D_MODEL, D_MODEL), bf16)],
        compiler_params=_params("arbitrary"), name="outproj")(merged, wo, x, g, wr, br)


def _plan_body(oh_ref, cw_ref, cnt_ref, dest_ref, wts_ref, te_ref, pad_ref, base_ref, carry_ref, *, rows):
    lane = lax.broadcasted_iota(jnp.int32, (8, LANES), 1)
    is_expert = (lane >= MOE_LANE0) & (lane < MOE_LANE0 + MOE_EXPERTS)

    @pl.when(pl.program_id(0) == 0)
    def _():
        cnt = jnp.broadcast_to(cnt_ref[...], (8, LANES))
        padded = jnp.floor((cnt + (MOE_TILE - 1)) * (1.0 / MOE_TILE)) * MOE_TILE
        src = lax.broadcasted_iota(jnp.int32, (LANES, LANES), 0)
        dst = lax.broadcasted_iota(jnp.int32, (LANES, LANES), 1)
        base = _dot_exact01(padded, (src < dst).astype(bf16), 3)
        base_ref[...] = base[0:1]
        carry_ref[...] = jnp.zeros_like(carry_ref)
        end = base + padded
        total = jnp.max(jnp.where(is_expert, end, 0.0), axis=1, keepdims=True)
        end_col = jnp.transpose(jnp.where(is_expert, end, jnp.inf))[:, 0:1]
        tile_start = (lax.broadcasted_iota(jnp.int32, (LANES, LANES), 1) * MOE_TILE).astype(f32)
        tile_exp = jnp.sum((end_col <= tile_start).astype(f32), axis=0, keepdims=True)
        tile_exp = jnp.minimum(tile_exp, MOE_EXPERTS - 1.0)
        te_ref[0:1, :] = tile_exp.astype(jnp.int32)
        te_ref[1:2, :] = jnp.broadcast_to(total[0:1] * (1.0 / MOE_TILE), (1, LANES)).astype(jnp.int32)
        shift = (src == dst + MOE_LANE0).astype(bf16)
        is_tail = lane[0:1] == MOE_EXPERTS
        pad_lo = jnp.where(is_tail, total[0:1], _dot_exact01(base + cnt, shift, 3)[0:1])
        pad_hi = jnp.where(is_tail, float(rows), _dot_exact01(end, shift, 3)[0:1])
        pad_ref[0:1, :] = pad_lo.astype(jnp.int32)
        pad_ref[1:2, :] = pad_hi.astype(jnp.int32)

    oh = oh_ref[...]
    n = oh.shape[0]
    row = lax.broadcasted_iota(jnp.int32, (n, n), 0)
    col = lax.broadcasted_iota(jnp.int32, (n, n), 1)
    rank = jnp.dot((row > col).astype(bf16), oh.astype(bf16), preferred_element_type=f32)
    pos = base_ref[...] + carry_ref[...] + rank
    carry_ref[...] += jnp.sum(oh, axis=0, keepdims=True)
    chosen = oh > 0.0
    d_lo = jnp.min(jnp.where(chosen, pos, jnp.inf), axis=1, keepdims=True)
    d_hi = jnp.max(jnp.where(chosen, pos, -1.0), axis=1, keepdims=True)
    cw = cw_ref[...]
    w_lo = jnp.sum(jnp.where(chosen & (pos == d_lo), cw, 0.0), axis=1, keepdims=True)
    w_hi = jnp.sum(jnp.where(chosen & (pos == d_hi), cw, 0.0), axis=1, keepdims=True)
    lanes = lax.broadcasted_iota(jnp.int32, oh.shape, 1)
    wts_ref[...] = jnp.where(lanes == 0, w_lo, jnp.where(lanes == 1, w_hi, 0.0))
    dests = jnp.where(lanes == 0, d_lo, jnp.where(lanes == 1, d_hi, 0.0))
    dest_ref[0] = jnp.transpose(dests)[0:2, :].astype(jnp.int32)


def _moe_plan(oh, cw, cnt, *, blk, rows):
    m = oh.shape[0]
    nblk = m // blk
    fix = lambda i: (0, 0)
    return pl.pallas_call(
        functools.partial(_plan_body, rows=rows), grid=(nblk,),
        in_specs=[pl.BlockSpec((blk, LANES), lambda i: (i, 0)), pl.BlockSpec((blk, LANES), lambda i: (i, 0)),
                  pl.BlockSpec((1, LANES), fix)],
        out_specs=[pl.BlockSpec((1, 2, blk), lambda i: (i, 0, 0)), pl.BlockSpec((blk, LANES), lambda i: (i, 0)),
                   pl.BlockSpec((2, LANES), fix), pl.BlockSpec((2, LANES), fix)],
        out_shape=[jax.ShapeDtypeStruct((nblk, 2, blk), jnp.int32), jax.ShapeDtypeStruct((m, LANES), f32),
                   jax.ShapeDtypeStruct((2, LANES), jnp.int32), jax.ShapeDtypeStruct((2, LANES), jnp.int32)],
        scratch_shapes=[pltpu.VMEM((1, LANES), f32), pltpu.VMEM((1, LANES), f32)],
        compiler_params=_params("arbitrary"), name="moe_plan")(oh, cw, cnt)


def _row_copy(src_ref, s, dst_ref, d, sem):
    return pltpu.make_async_copy(src_ref.at[pl.ds(s, 1), :], dst_ref.at[pl.ds(d, 1), :], sem)


def _dispatch_body(dest_ref, pad_ref, u_ref, xs_ref, zero_ref, sem, *, blk, nsub):
    i = pl.program_id(0)

    def wait_rows(count, which):
        pltpu.make_async_copy(xs_ref.at[pl.ds(0, count), :], xs_ref.at[pl.ds(0, count), :], sem.at[which]).wait()

    @pl.when(i == 0)
    def _():
        zero_ref[...] = jnp.zeros_like(zero_ref)
        group = zero_ref.shape[0]

        def fill_row(r, c):
            _row_copy(zero_ref, 0, xs_ref, r, sem.at[1]).start()
            return c

        def fill_group(lo8):
            def f(r, c):
                row0 = pl.multiple_of(lo8 + r * group, group)
                pltpu.make_async_copy(zero_ref, xs_ref.at[pl.ds(row0, group), :], sem.at[1]).start()
                return c
            return f

        nrow = jnp.int32(0)
        ngroup = jnp.int32(0)
        for e in range(MOE_EXPERTS + 1):
            lo = pad_ref[0, e]
            hi = pad_ref[1, e]
            lo8 = jnp.minimum((lo + group - 1) // group * group, hi)
            lax.fori_loop(lo, lo8, fill_row, 0)
            lax.fori_loop(0, (hi - lo8) // group, fill_group(lo8), 0)
            nrow = nrow + (lo8 - lo)
            ngroup = ngroup + (hi - lo8) // group

        def wait_row(r, c):
            _row_copy(zero_ref, 0, xs_ref, 0, sem.at[1]).wait()
            return c
        lax.fori_loop(0, nrow, wait_row, 0)

        def wait_group(r, c):
            wait_rows(group, 1)
            return c
        lax.fori_loop(0, ngroup, wait_group, 0)

    for s in range(nsub):
        def send(j, c, s=s):
            _row_copy(u_ref, s * blk + j, xs_ref, dest_ref[s, 0, j], sem.at[0]).start()
            _row_copy(u_ref, s * blk + j, xs_ref, dest_ref[s, 1, j], sem.at[0]).start()
            return c
        lax.fori_loop(0, blk, send, 0)
    for _ in range(MOE_TOP_K * nsub):
        wait_rows(blk, 0)


def _moe_dispatch(u, dest, pad, *, rows, blk):
    m = u.shape[0]
    nblk = m // blk
    nsub = max(d for d in range(1, 6) if nblk % d == 0)
    return pl.pallas_call(
        functools.partial(_dispatch_body, blk=blk, nsub=nsub), grid=(nblk // nsub,),
        in_specs=[pl.BlockSpec((nsub, 2, blk), lambda i: (i, 0, 0), memory_space=pltpu.SMEM),
                  pl.BlockSpec(memory_space=pltpu.SMEM),
                  pl.BlockSpec((nsub * blk, D_MODEL), lambda i: (i, 0))],
        out_specs=pl.BlockSpec(memory_space=pl.ANY),
        out_shape=jax.ShapeDtypeStruct((rows, D_MODEL), f32),
        scratch_shapes=[pltpu.VMEM((8, D_MODEL), f32), pltpu.SemaphoreType.DMA((2,))],
        compiler_params=_params("arbitrary"), name="moe_dispatch")(dest, pad, u)


def _gmm_body(te_ref, xs_ref, wg_ref, wu_ref, wd_ref, ys_ref, wgb_ref, wub_ref, wdb_ref):
    i = pl.program_id(0)
    n_used = te_ref[1, 0]

    @pl.when(i < n_used)
    def _():
        @pl.when((i == 0) | (te_ref[0, i] != te_ref[0, jnp.maximum(i - 1, 0)]))
        def _():
            wgb_ref[...] = wg_ref[...].astype(bf16)
            wub_ref[...] = wu_ref[...].astype(bf16)
            wdb_ref[...] = wd_ref[...].astype(bf16)

        x = xs_ref[...].astype(bf16)
        hg = jnp.dot(x, wgb_ref[...], preferred_element_type=f32)
        hu = jnp.dot(x, wub_ref[...], preferred_element_type=f32)
        h = ((hg * _sig(hg)) * hu).astype(bf16)
        ys_ref[...] = jnp.dot(h, wdb_ref[...], preferred_element_type=f32)

    @pl.when(i >= n_used)
    def _():
        ys_ref[...] = jnp.zeros_like(ys_ref)


def _moe_gmm(te, xs, wg, wu, wd, l):
    rows = xs.shape[0]
    ntiles = rows // MOE_TILE
    tile = lambda i, te: (jnp.minimum(i, te[1, 0] - 1), 0)
    up = lambda i, te: (l, te[0, i], 0, 0)
    return pl.pallas_call(
        _gmm_body,
        grid_spec=pltpu.PrefetchScalarGridSpec(
            num_scalar_prefetch=1, grid=(ntiles,),
            in_specs=[pl.BlockSpec((MOE_TILE, D_MODEL), tile),
                      pl.BlockSpec((None, None, D_MODEL, MOE_D_FF), up),
                      pl.BlockSpec((None, None, D_MODEL, MOE_D_FF), up),
                      pl.BlockSpec((None, None, MOE_D_FF, D_MODEL), up)],
            out_specs=pl.BlockSpec((MOE_TILE, D_MODEL), lambda i, te: (i, 0)),
            scratch_shapes=[pltpu.VMEM((D_MODEL, MOE_D_FF), bf16), pltpu.VMEM((D_MODEL, MOE_D_FF), bf16),
                            pltpu.VMEM((MOE_D_FF, D_MODEL), bf16)]),
        out_shape=jax.ShapeDtypeStruct((rows, D_MODEL), f32),
        compiler_params=_params("arbitrary"), name="moe_gmm")(te, xs, wg, wu, wd)


def _combine_body(dcur_ref, dnext_ref, xn_ref, wts_ref, ys_ref, o_ref, ya_ref, yb_ref, sem, *, blk, nblk):
    i = pl.program_id(0)
    slot = i % 2

    def gather(dref, s):
        def one(j, c):
            _row_copy(ys_ref, dref[0, 0, j], ya_ref.at[s], j, sem.at[s]).start()
            _row_copy(ys_ref, dref[0, 1, j], yb_ref.at[s], j, sem.at[s]).start()
            return c
        lax.fori_loop(0, blk, one, 0)

    @pl.when(i == 0)
    def _():
        gather(dcur_ref, 0)

    @pl.when(i + 1 < nblk)
    def _():
        gather(dnext_ref, 1 - slot)

    pltpu.make_async_copy(ys_ref.at[pl.ds(0, blk), :], ya_ref.at[slot], sem.at[slot]).wait()
    pltpu.make_async_copy(ys_ref.at[pl.ds(0, blk), :], yb_ref.at[slot], sem.at[slot]).wait()
    w = wts_ref[...]
    o_ref[...] = xn_ref[...] + w[:, 0:1] * ya_ref[slot] + w[:, 1:2] * yb_ref[slot]


def _moe_combine(dest, xn, wts, ys, *, blk):
    m = xn.shape[0]
    nblk = m // blk
    return pl.pallas_call(
        functools.partial(_combine_body, blk=blk, nblk=nblk), grid=(nblk,),
        in_specs=[pl.BlockSpec((1, 2, blk), lambda i: (i, 0, 0), memory_space=pltpu.SMEM),
                  pl.BlockSpec((1, 2, blk), lambda i: (jnp.minimum(i + 1, nblk - 1), 0, 0),
                               memory_space=pltpu.SMEM),
                  pl.BlockSpec((blk, D_MODEL), lambda i: (i, 0)),
                  pl.BlockSpec((blk, LANES), lambda i: (i, 0)),
                  pl.BlockSpec(memory_space=pl.ANY)],
        out_specs=pl.BlockSpec((blk, D_MODEL), lambda i: (i, 0)),
        out_shape=jax.ShapeDtypeStruct((m, D_MODEL), f32),
        scratch_shapes=[pltpu.VMEM((2, blk, D_MODEL), f32), pltpu.VMEM((2, blk, D_MODEL), f32),
                        pltpu.SemaphoreType.DMA((2,))],
        compiler_params=_params("arbitrary"), name="moe_combine")(dest, dest, xn, wts, ys)


def _moe(u, cw, oh, cnt, xn, wg, wu, wd, l):
    m = u.shape[0]
    blk = LANES
    rows = -(-(MOE_TOP_K * m + MOE_EXPERTS * (MOE_TILE - 1)) // MOE_TILE) * MOE_TILE
    assert rows // MOE_TILE <= LANES
    dest, wts, te, pad = _moe_plan(oh, cw, cnt, blk=blk, rows=rows)
    xs = _moe_dispatch(u, dest, pad, rows=rows, blk=blk)
    ys = _moe_gmm(te, xs, wg, wu, wd, l)
    return _moe_combine(dest, xn, wts, ys, blk=blk)


def _gla_body(q_ref, k_ref, v_ref, r_ref, sm_ref, w2_ref, bg_ref, gn_ref, s0_ref, prev_ref,
              o_ref, sout_ref, st_ref, b_ref, att_ref, *, chunk, nchunks):
    del prev_ref
    t = pl.program_id(1)

    @pl.when(t == 0)
    def _():
        st_ref[...] = s0_ref[0]

    glr = sm_ref[:, S_GLR:S_GLR + GLA_GATE_RANK].astype(bf16)
    pre = jnp.dot(glr, w2_ref[...].astype(bf16), preferred_element_type=f32) + bg_ref[...]
    b_ref[...] = _cumsum_rows(_log_sigmoid(pre) * (1.0 / GLA_GATE_TAU))

    sub_row = lax.broadcasted_iota(jnp.int32, (GLA_SUB, GLA_DK), 0)
    sub_col = lax.broadcasted_iota(jnp.int32, (GLA_SUB, GLA_SUB), 1)
    nt = (((1,), (1,)), ((), ()))
    tn = (((0,), (0,)), ((), ()))
    for h in range(GLA_HEADS):
        ks = slice(h * GLA_DK, (h + 1) * GLA_DK)
        vs = slice(h * GLA_DV, (h + 1) * GLA_DV)
        q = q_ref[:, ks] * (GLA_DK ** -0.5)
        k = k_ref[:, ks]
        vb = v_ref[:, vs].astype(bf16)
        b = b_ref[:, ks]
        b_last = b_ref[chunk - 1:chunk, ks]
        state = st_ref[h]
        o = jnp.dot((q * jnp.exp(b)).astype(bf16), state.astype(bf16), preferred_element_type=f32)

        att_ref[h] = jnp.zeros((chunk, chunk), f32)
        for blk in range(chunk // GLA_SUB):
            r0 = blk * GLA_SUB
            q_i = q[r0:r0 + GLA_SUB]
            b_i = b[r0:r0 + GLA_SUB]
            if blk > 0:
                c_i = b_ref[r0 - 1:r0, ks]
                q_dec = (q_i * jnp.exp(b_i - c_i)).astype(bf16)
                k_dec = (k[:r0] * jnp.exp(c_i - b[:r0])).astype(bf16)
                att_ref[h, r0:r0 + GLA_SUB, 0:r0] = lax.dot_general(q_dec, k_dec, nt,
                                                                   preferred_element_type=f32)
            diag = jnp.zeros((GLA_SUB, GLA_SUB), f32)
            for j in range(GLA_SUB):
                b_j = b_ref[r0 + j:r0 + j + 1, ks]
                k_j = k_ref[r0 + j:r0 + j + 1, ks]
                decay = jnp.exp(jnp.where(sub_row >= j, b_i - b_j, -jnp.inf))
                a_j = jnp.sum(q_i * decay * k_j, axis=1, keepdims=True)
                diag = jnp.where(sub_col == j, a_j, diag)
            att_ref[h, r0:r0 + GLA_SUB, r0:r0 + GLA_SUB] = diag

        o = o + jnp.dot(att_ref[h].astype(bf16), vb, preferred_element_type=f32)

        k_end = (k * jnp.exp(b_last - b)).astype(bf16)
        upd = lax.dot_general(k_end, vb, tn, preferred_element_type=f32)
        dec_col = jnp.transpose(jnp.broadcast_to(jnp.exp(b_last), (8, GLA_DK)))[:, 0:1]
        st_ref[h] = state * dec_col + upd

        o = o * lax.rsqrt(jnp.mean(o * o, axis=1, keepdims=True) + EPS) * gn_ref[...]
        r = r_ref[:, vs]
        o_ref[:, vs] = (o * (r * _sig(r))).astype(o_ref.dtype)

    @pl.when(t == nchunks - 1)
    def _():
        sout_ref[0] = st_ref[...]


def _gla(proj, small, w2, bg, gn, l, s0, l_s0, prev, *, row0, nseq, seqlen, chunk):
    nchunks = seqlen // chunk
    blk0 = row0 // chunk
    rows = lambda b, t: blk0 + b * nchunks + t
    qk, vw = GLA_QK, GLA_VW
    hd = (GLA_HEADS, GLA_DK, GLA_DV)
    return pl.pallas_call(
        functools.partial(_gla_body, chunk=chunk, nchunks=nchunks),
        grid=(nseq, nchunks),
        in_specs=[pl.BlockSpec((chunk, qk), lambda b, t: (rows(b, t), P_Q // qk)),
                  pl.BlockSpec((chunk, qk), lambda b, t: (rows(b, t), P_K // qk)),
                  pl.BlockSpec((chunk, vw), lambda b, t: (rows(b, t), P_V // vw)),
                  pl.BlockSpec((chunk, vw), lambda b, t: (rows(b, t), P_R // vw)),
                  pl.BlockSpec((chunk, LANES), lambda b, t: (rows(b, t), 0)),
                  pl.BlockSpec((None, GLA_GATE_RANK, qk), lambda b, t: (l, 0, 0)),
                  pl.BlockSpec((None, 1, qk), lambda b, t: (l, 0, 0)),
                  pl.BlockSpec((None, 1, GLA_DV), lambda b, t: (l, 0, 0)),
                  pl.BlockSpec((None, 1) + hd, lambda b, t: (l_s0, b, 0, 0, 0)),
                  pl.BlockSpec(memory_space=pl.ANY)],
        out_specs=[pl.BlockSpec((chunk, vw), lambda b, t: (rows(b, t), 0)),
                   pl.BlockSpec((1,) + hd, lambda b, t: (b, 0, 0, 0))],
        out_shape=[jax.ShapeDtypeStruct(prev.shape, bf16),
                   jax.ShapeDtypeStruct((nseq,) + hd, f32)],
        scratch_shapes=[pltpu.VMEM(hd, f32), pltpu.VMEM((chunk, qk), f32),
                        pltpu.VMEM((GLA_HEADS, chunk, chunk), f32)],
        input_output_aliases={9: 0},
        compiler_params=_params("arbitrary", "arbitrary"),
        name="gla")(proj, proj, proj, proj, small, w2, bg, gn, s0, prev)


def _ssd_body(z_ref, x_ref, bm_ref, cm_ref, sm_ref, wx_ref, wb_ref, wc_ref, bx_ref, bb_ref, bc_ref,
              sx_ref, sb_ref, sc_ref, dtb_ref, alog_ref, dskip_ref, nrm_ref, h0_ref, prev_ref,
              y_ref, hout_ref, cxo_ref, cbo_ref, cco_ref,
              h_ref, xpx_ref, xpb_ref, xpc_ref, xs_ref, *, chunk, nchunks):
    del prev_ref
    t = pl.program_id(1)
    hist = CONV_PAD - (SSM_CONV - 1)

    @pl.when(t == 0)
    def _():
        h_ref[...] = h0_ref[0]
        xpx_ref[hist:CONV_PAD, :] = sx_ref[0]
        xpb_ref[hist:CONV_PAD, :] = sb_ref[0]
        xpc_ref[hist:CONV_PAD, :] = sc_ref[0]

    def conv_silu(xp_ref, cur_ref, w_ref, bias_ref):
        x = cur_ref[...]
        taps = SSM_CONV - 1
        acc = bias_ref[...] + x * w_ref[taps:taps + 1, :]
        for s in range(1, SSM_CONV):
            acc = acc + pltpu.roll(x, s, axis=0) * w_ref[taps - s:taps - s + 1, :]
        xp_ref[CONV_PAD:2 * CONV_PAD, :] = x[0:CONV_PAD]
        head = bias_ref[...] + xp_ref[pl.ds(hist, CONV_PAD), :] * w_ref[0:1, :]
        for i in range(1, SSM_CONV):
            head = head + xp_ref[pl.ds(hist + i, CONV_PAD), :] * w_ref[i:i + 1, :]
        xp_ref[hist:CONV_PAD, :] = x[chunk - taps:chunk]
        acc = jnp.concatenate([head, acc[CONV_PAD:]], axis=0)
        return acc * _sig(acc)

    xs_ref[...] = conv_silu(xpx_ref, x_ref, wx_ref, bx_ref)
    bmat = conv_silu(xpb_ref, bm_ref, wb_ref, bb_ref).astype(bf16)
    cmat = conv_silu(xpc_ref, cm_ref, wc_ref, bc_ref).astype(bf16)

    @pl.when(t == nchunks - 1)
    def _():
        cxo_ref[0] = xpx_ref[hist:CONV_PAD, :]
        cbo_ref[0] = xpb_ref[hist:CONV_PAD, :]
        cco_ref[0] = xpc_ref[hist:CONV_PAD, :]

    dt = _softplus(sm_ref[...] + dtb_ref[...])
    cum = _cumsum_rows(dt * -jnp.exp(alog_ref[...]))
    last = cum[chunk - 1:chunk, :]
    w_end = jnp.exp(last - cum) * dt
    e_cum = jnp.exp(cum)
    e_last = jnp.exp(last)
    cum2 = cum * LOG2E
    adj2_t = ((cum - jnp.log(dt)) * LOG2E).T

    row = lax.broadcasted_iota(jnp.int32, (chunk, chunk), 0)
    col = lax.broadcasted_iota(jnp.int32, (chunk, chunk), 1)
    causal = row >= col
    head = lax.broadcasted_iota(jnp.int32, (LANES, SSM_GROUP_W), 0)
    chan = lax.broadcasted_iota(jnp.int32, (LANES, SSM_GROUP_W), 1)
    nt = (((1,), (1,)), ((), ()))
    tn = (((0,), (0,)), ((), ()))
    for g in range(SSM_GROUPS):
        cs = slice(g * SSM_GROUP_W, (g + 1) * SSM_GROUP_W)
        ns = slice(g * SSM_D_STATE, (g + 1) * SSM_D_STATE)
        xs = xs_ref[:, cs]
        xsb = xs.astype(bf16)
        b_g = bmat[:, ns]
        c_g = cmat[:, ns]
        widen = (head == g * SSM_HPG + chan // SSM_HEAD_DIM).astype(bf16)
        w_full = _dot_exact01(w_end, widen, 2)
        ecum_full = _dot_exact01(e_cum, widen, 2)

        cb = lax.dot_general(c_g, b_g, nt, preferred_element_type=f32)
        state = h_ref[g]
        y = lax.dot_general(c_g, state.astype(bf16), nt, preferred_element_type=f32) * ecum_full
        parts = []
        for j in range(SSM_HPG):
            hd = g * SSM_HPG + j
            seg = jnp.exp2(jnp.where(causal, cum2[:, hd:hd + 1] - adj2_t[hd:hd + 1, :], -jnp.inf))
            parts.append(jnp.dot((cb * seg).astype(bf16), xsb[:, j * SSM_HEAD_DIM:(j + 1) * SSM_HEAD_DIM],
                                 preferred_element_type=f32))
        y = y + jnp.concatenate(parts, axis=1)

        upd = lax.dot_general((xs * w_full).astype(bf16), b_g, tn, preferred_element_type=f32)
        for j in range(SSM_HPG):
            hd = g * SSM_HPG + j
            rs = slice(j * SSM_HEAD_DIM, (j + 1) * SSM_HEAD_DIM)
            h_ref[g, rs, :] = state[rs, :] * e_last[:, hd:hd + 1] + upd[rs, :]

        y = y + dskip_ref[:, cs] * xs
        z = z_ref[:, cs]
        y = y * (z * _sig(z))
        y = y * lax.rsqrt(jnp.mean(y * y, axis=1, keepdims=True) + EPS) * nrm_ref[:, cs]
        y_ref[:, cs] = y.astype(y_ref.dtype)

    @pl.when(t == nchunks - 1)
    def _():
        hout_ref[0] = h_ref[...]


def _ssd(proj, small, conv_w, conv_b, l, conv_s, l_cs, dtb, alog, dskip, nrm, h0, l_h0, prev,
         *, row0, nseq, seqlen, chunk):
    nchunks = seqlen // chunk
    blk0 = row0 // chunk
    di, bc = SSM_D_INNER, SSM_BC
    rows = lambda b, t: blk0 + b * nchunks + t
    taps, hist = SSM_CONV, SSM_CONV - 1
    xb, bb, cb = 0, di // bc, di // bc + 1
    hs = (SSM_GROUPS, SSM_GROUP_W, SSM_D_STATE)
    fix3 = lambda b, t: (l, 0, 0)
    in_specs = [
        pl.BlockSpec((chunk, di), lambda b, t: (rows(b, t), P_Z // di)),
        pl.BlockSpec((chunk, di), lambda b, t: (rows(b, t), P_X // di)),
        pl.BlockSpec((chunk, bc), lambda b, t: (rows(b, t), P_B // bc)),
        pl.BlockSpec((chunk, bc), lambda b, t: (rows(b, t), P_C // bc)),
        pl.BlockSpec((chunk, LANES), lambda b, t: (rows(b, t), 0)),
        pl.BlockSpec((None, taps, di), lambda b, t: (l, 0, xb)),
        pl.BlockSpec((None, taps, bc), lambda b, t: (l, 0, bb)),
        pl.BlockSpec((None, taps, bc), lambda b, t: (l, 0, cb)),
        pl.BlockSpec((None, 1, di), lambda b, t: (l, 0, xb)),
        pl.BlockSpec((None, 1, bc), lambda b, t: (l, 0, bb)),
        pl.BlockSpec((None, 1, bc), lambda b, t: (l, 0, cb)),
        pl.BlockSpec((None, 1, hist, di), lambda b, t: (l_cs, b, 0, xb)),
        pl.BlockSpec((None, 1, hist, bc), lambda b, t: (l_cs, b, 0, bb)),
        pl.BlockSpec((None, 1, hist, bc), lambda b, t: (l_cs, b, 0, cb)),
        pl.BlockSpec((None, 1, LANES), fix3),
        pl.BlockSpec((None, 1, LANES), fix3),
        pl.BlockSpec((None, 1, di), fix3),
        pl.BlockSpec((None, 1, di), fix3),
        pl.BlockSpec((None, 1) + hs, lambda b, t: (l_h0, b, 0, 0, 0)),
        pl.BlockSpec(memory_space=pl.ANY),
    ]
    return pl.pallas_call(
        functools.partial(_ssd_body, chunk=chunk, nchunks=nchunks),
        grid=(nseq, nchunks),
        in_specs=in_specs,
        out_specs=[pl.BlockSpec((chunk, di), lambda b, t: (rows(b, t), 0)),
                   pl.BlockSpec((1,) + hs, lambda b, t: (b, 0, 0, 0)),
                   pl.BlockSpec((1, hist, di), lambda b, t: (b, 0, 0)),
                   pl.BlockSpec((1, hist, bc), lambda b, t: (b, 0, 0)),
                   pl.BlockSpec((1, hist, bc), lambda b, t: (b, 0, 0))],
        out_shape=[jax.ShapeDtypeStruct(prev.shape, bf16),
                   jax.ShapeDtypeStruct((nseq,) + hs, f32),
                   jax.ShapeDtypeStruct((nseq, hist, di), f32),
                   jax.ShapeDtypeStruct((nseq, hist, bc), f32),
                   jax.ShapeDtypeStruct((nseq, hist, bc), f32)],
        scratch_shapes=[pltpu.VMEM(hs, f32), pltpu.VMEM((2 * CONV_PAD, di), f32),
                        pltpu.VMEM((2 * CONV_PAD, bc), f32), pltpu.VMEM((2 * CONV_PAD, bc), f32),
                        pltpu.VMEM((chunk, di), f32)],
        input_output_aliases={19: 0},
        compiler_params=_params("arbitrary", "arbitrary"),
        name="ssd")(proj, proj, proj, proj, small, conv_w, conv_w, conv_w, conv_b, conv_b, conv_b,
                    conv_s, conv_s, conv_s, dtb, alog, dskip, nrm, h0, prev)


def _pad_lanes(v):
    return jnp.pad(v, ((0, 0), (0, LANES - v.shape[-1])))[:, None, :]


def kernel(x_prompt, x_sample, state_gla, state_ssm, state_conv, norm_mix, w_in, gla_w_gate2, gla_b_gate,
           gla_norm, gla_w_out, ssm_conv_w, ssm_conv_b, ssm_dt_bias, ssm_a_log, ssm_d, ssm_norm, ssm_w_out,
           w_o, norm_ffn, router_w1, router_b1, router_w2, router_b2, exp_w_gate, exp_w_up, exp_w_down,
           norm_final):
    n_p, t_p = x_prompt.shape[:2]
    n_s, t_s = x_sample.shape[:2]
    assert n_p == 1
    rows_p, rows_s = n_p * t_p, n_s * t_s
    m = rows_p + rows_s
    tm = 832 if m % 832 == 0 else 128
    tm_out = 416 if m % 416 == 0 else 128
    gla_chunk_p, ssd_chunk_p = min(64, t_p), min(256, t_p)

    x = jnp.concatenate([x_prompt.reshape(rows_p, D_MODEL), x_sample.reshape(rows_s, D_MODEL)], axis=0)
    zeros_gla = jnp.zeros((1, n_p, GLA_HEADS, GLA_DK, GLA_DV), f32)
    zeros_ssm = jnp.zeros((1, n_p, SSM_GROUPS, SSM_GROUP_W, SSM_D_STATE), f32)
    zeros_conv = jnp.zeros((1, n_p, SSM_CONV - 1, SSM_CONV_DIM), f32)
    state_ssm_g = state_ssm.reshape(DEPTH, n_s, SSM_GROUPS, SSM_GROUP_W, SSM_D_STATE)

    bg3 = gla_b_gate[:, None, :]
    gn3 = gla_norm[:, None, :]
    conv_b3 = ssm_conv_b[:, None, :]
    dtb3 = _pad_lanes(ssm_dt_bias)
    alog3 = _pad_lanes(ssm_a_log)
    dskip3 = jnp.repeat(ssm_d, SSM_HEAD_DIM, axis=1)[:, None, :]
    nrm3 = ssm_norm[:, None, :]

    w_in_t = jnp.swapaxes(w_in, 1, 2)
    oa = jnp.zeros((m, GLA_VW), bf16)
    ob = jnp.zeros((m, SSM_D_INNER), bf16)

    outs = {k: [] for k in ("gla_p", "ssm_p", "conv_p", "gla_s", "ssm_s", "conv_s")}
    for l in range(DEPTH):
        w_small_t = jnp.concatenate(
            [w_in_t[l, W_DT:W_MG], w_in_t[l, W_GLR:W_Z],
             jnp.zeros((LANES - SSM_HEADS - GLA_GATE_RANK, D_MODEL), f32)], axis=0)

        u = _rmsnorm(x, norm_mix[l], rows=m, row0=0, tm=tm, out_dtype=bf16)
        proj = _inproj(u, w_in_t, l, tm=tm)
        small = _matmul_nt(u, w_small_t, tm=tm, name="inproj_small")

        gla_args = (proj, small, gla_w_gate2, bg3, gn3, l)
        oa, g_p = _gla(*gla_args, zeros_gla, 0, oa, row0=0, nseq=n_p, seqlen=t_p, chunk=gla_chunk_p)
        oa, g_s = _gla(*gla_args, state_gla, l, oa, row0=rows_p, nseq=n_s, seqlen=t_s, chunk=t_s)

        ssd_args = (proj, small, ssm_conv_w, conv_b3, l)
        ssd_tail = (dtb3, alog3, dskip3, nrm3)
        ob, h_p, *conv_p = _ssd(*ssd_args, zeros_conv, 0, *ssd_tail, zeros_ssm, 0, ob,
                                row0=0, nseq=n_p, seqlen=t_p, chunk=ssd_chunk_p)
        ob, h_s, *conv_s = _ssd(*ssd_args, state_conv, l, *ssd_tail, state_ssm_g, l, ob,
                                row0=rows_p, nseq=n_s, seqlen=t_s, chunk=t_s)

        merged = _merge(oa, ob, proj, gla_w_out, ssm_w_out, l, tm=tm_out, tn=512)

        wr = jnp.concatenate([router_w1[l], router_w2[l],
                              jnp.zeros((D_MODEL, LANES - MOE_GROUPS - MOE_EXPERTS), f32)], axis=1)
        br = jnp.pad(jnp.concatenate([router_b1[l], router_b2[l]]),
                     (0, LANES - MOE_GROUPS - MOE_EXPERTS)).reshape(1, LANES)
        xn, u2, cw, oh, cnt = _outproj(merged, w_o, l, x, norm_ffn[l].reshape(1, -1), wr, br, tm=tm_out)
        x = _moe(u2, cw, oh, cnt, xn, exp_w_gate, exp_w_up, exp_w_down, l)

        outs["gla_p"].append(g_p)
        outs["gla_s"].append(g_s)
        outs["ssm_p"].append(h_p.reshape(n_p, SSM_HEADS, SSM_HEAD_DIM, SSM_D_STATE))
        outs["ssm_s"].append(h_s.reshape(n_s, SSM_HEADS, SSM_HEAD_DIM, SSM_D_STATE))
        outs["conv_p"].append(jnp.concatenate(conv_p, axis=-1))
        outs["conv_s"].append(jnp.concatenate(conv_s, axis=-1))

    tm_p = 512 if rows_p % 512 == 0 else 128
    tm_s = rows_s if rows_p % rows_s == 0 else 8
    y_p = _rmsnorm(x, norm_final, rows=rows_p, row0=0, tm=tm_p, out_dtype=f32)
    y_s = _rmsnorm(x, norm_final, rows=rows_s, row0=rows_p, tm=tm_s, out_dtype=f32)
    return (y_p.reshape(n_p, t_p, D_MODEL), y_s.reshape(n_s, t_s, D_MODEL),
            jnp.stack(outs["gla_p"]), jnp.stack(outs["ssm_p"]), jnp.stack(outs["conv_p"]),
            jnp.stack(outs["gla_s"]), jnp.stack(outs["ssm_s"]), jnp.stack(outs["conv_s"]))
```

```python
import functools
import math

import jax
import jax.numpy as jnp
from jax import lax
from jax.experimental import pallas as pl
from jax.experimental.pallas import tpu as pltpu

f32 = jnp.float32
bf16 = jnp.bfloat16
HIGHEST = lax.Precision.HIGHEST
LOG2E = math.log2(math.e)

D_MODEL = 2048
DEPTH = 2
EPS = 1e-6
GLA_HEADS = 4
GLA_DK = 256
GLA_DV = 512
GLA_QK = GLA_HEADS * GLA_DK
GLA_VW = GLA_HEADS * GLA_DV
GLA_GATE_RANK = 16
GLA_GATE_TAU = 16.0
GLA_SUB = 16
SSM_D_INNER = 4096
SSM_HEAD_DIM = 64
SSM_HEADS = 64
SSM_GROUPS = 8
SSM_HPG = 8
SSM_GROUP_W = SSM_HPG * SSM_HEAD_DIM
SSM_D_STATE = 128
SSM_BC = SSM_GROUPS * SSM_D_STATE
SSM_CONV = 4
SSM_CONV_DIM = 6144
MOE_GROUPS = 4
MOE_PER_GROUP = 4
MOE_EXPERTS = 16
MOE_D_FF = 512
MOE_TOP_K = 2
MOE_LANE0 = MOE_GROUPS
MOE_TILE = 256
LANES = 128
CONV_PAD = 8

W_GLR = 6144
W_Z = 6160
W_DT = 16400
W_MG = 16464
IN_TN = 512
IN_TILES = 20480 // IN_TN
IN_ALIGNED_TILES = W_GLR // IN_TN
IN_MG_TILE0 = (W_DT - W_Z + W_GLR) // IN_TN
IN_SHIFT_Z = W_Z - W_GLR
IN_SHIFT_MG = W_MG - IN_MG_TILE0 * IN_TN
P_Z, P_X, P_Q, P_K, P_V, P_R, P_B, P_C, P_GA, P_GB, P_END = (
    0, 4096, 8192, 9216, 10240, 12288, 14336, 15360, 16384, 18432, 20480)
S_GLR = 64

VMEM_LIMIT = 56 * 1024 * 1024


def _sig(x):
    return 1.0 / (1.0 + jnp.exp(-x))


def _softplus(x):
    return jnp.maximum(x, 0.0) + jnp.log1p(jnp.exp(-jnp.abs(x)))


def _log_sigmoid(x):
    return jnp.minimum(x, 0.0) - jnp.log1p(jnp.exp(-jnp.abs(x)))


def _params(*sem):
    return pltpu.CompilerParams(dimension_semantics=sem, vmem_limit_bytes=VMEM_LIMIT)


def _split_bf16(x, pieces):
    out = []
    for _ in range(pieces - 1):
        p = x.astype(bf16)
        out.append(p)
        x = x - p.astype(f32)
    out.append(x.astype(bf16))
    return out


def _dot_exact01(a, b01, pieces, dims=None):
    acc = None
    for p in _split_bf16(a, pieces):
        if dims is None:
            d = jnp.dot(p, b01, preferred_element_type=f32)
        else:
            d = lax.dot_general(p, b01, dims, preferred_element_type=f32)
        acc = d if acc is None else acc + d
    return acc


def _cumsum_rows(x):
    n = x.shape[0]
    row = lax.broadcasted_iota(jnp.int32, (n, n), 0)
    col = lax.broadcasted_iota(jnp.int32, (n, n), 1)
    tri = (row >= col).astype(bf16)
    acc = None
    for p in _split_bf16(x, 3):
        d = jnp.dot(tri, p, preferred_element_type=f32)
        acc = d if acc is None else acc + d
    return acc


def _norm_body(x_ref, g_ref, o_ref):
    x = x_ref[...]
    y = x * lax.rsqrt(jnp.mean(x * x, axis=-1, keepdims=True) + EPS)
    o_ref[...] = (y * g_ref[...]).astype(o_ref.dtype)


def _rmsnorm(x, g, *, rows, row0, tm, out_dtype):
    blk0 = row0 // tm
    return pl.pallas_call(
        _norm_body, grid=(rows // tm,),
        in_specs=[pl.BlockSpec((tm, D_MODEL), lambda m: (blk0 + m, 0)),
                  pl.BlockSpec((1, D_MODEL), lambda m: (0, 0))],
        out_specs=pl.BlockSpec((tm, D_MODEL), lambda m: (m, 0)),
        out_shape=jax.ShapeDtypeStruct((rows, D_MODEL), out_dtype),
        compiler_params=_params("arbitrary"), name="rmsnorm")(x, g.reshape(1, D_MODEL))


def _inproj_body(u_ref, wa_ref, wb_ref, o_ref, wbf_ref):
    n = pl.program_id(0)

    @pl.when(pl.program_id(1) == 0)
    def _():
        def fill(shift):
            def f():
                if shift == 0:
                    wbf_ref[...] = wa_ref[...].astype(bf16)
                else:
                    wbf_ref[:IN_TN - shift, :] = wa_ref[shift:, :].astype(bf16)
                    wbf_ref[IN_TN - shift:, :] = wb_ref[:shift, :].astype(bf16)
            return f
        pl.when(n < IN_ALIGNED_TILES)(fill(0))
        pl.when((n >= IN_ALIGNED_TILES) & (n < IN_MG_TILE0))(fill(IN_SHIFT_Z))
        pl.when(n >= IN_MG_TILE0)(fill(IN_SHIFT_MG))

    o_ref[...] = lax.dot_general(u_ref[...], wbf_ref[...], (((1,), (1,)), ((), ())),
                                 preferred_element_type=f32)


def _inproj_out_tile(n):
    lead = IN_ALIGNED_TILES
    zx = (P_Q - P_Z) // IN_TN
    return jnp.where(n < lead, n + zx, jnp.where(n < lead + zx, n - lead, n))


def _inproj(u, w_in_t, l, *, tm):
    m = u.shape[0]
    return pl.pallas_call(
        _inproj_body, grid=(IN_TILES, m // tm),
        in_specs=[pl.BlockSpec((tm, D_MODEL), lambda n, i: (i, 0)),
                  pl.BlockSpec((None, IN_TN, D_MODEL), lambda n, i: (l, n, 0)),
                  pl.BlockSpec((None, LANES, D_MODEL), lambda n, i: (l, (n + 1) * (IN_TN // LANES), 0))],
        out_specs=pl.BlockSpec((tm, IN_TN), lambda n, i: (i, _inproj_out_tile(n))),
        out_shape=jax.ShapeDtypeStruct((m, P_END), f32),
        scratch_shapes=[pltpu.VMEM((IN_TN, D_MODEL), bf16)],
        compiler_params=_params("arbitrary", "arbitrary"), name="inproj")(u, w_in_t, w_in_t)


def _mm_nt_body(a_ref, w_ref, o_ref):
    o_ref[...] = lax.dot_general(a_ref[...], w_ref[...].astype(bf16), (((1,), (1,)), ((), ())),
                                 preferred_element_type=f32)


def _matmul_nt(a, w_t, *, tm, name):
    m, k = a.shape
    n = w_t.shape[0]
    return pl.pallas_call(
        _mm_nt_body, grid=(m // tm,),
        in_specs=[pl.BlockSpec((tm, k), lambda i: (i, 0)),
                  pl.BlockSpec((n, k), lambda i: (0, 0))],
        out_specs=pl.BlockSpec((tm, n), lambda i: (i, 0)),
        out_shape=jax.ShapeDtypeStruct((m, n), f32),
        compiler_params=_params("arbitrary"), name=name)(a, w_t)


def _merge_body(oa_ref, ob_ref, ga_ref, gb_ref, wa_ref, wb_ref, o_ref, wabf_ref, wbbf_ref):
    @pl.when(pl.program_id(1) == 0)
    def _():
        wabf_ref[...] = wa_ref[...].astype(bf16)
        wbbf_ref[...] = wb_ref[...].astype(bf16)

    ya = jnp.dot(oa_ref[...], wabf_ref[...], preferred_element_type=f32)
    yb = jnp.dot(ob_ref[...], wbbf_ref[...], preferred_element_type=f32)
    o_ref[...] = (_sig(ga_ref[...]) * ya + _sig(gb_ref[...]) * yb).astype(o_ref.dtype)


def _merge(oa, ob, proj, wa, wb, l, *, tm, tn):
    m = oa.shape[0]
    ga0, gb0 = P_GA // tn, P_GB // tn
    return pl.pallas_call(
        _merge_body, grid=(D_MODEL // tn, m // tm),
        in_specs=[pl.BlockSpec((tm, GLA_VW), lambda j, i: (i, 0)),
                  pl.BlockSpec((tm, SSM_D_INNER), lambda j, i: (i, 0)),
                  pl.BlockSpec((tm, tn), lambda j, i: (i, ga0 + j)),
                  pl.BlockSpec((tm, tn), lambda j, i: (i, gb0 + j)),
                  pl.BlockSpec((None, GLA_VW, tn), lambda j, i: (l, 0, j)),
                  pl.BlockSpec((None, SSM_D_INNER, tn), lambda j, i: (l, 0, j))],
        out_specs=pl.BlockSpec((tm, tn), lambda j, i: (i, j)),
        out_shape=jax.ShapeDtypeStruct((m, D_MODEL), bf16),
        scratch_shapes=[pltpu.VMEM((GLA_VW, tn), bf16), pltpu.VMEM((SSM_D_INNER, tn), bf16)],
        compiler_params=_params("arbitrary", "arbitrary"), name="merge")(oa, ob, proj, proj, wa, wb)


def _route(logits):
    c = lax.broadcasted_iota(jnp.int32, logits.shape, 1)
    big = jnp.int32(1 << 20)
    is_group = c < MOE_GROUPS
    lg = jnp.where(is_group, logits, -jnp.inf)
    top = jnp.max(lg, axis=1, keepdims=True)
    gsel = jnp.min(jnp.where(is_group & (logits == top), c, big), axis=1, keepdims=True)
    p_sel = 1.0 / jnp.sum(jnp.exp(lg - top), axis=1, keepdims=True)
    in_group = ((c >= MOE_LANE0) & (c < MOE_LANE0 + MOE_EXPERTS)
                & (((c - MOE_LANE0) // MOE_PER_GROUP) == gsel))
    le = jnp.where(in_group, logits, -jnp.inf)
    tv0 = jnp.max(le, axis=1, keepdims=True)
    i0 = jnp.min(jnp.where(in_group & (logits == tv0), c, big), axis=1, keepdims=True)
    rest = in_group & (c != i0)
    le1 = jnp.where(rest, logits, -jnp.inf)
    tv1 = jnp.max(le1, axis=1, keepdims=True)
    i1 = jnp.min(jnp.where(rest & (logits == tv1), c, big), axis=1, keepdims=True)
    t = jnp.exp(tv1 - tv0)
    w0 = p_sel / (1.0 + t)
    w1 = p_sel * t / (1.0 + t)
    cw = jnp.where(c == i0, w0, jnp.where(c == i1, w1, 0.0))
    return cw, ((c == i0) | (c == i1)).astype(f32)


def _outproj_body(m_ref, wo_ref, x_ref, g_ref, wr_ref, br_ref, xn_ref, u_ref, cw_ref, oh_ref, cnt_ref,
                  wobf_ref):
    @pl.when(pl.program_id(0) == 0)
    def _():
        wobf_ref[...] = wo_ref[...].astype(bf16)
        cnt_ref[...] = jnp.zeros_like(cnt_ref)

    w_hi, w_lo = _split_bf16(wr_ref[...], 2)
    tm = x_ref.shape[0]
    parts = 2 if tm % 32 == 0 else 1
    for p in range(parts):
        rs = slice(p * (tm // parts), (p + 1) * (tm // parts))
        xn = x_ref[rs, :] + jnp.dot(m_ref[rs, :], wobf_ref[...], preferred_element_type=f32)
        xn_ref[rs, :] = xn
        u = xn * lax.rsqrt(jnp.mean(xn * xn, axis=-1, keepdims=True) + EPS) * g_ref[...]
        u_ref[rs, :] = u
        u_hi, u_lo = _split_bf16(u, 2)
        logits = (jnp.dot(u_hi, w_hi, preferred_element_type=f32)
                  + (jnp.dot(u_hi, w_lo, preferred_element_type=f32)
                     + jnp.dot(u_lo, w_hi, preferred_element_type=f32))) + br_ref[...]
        cw, oh = _route(logits)
        cw_ref[rs, :] = cw
        oh_ref[rs, :] = oh
        cnt_ref[...] += jnp.sum(oh, axis=0, keepdims=True)


def _outproj(merged, wo, l, x, g, wr, br, *, tm):
    m = x.shape[0]
    row = lambda i: (i, 0)
    fix = lambda i: (0, 0)
    return pl.pallas_call(
        _outproj_body, grid=(m // tm,),
        in_specs=[pl.BlockSpec((tm, D_MODEL), row),
                  pl.BlockSpec((None, D_MODEL, D_MODEL), lambda i: (l, 0, 0), pipeline_mode=pl.Buffered(1)),
                  pl.BlockSpec((tm, D_MODEL), row), pl.BlockSpec((1, D_MODEL), fix),
                  pl.BlockSpec((D_MODEL, LANES), fix), pl.BlockSpec((1, LANES), fix)],
        out_specs=[pl.BlockSpec((tm, D_MODEL), row), pl.BlockSpec((tm, D_MODEL), row),
                   pl.BlockSpec((tm, LANES), row), pl.BlockSpec((tm, LANES), row), pl.BlockSpec((1, LANES), fix)],
        out_shape=[jax.ShapeDtypeStruct((m, D_MODEL), f32), jax.ShapeDtypeStruct((m, D_MODEL), f32),
                   jax.ShapeDtypeStruct((m, LANES), f32), jax.ShapeDtypeStruct((m, LANES), f32),
                   jax.ShapeDtypeStruct((1, LANES), f32)],
        scratch_shapes=[pltpu.VMEM((D_MODEL, D_MODEL), bf16)],
        compiler_params=_params("arbitrary"), name="outproj")(merged, wo, x, g, wr, br)


def _plan_body(oh_ref, cw_ref, cnt_ref, dest_ref, wts_ref, te_ref, pad_ref, base_ref, carry_ref, *, rows):
    lane = lax.broadcasted_iota(jnp.int32, (8, LANES), 1)
    is_expert = (lane >= MOE_LANE0) & (lane < MOE_LANE0 + MOE_EXPERTS)

    @pl.when(pl.program_id(0) == 0)
    def _():
        cnt = jnp.broadcast_to(cnt_ref[...], (8, LANES))
        padded = jnp.floor((cnt + (MOE_TILE - 1)) * (1.0 / MOE_TILE)) * MOE_TILE
        src = lax.broadcasted_iota(jnp.int32, (LANES, LANES), 0)
        dst = lax.broadcasted_iota(jnp.int32, (LANES, LANES), 1)
        base = _dot_exact01(padded, (src < dst).astype(bf16), 3)
        base_ref[...] = base[0:1]
        carry_ref[...] = jnp.zeros_like(carry_ref)
        end = base + padded
        total = jnp.max(jnp.where(is_expert, end, 0.0), axis=1, keepdims=True)
        end_col = jnp.transpose(jnp.where(is_expert, end, jnp.inf))[:, 0:1]
        tile_start = (lax.broadcasted_iota(jnp.int32, (LANES, LANES), 1) * MOE_TILE).astype(f32)
        tile_exp = jnp.sum((end_col <= tile_start).astype(f32), axis=0, keepdims=True)
        tile_exp = jnp.minimum(tile_exp, MOE_EXPERTS - 1.0)
        te_ref[0:1, :] = tile_exp.astype(jnp.int32)
        te_ref[1:2, :] = jnp.broadcast_to(total[0:1] * (1.0 / MOE_TILE), (1, LANES)).astype(jnp.int32)
        shift = (src == dst + MOE_LANE0).astype(bf16)
        is_tail = lane[0:1] == MOE_EXPERTS
        pad_lo = jnp.where(is_tail, total[0:1], _dot_exact01(base + cnt, shift, 3)[0:1])
        pad_hi = jnp.where(is_tail, float(rows), _dot_exact01(end, shift, 3)[0:1])
        pad_ref[0:1, :] = pad_lo.astype(jnp.int32)
        pad_ref[1:2, :] = pad_hi.astype(jnp.int32)

    oh = oh_ref[...]
    n = oh.shape[0]
    row = lax.broadcasted_iota(jnp.int32, (n, n), 0)
    col = lax.broadcasted_iota(jnp.int32, (n, n), 1)
    rank = jnp.dot((row > col).astype(bf16), oh.astype(bf16), preferred_element_type=f32)
    pos = base_ref[...] + carry_ref[...] + rank
    carry_ref[...] += jnp.sum(oh, axis=0, keepdims=True)
    chosen = oh > 0.0
    d_lo = jnp.min(jnp.where(chosen, pos, jnp.inf), axis=1, keepdims=True)
    d_hi = jnp.max(jnp.where(chosen, pos, -1.0), axis=1, keepdims=True)
    cw = cw_ref[...]
    w_lo = jnp.sum(jnp.where(chosen & (pos == d_lo), cw, 0.0), axis=1, keepdims=True)
    w_hi = jnp.sum(jnp.where(chosen & (pos == d_hi), cw, 0.0), axis=1, keepdims=True)
    lanes = lax.broadcasted_iota(jnp.int32, oh.shape, 1)
    wts_ref[...] = jnp.where(lanes == 0, w_lo, jnp.where(lanes == 1, w_hi, 0.0))
    dests = jnp.where(lanes == 0, d_lo, jnp.where(lanes == 1, d_hi, 0.0))
    dest_ref[0] = jnp.transpose(dests)[0:2, :].astype(jnp.int32)


def _moe_plan(oh, cw, cnt, *, blk, rows):
    m = oh.shape[0]
    nblk = m // blk
    fix = lambda i: (0, 0)
    return pl.pallas_call(
        functools.partial(_plan_body, rows=rows), grid=(nblk,),
        in_specs=[pl.BlockSpec((blk, LANES), lambda i: (i, 0)), pl.BlockSpec((blk, LANES), lambda i: (i, 0)),
                  pl.BlockSpec((1, LANES), fix)],
        out_specs=[pl.BlockSpec((1, 2, blk), lambda i: (i, 0, 0)), pl.BlockSpec((blk, LANES), lambda i: (i, 0)),
                   pl.BlockSpec((2, LANES), fix), pl.BlockSpec((2, LANES), fix)],
        out_shape=[jax.ShapeDtypeStruct((nblk, 2, blk), jnp.int32), jax.ShapeDtypeStruct((m, LANES), f32),
                   jax.ShapeDtypeStruct((2, LANES), jnp.int32), jax.ShapeDtypeStruct((2, LANES), jnp.int32)],
        scratch_shapes=[pltpu.VMEM((1, LANES), f32), pltpu.VMEM((1, LANES), f32)],
        compiler_params=_params("arbitrary"), name="moe_plan")(oh, cw, cnt)


def _row_copy(src_ref, s, dst_ref, d, sem):
    return pltpu.make_async_copy(src_ref.at[pl.ds(s, 1), :], dst_ref.at[pl.ds(d, 1), :], sem)


def _dispatch_body(dest_ref, pad_ref, u_ref, xs_ref, zero_ref, sem, *, blk, nsub):
    i = pl.program_id(0)

    def wait_rows(count, which):
        pltpu.make_async_copy(xs_ref.at[pl.ds(0, count), :], xs_ref.at[pl.ds(0, count), :], sem.at[which]).wait()

    @pl.when(i == 0)
    def _():
        zero_ref[...] = jnp.zeros_like(zero_ref)
        group = zero_ref.shape[0]

        def fill_row(r, c):
            _row_copy(zero_ref, 0, xs_ref, r, sem.at[1]).start()
            return c

        def fill_group(lo8):
            def f(r, c):
                row0 = pl.multiple_of(lo8 + r * group, group)
                pltpu.make_async_copy(zero_ref, xs_ref.at[pl.ds(row0, group), :], sem.at[1]).start()
                return c
            return f

        nrow = jnp.int32(0)
        ngroup = jnp.int32(0)
        for e in range(MOE_EXPERTS + 1):
            lo = pad_ref[0, e]
            hi = pad_ref[1, e]
            lo8 = jnp.minimum((lo + group - 1) // group * group, hi)
            lax.fori_loop(lo, lo8, fill_row, 0)
            lax.fori_loop(0, (hi - lo8) // group, fill_group(lo8), 0)
            nrow = nrow + (lo8 - lo)
            ngroup = ngroup + (hi - lo8) // group

        def wait_row(r, c):
            _row_copy(zero_ref, 0, xs_ref, 0, sem.at[1]).wait()
            return c
        lax.fori_loop(0, nrow, wait_row, 0)

        def wait_group(r, c):
            wait_rows(group, 1)
            return c
        lax.fori_loop(0, ngroup, wait_group, 0)

    for s in range(nsub):
        def send(j, c, s=s):
            _row_copy(u_ref, s * blk + j, xs_ref, dest_ref[s, 0, j], sem.at[0]).start(priority=0)
            _row_copy(u_ref, s * blk + j, xs_ref, dest_ref[s, 1, j], sem.at[0]).start(priority=1)
            return c
        lax.fori_loop(0, blk, send, 0)
    for _ in range(MOE_TOP_K * nsub):
        wait_rows(blk, 0)


def _moe_dispatch(u, dest, pad, *, rows, blk):
    m = u.shape[0]
    nblk = m // blk
    nsub = max(d for d in range(1, 6) if nblk % d == 0)
    return pl.pallas_call(
        functools.partial(_dispatch_body, blk=blk, nsub=nsub), grid=(nblk // nsub,),
        in_specs=[pl.BlockSpec((nsub, 2, blk), lambda i: (i, 0, 0), memory_space=pltpu.SMEM),
                  pl.BlockSpec(memory_space=pltpu.SMEM),
                  pl.BlockSpec((nsub * blk, D_MODEL), lambda i: (i, 0))],
        out_specs=pl.BlockSpec(memory_space=pl.ANY),
        out_shape=jax.ShapeDtypeStruct((rows, D_MODEL), f32),
        scratch_shapes=[pltpu.VMEM((8, D_MODEL), f32), pltpu.SemaphoreType.DMA((2,))],
        compiler_params=_params("arbitrary"), name="moe_dispatch")(dest, pad, u)


def _gmm_body(te_ref, xs_ref, wg_ref, wu_ref, wd_ref, ys_ref, wgb_ref, wub_ref, wdb_ref):
    i = pl.program_id(0)
    n_used = te_ref[1, 0]

    @pl.when(i < n_used)
    def _():
        @pl.when((i == 0) | (te_ref[0, i] != te_ref[0, jnp.maximum(i - 1, 0)]))
        def _():
            wgb_ref[...] = wg_ref[...].astype(bf16)
            wub_ref[...] = wu_ref[...].astype(bf16)
            wdb_ref[...] = wd_ref[...].astype(bf16)

        x = xs_ref[...].astype(bf16)
        hg = jnp.dot(x, wgb_ref[...], preferred_element_type=f32)
        hu = jnp.dot(x, wub_ref[...], preferred_element_type=f32)
        h = ((hg * _sig(hg)) * hu).astype(bf16)
        ys_ref[...] = jnp.dot(h, wdb_ref[...], preferred_element_type=f32)

    @pl.when(i >= n_used)
    def _():
        ys_ref[...] = jnp.zeros_like(ys_ref)


def _moe_gmm(te, xs, wg, wu, wd, l):
    rows = xs.shape[0]
    ntiles = rows // MOE_TILE
    tile = lambda i, te: (jnp.minimum(i, te[1, 0] - 1), 0)
    up = lambda i, te: (l, te[0, i], 0, 0)
    return pl.pallas_call(
        _gmm_body,
        grid_spec=pltpu.PrefetchScalarGridSpec(
            num_scalar_prefetch=1, grid=(ntiles,),
            in_specs=[pl.BlockSpec((MOE_TILE, D_MODEL), tile),
                      pl.BlockSpec((None, None, D_MODEL, MOE_D_FF), up),
                      pl.BlockSpec((None, None, D_MODEL, MOE_D_FF), up),
                      pl.BlockSpec((None, None, MOE_D_FF, D_MODEL), up)],
            out_specs=pl.BlockSpec((MOE_TILE, D_MODEL), lambda i, te: (i, 0)),
            scratch_shapes=[pltpu.VMEM((D_MODEL, MOE_D_FF), bf16), pltpu.VMEM((D_MODEL, MOE_D_FF), bf16),
                            pltpu.VMEM((MOE_D_FF, D_MODEL), bf16)]),
        out_shape=jax.ShapeDtypeStruct((rows, D_MODEL), f32),
        compiler_params=_params("arbitrary"), name="moe_gmm")(te, xs, wg, wu, wd)


def _combine_body(dcur_ref, dnext_ref, xn_ref, wts_ref, ys_ref, o_ref, ya_ref, yb_ref, sem, *, blk, nblk):
    i = pl.program_id(0)
    slot = i % 2

    def gather(dref, s):
        def one(j, c):
            _row_copy(ys_ref, dref[0, 0, j], ya_ref.at[s], j, sem.at[s]).start(priority=0)
            _row_copy(ys_ref, dref[0, 1, j], yb_ref.at[s], j, sem.at[s]).start(priority=1)
            return c
        lax.fori_loop(0, blk, one, 0)

    @pl.when(i == 0)
    def _():
        gather(dcur_ref, 0)

    @pl.when(i + 1 < nblk)
    def _():
        gather(dnext_ref, 1 - slot)

    pltpu.make_async_copy(ys_ref.at[pl.ds(0, blk), :], ya_ref.at[slot], sem.at[slot]).wait()
    pltpu.make_async_copy(ys_ref.at[pl.ds(0, blk), :], yb_ref.at[slot], sem.at[slot]).wait()
    w = wts_ref[...]
    o_ref[...] = xn_ref[...] + w[:, 0:1] * ya_ref[slot] + w[:, 1:2] * yb_ref[slot]


def _moe_combine(dest, xn, wts, ys, *, blk):
    m = xn.shape[0]
    nblk = m // blk
    return pl.pallas_call(
        functools.partial(_combine_body, blk=blk, nblk=nblk), grid=(nblk,),
        in_specs=[pl.BlockSpec((1, 2, blk), lambda i: (i, 0, 0), memory_space=pltpu.SMEM),
                  pl.BlockSpec((1, 2, blk), lambda i: (jnp.minimum(i + 1, nblk - 1), 0, 0),
                               memory_space=pltpu.SMEM),
                  pl.BlockSpec((blk, D_MODEL), lambda i: (i, 0)),
                  pl.BlockSpec((blk, LANES), lambda i: (i, 0)),
                  pl.BlockSpec(memory_space=pl.ANY)],
        out_specs=pl.BlockSpec((blk, D_MODEL), lambda i: (i, 0)),
        out_shape=jax.ShapeDtypeStruct((m, D_MODEL), f32),
        scratch_shapes=[pltpu.VMEM((2, blk, D_MODEL), f32), pltpu.VMEM((2, blk, D_MODEL), f32),
                        pltpu.SemaphoreType.DMA((2,))],
        compiler_params=_params("arbitrary"), name="moe_combine")(dest, dest, xn, wts, ys)


def _moe(u, cw, oh, cnt, xn, wg, wu, wd, l):
    m = u.shape[0]
    blk = LANES
    rows = -(-(MOE_TOP_K * m + MOE_EXPERTS * (MOE_TILE - 1)) // MOE_TILE) * MOE_TILE
    assert rows // MOE_TILE <= LANES
    dest, wts, te, pad = _moe_plan(oh, cw, cnt, blk=blk, rows=rows)
    xs = _moe_dispatch(u, dest, pad, rows=rows, blk=blk)
    ys = _moe_gmm(te, xs, wg, wu, wd, l)
    return _moe_combine(dest, xn, wts, ys, blk=blk)


def _gla_body(q_ref, k_ref, v_ref, r_ref, sm_ref, w2_ref, bg_ref, gn_ref, s0_ref, prev_ref,
              o_ref, sout_ref, st_ref, b_ref, att_ref, *, chunk, nchunks):
    del prev_ref
    t = pl.program_id(1)

    @pl.when(t == 0)
    def _():
        st_ref[...] = s0_ref[0]

    glr = sm_ref[:, S_GLR:S_GLR + GLA_GATE_RANK].astype(bf16)
    pre = jnp.dot(glr, w2_ref[...].astype(bf16), preferred_element_type=f32) + bg_ref[...]
    b_ref[...] = _cumsum_rows(_log_sigmoid(pre) * (1.0 / GLA_GATE_TAU))

    sub_row = lax.broadcasted_iota(jnp.int32, (GLA_SUB, GLA_DK), 0)
    sub_col = lax.broadcasted_iota(jnp.int32, (GLA_SUB, GLA_SUB), 1)
    nt = (((1,), (1,)), ((), ()))
    tn = (((0,), (0,)), ((), ()))
    for h in range(GLA_HEADS):
        ks = slice(h * GLA_DK, (h + 1) * GLA_DK)
        vs = slice(h * GLA_DV, (h + 1) * GLA_DV)
        q = q_ref[:, ks] * (GLA_DK ** -0.5)
        k = k_ref[:, ks]
        vb = v_ref[:, vs].astype(bf16)
        b = b_ref[:, ks]
        b_last = b_ref[chunk - 1:chunk, ks]
        state = st_ref[h]
        o = jnp.dot((q * jnp.exp(b)).astype(bf16), state.astype(bf16), preferred_element_type=f32)

        att_ref[h] = jnp.zeros((chunk, chunk), f32)
        for blk in range(chunk // GLA_SUB):
            r0 = blk * GLA_SUB
            q_i = q[r0:r0 + GLA_SUB]
            b_i = b[r0:r0 + GLA_SUB]
            if blk > 0:
                c_i = b_ref[r0 - 1:r0, ks]
                q_dec = (q_i * jnp.exp(b_i - c_i)).astype(bf16)
                k_dec = (k[:r0] * jnp.exp(c_i - b[:r0])).astype(bf16)
                att_ref[h, r0:r0 + GLA_SUB, 0:r0] = lax.dot_general(q_dec, k_dec, nt,
                                                                   preferred_element_type=f32)
            diag = jnp.zeros((GLA_SUB, GLA_SUB), f32)
            for j in range(GLA_SUB):
                b_j = b_ref[r0 + j:r0 + j + 1, ks]
                k_j = k_ref[r0 + j:r0 + j + 1, ks]
                decay = jnp.exp(jnp.where(sub_row >= j, b_i - b_j, -jnp.inf))
                a_j = jnp.sum(q_i * decay * k_j, axis=1, keepdims=True)
                diag = jnp.where(sub_col == j, a_j, diag)
            att_ref[h, r0:r0 + GLA_SUB, r0:r0 + GLA_SUB] = diag

        o = o + jnp.dot(att_ref[h].astype(bf16), vb, preferred_element_type=f32)

        k_end = (k * jnp.exp(b_last - b)).astype(bf16)
        upd = lax.dot_general(k_end, vb, tn, preferred_element_type=f32)
        dec_col = jnp.transpose(jnp.broadcast_to(jnp.exp(b_last), (8, GLA_DK)))[:, 0:1]
        st_ref[h] = state * dec_col + upd

        o = o * lax.rsqrt(jnp.mean(o * o, axis=1, keepdims=True) + EPS) * gn_ref[...]
        r = r_ref[:, vs]
        o_ref[:, vs] = (o * (r * _sig(r))).astype(o_ref.dtype)

    @pl.when(t == nchunks - 1)
    def _():
        sout_ref[0] = st_ref[...]


def _gla(proj, small, w2, bg, gn, l, s0, l_s0, prev, *, row0, nseq, seqlen, chunk):
    nchunks = seqlen // chunk
    blk0 = row0 // chunk
    rows = lambda b, t: blk0 + b * nchunks + t
    qk, vw = GLA_QK, GLA_VW
    hd = (GLA_HEADS, GLA_DK, GLA_DV)
    return pl.pallas_call(
        functools.partial(_gla_body, chunk=chunk, nchunks=nchunks),
        grid=(nseq, nchunks),
        in_specs=[pl.BlockSpec((chunk, qk), lambda b, t: (rows(b, t), P_Q // qk)),
                  pl.BlockSpec((chunk, qk), lambda b, t: (rows(b, t), P_K // qk)),
                  pl.BlockSpec((chunk, vw), lambda b, t: (rows(b, t), P_V // vw)),
                  pl.BlockSpec((chunk, vw), lambda b, t: (rows(b, t), P_R // vw)),
                  pl.BlockSpec((chunk, LANES), lambda b, t: (rows(b, t), 0)),
                  pl.BlockSpec((None, GLA_GATE_RANK, qk), lambda b, t: (l, 0, 0)),
                  pl.BlockSpec((None, 1, qk), lambda b, t: (l, 0, 0)),
                  pl.BlockSpec((None, 1, GLA_DV), lambda b, t: (l, 0, 0)),
                  pl.BlockSpec((None, 1) + hd, lambda b, t: (l_s0, b, 0, 0, 0)),
                  pl.BlockSpec(memory_space=pl.ANY)],
        out_specs=[pl.BlockSpec((chunk, vw), lambda b, t: (rows(b, t), 0)),
                   pl.BlockSpec((1,) + hd, lambda b, t: (b, 0, 0, 0))],
        out_shape=[jax.ShapeDtypeStruct(prev.shape, bf16),
                   jax.ShapeDtypeStruct((nseq,) + hd, f32)],
        scratch_shapes=[pltpu.VMEM(hd, f32), pltpu.VMEM((chunk, qk), f32),
                        pltpu.VMEM((GLA_HEADS, chunk, chunk), f32)],
        input_output_aliases={9: 0},
        compiler_params=_params("arbitrary", "arbitrary"),
        name="gla")(proj, proj, proj, proj, small, w2, bg, gn, s0, prev)


def _ssd_body(z_ref, x_ref, bm_ref, cm_ref, sm_ref, wx_ref, wb_ref, wc_ref, bx_ref, bb_ref, bc_ref,
              sx_ref, sb_ref, sc_ref, dtb_ref, alog_ref, dskip_ref, nrm_ref, h0_ref, prev_ref,
              y_ref, hout_ref, cxo_ref, cbo_ref, cco_ref,
              h_ref, xpx_ref, xpb_ref, xpc_ref, xs_ref, *, chunk, nchunks):
    del prev_ref
    t = pl.program_id(1)
    hist = CONV_PAD - (SSM_CONV - 1)

    @pl.when(t == 0)
    def _():
        h_ref[...] = h0_ref[0]
        xpx_ref[hist:CONV_PAD, :] = sx_ref[0]
        xpb_ref[hist:CONV_PAD, :] = sb_ref[0]
        xpc_ref[hist:CONV_PAD, :] = sc_ref[0]

    def conv_silu(xp_ref, cur_ref, w_ref, bias_ref):
        x = cur_ref[...]
        taps = SSM_CONV - 1
        acc = bias_ref[...] + x * w_ref[taps:taps + 1, :]
        for s in range(1, SSM_CONV):
            acc = acc + pltpu.roll(x, s, axis=0) * w_ref[taps - s:taps - s + 1, :]
        xp_ref[CONV_PAD:2 * CONV_PAD, :] = x[0:CONV_PAD]
        head = bias_ref[...] + xp_ref[pl.ds(hist, CONV_PAD), :] * w_ref[0:1, :]
        for i in range(1, SSM_CONV):
            head = head + xp_ref[pl.ds(hist + i, CONV_PAD), :] * w_ref[i:i + 1, :]
        xp_ref[hist:CONV_PAD, :] = x[chunk - taps:chunk]
        acc = jnp.concatenate([head, acc[CONV_PAD:]], axis=0)
        return acc * _sig(acc)

    xs_ref[...] = conv_silu(xpx_ref, x_ref, wx_ref, bx_ref)
    bmat = conv_silu(xpb_ref, bm_ref, wb_ref, bb_ref).astype(bf16)
    cmat = conv_silu(xpc_ref, cm_ref, wc_ref, bc_ref).astype(bf16)

    @pl.when(t == nchunks - 1)
    def _():
        cxo_ref[0] = xpx_ref[hist:CONV_PAD, :]
        cbo_ref[0] = xpb_ref[hist:CONV_PAD, :]
        cco_ref[0] = xpc_ref[hist:CONV_PAD, :]

    dt = _softplus(sm_ref[...] + dtb_ref[...])
    cum = _cumsum_rows(dt * -jnp.exp(alog_ref[...]))
    last = cum[chunk - 1:chunk, :]
    w_end = jnp.exp(last - cum) * dt
    e_cum = jnp.exp(cum)
    e_last = jnp.exp(last)
    cum2 = cum * LOG2E
    adj2_t = ((cum - jnp.log(dt)) * LOG2E).T

    row = lax.broadcasted_iota(jnp.int32, (chunk, chunk), 0)
    col = lax.broadcasted_iota(jnp.int32, (chunk, chunk), 1)
    causal = row >= col
    head = lax.broadcasted_iota(jnp.int32, (LANES, SSM_GROUP_W), 0)
    chan = lax.broadcasted_iota(jnp.int32, (LANES, SSM_GROUP_W), 1)
    nt = (((1,), (1,)), ((), ()))
    tn = (((0,), (0,)), ((), ()))
    for g in range(SSM_GROUPS):
        cs = slice(g * SSM_GROUP_W, (g + 1) * SSM_GROUP_W)
        ns = slice(g * SSM_D_STATE, (g + 1) * SSM_D_STATE)
        xs = xs_ref[:, cs]
        xsb = xs.astype(bf16)
        b_g = bmat[:, ns]
        c_g = cmat[:, ns]
        widen = (head == g * SSM_HPG + chan // SSM_HEAD_DIM).astype(bf16)
        w_full = _dot_exact01(w_end, widen, 2)
        ecum_full = _dot_exact01(e_cum, widen, 2)

        cb = lax.dot_general(c_g, b_g, nt, preferred_element_type=f32)
        state = h_ref[g]
        y = lax.dot_general(c_g, state.astype(bf16), nt, preferred_element_type=f32) * ecum_full
        parts = []
        for j in range(SSM_HPG):
            hd = g * SSM_HPG + j
            seg = jnp.exp2(jnp.where(causal, cum2[:, hd:hd + 1] - adj2_t[hd:hd + 1, :], -jnp.inf))
            parts.append(jnp.dot((cb * seg).astype(bf16), xsb[:, j * SSM_HEAD_DIM:(j + 1) * SSM_HEAD_DIM],
                                 preferred_element_type=f32))
        y = y + jnp.concatenate(parts, axis=1)

        upd = lax.dot_general((xs * w_full).astype(bf16), b_g, tn, preferred_element_type=f32)
        for j in range(SSM_HPG):
            hd = g * SSM_HPG + j
            rs = slice(j * SSM_HEAD_DIM, (j + 1) * SSM_HEAD_DIM)
            h_ref[g, rs, :] = state[rs, :] * e_last[:, hd:hd + 1] + upd[rs, :]

        y = y + dskip_ref[:, cs] * xs
        z = z_ref[:, cs]
        y = y * (z * _sig(z))
        y = y * lax.rsqrt(jnp.mean(y * y, axis=1, keepdims=True) + EPS) * nrm_ref[:, cs]
        y_ref[:, cs] = y.astype(y_ref.dtype)

    @pl.when(t == nchunks - 1)
    def _():
        hout_ref[0] = h_ref[...]


def _ssd(proj, small, conv_w, conv_b, l, conv_s, l_cs, dtb, alog, dskip, nrm, h0, l_h0, prev,
         *, row0, nseq, seqlen, chunk):
    nchunks = seqlen // chunk
    blk0 = row0 // chunk
    di, bc = SSM_D_INNER, SSM_BC
    rows = lambda b, t: blk0 + b * nchunks + t
    taps, hist = SSM_CONV, SSM_CONV - 1
    xb, bb, cb = 0, di // bc, di // bc + 1
    hs = (SSM_GROUPS, SSM_GROUP_W, SSM_D_STATE)
    fix3 = lambda b, t: (l, 0, 0)
    in_specs = [
        pl.BlockSpec((chunk, di), lambda b, t: (rows(b, t), P_Z // di)),
        pl.BlockSpec((chunk, di), lambda b, t: (rows(b, t), P_X // di)),
        pl.BlockSpec((chunk, bc), lambda b, t: (rows(b, t), P_B // bc)),
        pl.BlockSpec((chunk, bc), lambda b, t: (rows(b, t), P_C // bc)),
        pl.BlockSpec((chunk, LANES), lambda b, t: (rows(b, t), 0)),
        pl.BlockSpec((None, taps, di), lambda b, t: (l, 0, xb)),
        pl.BlockSpec((None, taps, bc), lambda b, t: (l, 0, bb)),
        pl.BlockSpec((None, taps, bc), lambda b, t: (l, 0, cb)),
        pl.BlockSpec((None, 1, di), lambda b, t: (l, 0, xb)),
        pl.BlockSpec((None, 1, bc), lambda b, t: (l, 0, bb)),
        pl.BlockSpec((None, 1, bc), lambda b, t: (l, 0, cb)),
        pl.BlockSpec((None, 1, hist, di), lambda b, t: (l_cs, b, 0, xb)),
        pl.BlockSpec((None, 1, hist, bc), lambda b, t: (l_cs, b, 0, bb)),
        pl.BlockSpec((None, 1, hist, bc), lambda b, t: (l_cs, b, 0, cb)),
        pl.BlockSpec((None, 1, LANES), fix3),
        pl.BlockSpec((None, 1, LANES), fix3),
        pl.BlockSpec((None, 1, di), fix3),
        pl.BlockSpec((None, 1, di), fix3),
        pl.BlockSpec((None, 1) + hs, lambda b, t: (l_h0, b, 0, 0, 0)),
        pl.BlockSpec(memory_space=pl.ANY),
    ]
    return pl.pallas_call(
        functools.partial(_ssd_body, chunk=chunk, nchunks=nchunks),
        grid=(nseq, nchunks),
        in_specs=in_specs,
        out_specs=[pl.BlockSpec((chunk, di), lambda b, t: (rows(b, t), 0)),
                   pl.BlockSpec((1,) + hs, lambda b, t: (b, 0, 0, 0)),
                   pl.BlockSpec((1, hist, di), lambda b, t: (b, 0, 0)),
                   pl.BlockSpec((1, hist, bc), lambda b, t: (b, 0, 0)),
                   pl.BlockSpec((1, hist, bc), lambda b, t: (b, 0, 0))],
        out_shape=[jax.ShapeDtypeStruct(prev.shape, bf16),
                   jax.ShapeDtypeStruct((nseq,) + hs, f32),
                   jax.ShapeDtypeStruct((nseq, hist, di), f32),
                   jax.ShapeDtypeStruct((nseq, hist, bc), f32),
                   jax.ShapeDtypeStruct((nseq, hist, bc), f32)],
        scratch_shapes=[pltpu.VMEM(hs, f32), pltpu.VMEM((2 * CONV_PAD, di), f32),
                        pltpu.VMEM((2 * CONV_PAD, bc), f32), pltpu.VMEM((2 * CONV_PAD, bc), f32),
                        pltpu.VMEM((chunk, di), f32)],
        input_output_aliases={19: 0},
        compiler_params=_params("arbitrary", "arbitrary"),
        name="ssd")(proj, proj, proj, proj, small, conv_w, conv_w, conv_w, conv_b, conv_b, conv_b,
                    conv_s, conv_s, conv_s, dtb, alog, dskip, nrm, h0, prev)


def _pad_lanes(v):
    return jnp.pad(v, ((0, 0), (0, LANES - v.shape[-1])))[:, None, :]


def kernel(x_prompt, x_sample, state_gla, state_ssm, state_conv, norm_mix, w_in, gla_w_gate2, gla_b_gate,
           gla_norm, gla_w_out, ssm_conv_w, ssm_conv_b, ssm_dt_bias, ssm_a_log, ssm_d, ssm_norm, ssm_w_out,
           w_o, norm_ffn, router_w1, router_b1, router_w2, router_b2, exp_w_gate, exp_w_up, exp_w_down,
           norm_final):
    n_p, t_p = x_prompt.shape[:2]
    n_s, t_s = x_sample.shape[:2]
    assert n_p == 1
    rows_p, rows_s = n_p * t_p, n_s * t_s
    m = rows_p + rows_s
    tm = 832 if m % 832 == 0 else 128
    tm_out = 416 if m % 416 == 0 else 128
    tm_in = 1664 if m % 1664 == 0 else tm
    gla_chunk_p, ssd_chunk_p = min(64, t_p), min(256, t_p)

    x = jnp.concatenate([x_prompt.reshape(rows_p, D_MODEL), x_sample.reshape(rows_s, D_MODEL)], axis=0)
    zeros_gla = jnp.zeros((1, n_p, GLA_HEADS, GLA_DK, GLA_DV), f32)
    zeros_ssm = jnp.zeros((1, n_p, SSM_GROUPS, SSM_GROUP_W, SSM_D_STATE), f32)
    zeros_conv = jnp.zeros((1, n_p, SSM_CONV - 1, SSM_CONV_DIM), f32)
    state_ssm_g = state_ssm.reshape(DEPTH, n_s, SSM_GROUPS, SSM_GROUP_W, SSM_D_STATE)

    bg3 = gla_b_gate[:, None, :]
    gn3 = gla_norm[:, None, :]
    conv_b3 = ssm_conv_b[:, None, :]
    dtb3 = _pad_lanes(ssm_dt_bias)
    alog3 = _pad_lanes(ssm_a_log)
    dskip3 = jnp.repeat(ssm_d, SSM_HEAD_DIM, axis=1)[:, None, :]
    nrm3 = ssm_norm[:, None, :]

    w_in_t = jnp.swapaxes(w_in, 1, 2)
    oa = jnp.zeros((m, GLA_VW), bf16)
    ob = jnp.zeros((m, SSM_D_INNER), bf16)

    outs = {k: [] for k in ("gla_p", "ssm_p", "conv_p", "gla_s", "ssm_s", "conv_s")}
    for l in range(DEPTH):
        w_small_t = jnp.concatenate(
            [w_in_t[l, W_DT:W_MG], w_in_t[l, W_GLR:W_Z],
             jnp.zeros((LANES - SSM_HEADS - GLA_GATE_RANK, D_MODEL), f32)], axis=0)

        u = _rmsnorm(x, norm_mix[l], rows=m, row0=0, tm=tm, out_dtype=bf16)
        proj = _inproj(u, w_in_t, l, tm=tm_in)
        small = _matmul_nt(u, w_small_t, tm=tm, name="inproj_small")

        gla_args = (proj, small, gla_w_gate2, bg3, gn3, l)
        oa, g_p = _gla(*gla_args, zeros_gla, 0, oa, row0=0, nseq=n_p, seqlen=t_p, chunk=gla_chunk_p)
        oa, g_s = _gla(*gla_args, state_gla, l, oa, row0=rows_p, nseq=n_s, seqlen=t_s, chunk=t_s)

        ssd_args = (proj, small, ssm_conv_w, conv_b3, l)
        ssd_tail = (dtb3, alog3, dskip3, nrm3)
        ob, h_p, *conv_p = _ssd(*ssd_args, zeros_conv, 0, *ssd_tail, zeros_ssm, 0, ob,
                                row0=0, nseq=n_p, seqlen=t_p, chunk=ssd_chunk_p)
        ob, h_s, *conv_s = _ssd(*ssd_args, state_conv, l, *ssd_tail, state_ssm_g, l, ob,
                                row0=rows_p, nseq=n_s, seqlen=t_s, chunk=t_s)

        merged = _merge(oa, ob, proj, gla_w_out, ssm_w_out, l, tm=tm_out, tn=512)

        wr = jnp.concatenate([router_w1[l], router_w2[l],
                              jnp.zeros((D_MODEL, LANES - MOE_GROUPS - MOE_EXPERTS), f32)], axis=1)
        br = jnp.pad(jnp.concatenate([router_b1[l], router_b2[l]]),
                     (0, LANES - MOE_GROUPS - MOE_EXPERTS)).reshape(1, LANES)
        xn, u2, cw, oh, cnt = _outproj(merged, w_o, l, x, norm_ffn[l].reshape(1, -1), wr, br, tm=tm_out)
        x = _moe(u2, cw, oh, cnt, xn, exp_w_gate, exp_w_up, exp_w_down, l)

        outs["gla_p"].append(g_p)
        outs["gla_s"].append(g_s)
        outs["ssm_p"].append(h_p.reshape(n_p, SSM_HEADS, SSM_HEAD_DIM, SSM_D_STATE))
        outs["ssm_s"].append(h_s.reshape(n_s, SSM_HEADS, SSM_HEAD_DIM, SSM_D_STATE))
        outs["conv_p"].append(jnp.concatenate(conv_p, axis=-1))
        outs["conv_s"].append(jnp.concatenate(conv_s, axis=-1))

    tm_p = 512 if rows_p % 512 == 0 else 128
    tm_s = rows_s if rows_p % rows_s == 0 else 8
    y_p = _rmsnorm(x, norm_final, rows=rows_p, row0=0, tm=tm_p, out_dtype=f32)
    y_s = _rmsnorm(x, norm_final, rows=rows_s, row0=rows_p, tm=tm_s, out_dtype=f32)
    return (y_p.reshape(n_p, t_p, D_MODEL), y_s.reshape(n_s, t_s, D_MODEL),
            jnp.stack(outs["gla_p"]), jnp.stack(outs["ssm_p"]), jnp.stack(outs["conv_p"]),
            jnp.stack(outs["gla_s"]), jnp.stack(outs["ssm_s"]), jnp.stack(outs["conv_s"]))
```

```python
import functools
import math

import jax
import jax.numpy as jnp
from jax import lax
from jax.experimental import pallas as pl
from jax.experimental.pallas import tpu as pltpu

f32 = jnp.float32
bf16 = jnp.bfloat16
HIGHEST = lax.Precision.HIGHEST
LOG2E = math.log2(math.e)

D_MODEL = 2048
DEPTH = 2
EPS = 1e-6
GLA_HEADS = 4
GLA_DK = 256
GLA_DV = 512
GLA_QK = GLA_HEADS * GLA_DK
GLA_VW = GLA_HEADS * GLA_DV
GLA_GATE_RANK = 16
GLA_GATE_TAU = 16.0
GLA_SUB = 16
SSM_D_INNER = 4096
SSM_HEAD_DIM = 64
SSM_HEADS = 64
SSM_GROUPS = 8
SSM_HPG = 8
SSM_GROUP_W = SSM_HPG * SSM_HEAD_DIM
SSM_D_STATE = 128
SSM_BC = SSM_GROUPS * SSM_D_STATE
SSM_CONV = 4
SSM_CONV_DIM = 6144
MOE_GROUPS = 4
MOE_PER_GROUP = 4
MOE_EXPERTS = 16
MOE_D_FF = 512
MOE_TOP_K = 2
MOE_LANE0 = MOE_GROUPS
MOE_TILE = 256
LANES = 128
CONV_PAD = 8

W_GLR = 6144
W_Z = 6160
W_DT = 16400
W_MG = 16464
IN_TN = 1024
IN_TILES = 20480 // IN_TN
IN_ALIGNED_TILES = W_GLR // IN_TN
IN_MG_TILE0 = (W_DT - W_Z + W_GLR) // IN_TN
IN_SHIFT_Z = W_Z - W_GLR
IN_SHIFT_MG = W_MG - IN_MG_TILE0 * IN_TN
P_Z, P_X, P_Q, P_K, P_V, P_R, P_B, P_C, P_GA, P_GB, P_END = (
    0, 4096, 8192, 9216, 10240, 12288, 14336, 15360, 16384, 18432, 20480)
S_GLR = 64

VMEM_LIMIT = 56 * 1024 * 1024


def _sig(x):
    return 1.0 / (1.0 + jnp.exp(-x))


def _softplus(x):
    return jnp.maximum(x, 0.0) + jnp.log1p(jnp.exp(-jnp.abs(x)))


def _log_sigmoid(x):
    return jnp.minimum(x, 0.0) - jnp.log1p(jnp.exp(-jnp.abs(x)))


def _params(*sem):
    return pltpu.CompilerParams(dimension_semantics=sem, vmem_limit_bytes=VMEM_LIMIT)


def _split_bf16(x, pieces):
    out = []
    for _ in range(pieces - 1):
        p = x.astype(bf16)
        out.append(p)
        x = x - p.astype(f32)
    out.append(x.astype(bf16))
    return out


def _dot_exact01(a, b01, pieces, dims=None):
    acc = None
    for p in _split_bf16(a, pieces):
        if dims is None:
            d = jnp.dot(p, b01, preferred_element_type=f32)
        else:
            d = lax.dot_general(p, b01, dims, preferred_element_type=f32)
        acc = d if acc is None else acc + d
    return acc


def _cumsum_rows(x):
    n = x.shape[0]
    row = lax.broadcasted_iota(jnp.int32, (n, n), 0)
    col = lax.broadcasted_iota(jnp.int32, (n, n), 1)
    tri = (row >= col).astype(bf16)
    acc = None
    for p in _split_bf16(x, 3):
        d = jnp.dot(tri, p, preferred_element_type=f32)
        acc = d if acc is None else acc + d
    return acc


def _norm_body(x_ref, g_ref, o_ref):
    x = x_ref[...]
    y = x * lax.rsqrt(jnp.mean(x * x, axis=-1, keepdims=True) + EPS)
    o_ref[...] = (y * g_ref[...]).astype(o_ref.dtype)


def _rmsnorm(x, g, *, rows, row0, tm, out_dtype):
    blk0 = row0 // tm
    return pl.pallas_call(
        _norm_body, grid=(rows // tm,),
        in_specs=[pl.BlockSpec((tm, D_MODEL), lambda m: (blk0 + m, 0)),
                  pl.BlockSpec((1, D_MODEL), lambda m: (0, 0))],
        out_specs=pl.BlockSpec((tm, D_MODEL), lambda m: (m, 0)),
        out_shape=jax.ShapeDtypeStruct((rows, D_MODEL), out_dtype),
        compiler_params=_params("arbitrary"), name="rmsnorm")(x, g.reshape(1, D_MODEL))


def _inproj_body(u_ref, wa_ref, wb_ref, o_ref, wbf_ref):
    n = pl.program_id(0)

    @pl.when(pl.program_id(1) == 0)
    def _():
        def fill(shift):
            def f():
                if shift == 0:
                    wbf_ref[...] = wa_ref[...].astype(bf16)
                else:
                    wbf_ref[:IN_TN - shift, :] = wa_ref[shift:, :].astype(bf16)
                    wbf_ref[IN_TN - shift:, :] = wb_ref[:shift, :].astype(bf16)
            return f
        pl.when(n < IN_ALIGNED_TILES)(fill(0))
        pl.when((n >= IN_ALIGNED_TILES) & (n < IN_MG_TILE0))(fill(IN_SHIFT_Z))
        pl.when(n >= IN_MG_TILE0)(fill(IN_SHIFT_MG))

    o_ref[...] = lax.dot_general(u_ref[...], wbf_ref[...], (((1,), (1,)), ((), ())),
                                 preferred_element_type=f32)


def _inproj_out_tile(n):
    lead = IN_ALIGNED_TILES
    zx = (P_Q - P_Z) // IN_TN
    return jnp.where(n < lead, n + zx, jnp.where(n < lead + zx, n - lead, n))


def _inproj(u, w_in_t, l, *, tm):
    m = u.shape[0]
    return pl.pallas_call(
        _inproj_body, grid=(IN_TILES, m // tm),
        in_specs=[pl.BlockSpec((tm, D_MODEL), lambda n, i: (i, 0)),
                  pl.BlockSpec((None, IN_TN, D_MODEL), lambda n, i: (l, n, 0)),
                  pl.BlockSpec((None, LANES, D_MODEL), lambda n, i: (l, (n + 1) * (IN_TN // LANES), 0))],
        out_specs=pl.BlockSpec((tm, IN_TN), lambda n, i: (i, _inproj_out_tile(n))),
        out_shape=jax.ShapeDtypeStruct((m, P_END), f32),
        scratch_shapes=[pltpu.VMEM((IN_TN, D_MODEL), bf16)],
        compiler_params=_params("arbitrary", "arbitrary"), name="inproj")(u, w_in_t, w_in_t)


def _mm_nt_body(a_ref, w_ref, o_ref):
    o_ref[...] = lax.dot_general(a_ref[...], w_ref[...].astype(bf16), (((1,), (1,)), ((), ())),
                                 preferred_element_type=f32)


def _matmul_nt(a, w_t, *, tm, name):
    m, k = a.shape
    n = w_t.shape[0]
    return pl.pallas_call(
        _mm_nt_body, grid=(m // tm,),
        in_specs=[pl.BlockSpec((tm, k), lambda i: (i, 0)),
                  pl.BlockSpec((n, k), lambda i: (0, 0))],
        out_specs=pl.BlockSpec((tm, n), lambda i: (i, 0)),
        out_shape=jax.ShapeDtypeStruct((m, n), f32),
        compiler_params=_params("arbitrary"), name=name)(a, w_t)


def _merge_body(oa_ref, ob_ref, ga_ref, gb_ref, wa_ref, wb_ref, o_ref, wabf_ref, wbbf_ref):
    @pl.when(pl.program_id(1) == 0)
    def _():
        wabf_ref[...] = wa_ref[...].astype(bf16)
        wbbf_ref[...] = wb_ref[...].astype(bf16)

    ya = jnp.dot(oa_ref[...], wabf_ref[...], preferred_element_type=f32)
    yb = jnp.dot(ob_ref[...], wbbf_ref[...], preferred_element_type=f32)
    o_ref[...] = (_sig(ga_ref[...]) * ya + _sig(gb_ref[...]) * yb).astype(o_ref.dtype)


def _merge(oa, ob, proj, wa, wb, l, *, tm, tn):
    m = oa.shape[0]
    ga0, gb0 = P_GA // tn, P_GB // tn
    return pl.pallas_call(
        _merge_body, grid=(D_MODEL // tn, m // tm),
        in_specs=[pl.BlockSpec((tm, GLA_VW), lambda j, i: (i, 0)),
                  pl.BlockSpec((tm, SSM_D_INNER), lambda j, i: (i, 0)),
                  pl.BlockSpec((tm, tn), lambda j, i: (i, ga0 + j)),
                  pl.BlockSpec((tm, tn), lambda j, i: (i, gb0 + j)),
                  pl.BlockSpec((None, GLA_VW, tn), lambda j, i: (l, 0, j)),
                  pl.BlockSpec((None, SSM_D_INNER, tn), lambda j, i: (l, 0, j))],
        out_specs=pl.BlockSpec((tm, tn), lambda j, i: (i, j)),
        out_shape=jax.ShapeDtypeStruct((m, D_MODEL), bf16),
        scratch_shapes=[pltpu.VMEM((GLA_VW, tn), bf16), pltpu.VMEM((SSM_D_INNER, tn), bf16)],
        compiler_params=_params("arbitrary", "arbitrary"), name="merge")(oa, ob, proj, proj, wa, wb)


def _route(logits):
    c = lax.broadcasted_iota(jnp.int32, logits.shape, 1)
    big = jnp.int32(1 << 20)
    is_group = c < MOE_GROUPS
    lg = jnp.where(is_group, logits, -jnp.inf)
    top = jnp.max(lg, axis=1, keepdims=True)
    gsel = jnp.min(jnp.where(is_group & (logits == top), c, big), axis=1, keepdims=True)
    p_sel = 1.0 / jnp.sum(jnp.exp(lg - top), axis=1, keepdims=True)
    in_group = ((c >= MOE_LANE0) & (c < MOE_LANE0 + MOE_EXPERTS)
                & (((c - MOE_LANE0) // MOE_PER_GROUP) == gsel))
    le = jnp.where(in_group, logits, -jnp.inf)
    tv0 = jnp.max(le, axis=1, keepdims=True)
    i0 = jnp.min(jnp.where(in_group & (logits == tv0), c, big), axis=1, keepdims=True)
    rest = in_group & (c != i0)
    le1 = jnp.where(rest, logits, -jnp.inf)
    tv1 = jnp.max(le1, axis=1, keepdims=True)
    i1 = jnp.min(jnp.where(rest & (logits == tv1), c, big), axis=1, keepdims=True)
    t = jnp.exp(tv1 - tv0)
    w0 = p_sel / (1.0 + t)
    w1 = p_sel * t / (1.0 + t)
    cw = jnp.where(c == i0, w0, jnp.where(c == i1, w1, 0.0))
    return cw, ((c == i0) | (c == i1)).astype(f32)


def _outproj_body(m_ref, wo_ref, x_ref, g_ref, wr_ref, br_ref, xn_ref, u_ref, cw_ref, oh_ref, cnt_ref,
                  wobf_ref):
    @pl.when(pl.program_id(0) == 0)
    def _():
        wobf_ref[...] = wo_ref[...].astype(bf16)
        cnt_ref[...] = jnp.zeros_like(cnt_ref)

    w_hi, w_lo = _split_bf16(wr_ref[...], 2)
    tm = x_ref.shape[0]
    parts = 2 if tm % 32 == 0 else 1
    for p in range(parts):
        rs = slice(p * (tm // parts), (p + 1) * (tm // parts))
        xn = x_ref[rs, :] + jnp.dot(m_ref[rs, :], wobf_ref[...], preferred_element_type=f32)
        xn_ref[rs, :] = xn
        u = xn * lax.rsqrt(jnp.mean(xn * xn, axis=-1, keepdims=True) + EPS) * g_ref[...]
        u_ref[rs, :] = u
        u_hi, u_lo = _split_bf16(u, 2)
        logits = (jnp.dot(u_hi, w_hi, preferred_element_type=f32)
                  + (jnp.dot(u_hi, w_lo, preferred_element_type=f32)
                     + jnp.dot(u_lo, w_hi, preferred_element_type=f32))) + br_ref[...]
        cw, oh = _route(logits)
        cw_ref[rs, :] = cw
        oh_ref[rs, :] = oh
        cnt_ref[...] += jnp.sum(oh, axis=0, keepdims=True)


def _outproj(merged, wo, l, x, g, wr, br, *, tm):
    m = x.shape[0]
    row = lambda i: (i, 0)
    fix = lambda i: (0, 0)
    return pl.pallas_call(
        _outproj_body, grid=(m // tm,),
        in_specs=[pl.BlockSpec((tm, D_MODEL), row),
                  pl.BlockSpec((None, D_MODEL, D_MODEL), lambda i: (l, 0, 0), pipeline_mode=pl.Buffered(1)),
                  pl.BlockSpec((tm, D_MODEL), row), pl.BlockSpec((1, D_MODEL), fix),
                  pl.BlockSpec((D_MODEL, LANES), fix), pl.BlockSpec((1, LANES), fix)],
        out_specs=[pl.BlockSpec((tm, D_MODEL), row), pl.BlockSpec((tm, D_MODEL), row),
                   pl.BlockSpec((tm, LANES), row), pl.BlockSpec((tm, LANES), row), pl.BlockSpec((1, LANES), fix)],
        out_shape=[jax.ShapeDtypeStruct((m, D_MODEL), f32), jax.ShapeDtypeStruct((m, D_MODEL), f32),
                   jax.ShapeDtypeStruct((m, LANES), f32), jax.ShapeDtypeStruct((m, LANES), f32),
                   jax.ShapeDtypeStruct((1, LANES), f32)],
        scratch_shapes=[pltpu.VMEM((D_MODEL, D_MODEL), bf16)],
        compiler_params=_params("arbitrary"), name="outproj")(merged, wo, x, g, wr, br)


def _plan_body(oh_ref, cw_ref, cnt_ref, dest_ref, wts_ref, te_ref, pad_ref, base_ref, carry_ref, *, rows):
    lane = lax.broadcasted_iota(jnp.int32, (8, LANES), 1)
    is_expert = (lane >= MOE_LANE0) & (lane < MOE_LANE0 + MOE_EXPERTS)

    @pl.when(pl.program_id(0) == 0)
    def _():
        cnt = jnp.broadcast_to(cnt_ref[...], (8, LANES))
        padded = jnp.floor((cnt + (MOE_TILE - 1)) * (1.0 / MOE_TILE)) * MOE_TILE
        src = lax.broadcasted_iota(jnp.int32, (LANES, LANES), 0)
        dst = lax.broadcasted_iota(jnp.int32, (LANES, LANES), 1)
        base = _dot_exact01(padded, (src < dst).astype(bf16), 3)
        base_ref[...] = base[0:1]
        carry_ref[...] = jnp.zeros_like(carry_ref)
        end = base + padded
        total = jnp.max(jnp.where(is_expert, end, 0.0), axis=1, keepdims=True)
        end_col = jnp.transpose(jnp.where(is_expert, end, jnp.inf))[:, 0:1]
        tile_start = (lax.broadcasted_iota(jnp.int32, (LANES, LANES), 1) * MOE_TILE).astype(f32)
        tile_exp = jnp.sum((end_col <= tile_start).astype(f32), axis=0, keepdims=True)
        tile_exp = jnp.minimum(tile_exp, MOE_EXPERTS - 1.0)
        te_ref[0:1, :] = tile_exp.astype(jnp.int32)
        te_ref[1:2, :] = jnp.broadcast_to(total[0:1] * (1.0 / MOE_TILE), (1, LANES)).astype(jnp.int32)
        shift = (src == dst + MOE_LANE0).astype(bf16)
        is_tail = lane[0:1] == MOE_EXPERTS
        pad_lo = jnp.where(is_tail, total[0:1], _dot_exact01(base + cnt, shift, 3)[0:1])
        pad_hi = jnp.where(is_tail, float(rows), _dot_exact01(end, shift, 3)[0:1])
        pad_ref[0:1, :] = pad_lo.astype(jnp.int32)
        pad_ref[1:2, :] = pad_hi.astype(jnp.int32)

    oh = oh_ref[...]
    n = oh.shape[0]
    row = lax.broadcasted_iota(jnp.int32, (n, n), 0)
    col = lax.broadcasted_iota(jnp.int32, (n, n), 1)
    rank = jnp.dot((row > col).astype(bf16), oh.astype(bf16), preferred_element_type=f32)
    pos = base_ref[...] + carry_ref[...] + rank
    carry_ref[...] += jnp.sum(oh, axis=0, keepdims=True)
    chosen = oh > 0.0
    d_lo = jnp.min(jnp.where(chosen, pos, jnp.inf), axis=1, keepdims=True)
    d_hi = jnp.max(jnp.where(chosen, pos, -1.0), axis=1, keepdims=True)
    cw = cw_ref[...]
    w_lo = jnp.sum(jnp.where(chosen & (pos == d_lo), cw, 0.0), axis=1, keepdims=True)
    w_hi = jnp.sum(jnp.where(chosen & (pos == d_hi), cw, 0.0), axis=1, keepdims=True)
    lanes = lax.broadcasted_iota(jnp.int32, oh.shape, 1)
    wts_ref[...] = jnp.where(lanes == 0, w_lo, jnp.where(lanes == 1, w_hi, 0.0))
    dests = jnp.where(lanes == 0, d_lo, jnp.where(lanes == 1, d_hi, 0.0))
    dest_ref[0] = jnp.transpose(dests)[0:2, :].astype(jnp.int32)


def _moe_plan(oh, cw, cnt, *, blk, rows):
    m = oh.shape[0]
    nblk = m // blk
    fix = lambda i: (0, 0)
    return pl.pallas_call(
        functools.partial(_plan_body, rows=rows), grid=(nblk,),
        in_specs=[pl.BlockSpec((blk, LANES), lambda i: (i, 0)), pl.BlockSpec((blk, LANES), lambda i: (i, 0)),
                  pl.BlockSpec((1, LANES), fix)],
        out_specs=[pl.BlockSpec((1, 2, blk), lambda i: (i, 0, 0)), pl.BlockSpec((blk, LANES), lambda i: (i, 0)),
                   pl.BlockSpec((2, LANES), fix), pl.BlockSpec((2, LANES), fix)],
        out_shape=[jax.ShapeDtypeStruct((nblk, 2, blk), jnp.int32), jax.ShapeDtypeStruct((m, LANES), f32),
                   jax.ShapeDtypeStruct((2, LANES), jnp.int32), jax.ShapeDtypeStruct((2, LANES), jnp.int32)],
        scratch_shapes=[pltpu.VMEM((1, LANES), f32), pltpu.VMEM((1, LANES), f32)],
        compiler_params=_params("arbitrary"), name="moe_plan")(oh, cw, cnt)


def _row_copy(src_ref, s, dst_ref, d, sem):
    return pltpu.make_async_copy(src_ref.at[pl.ds(s, 1), :], dst_ref.at[pl.ds(d, 1), :], sem)


def _dispatch_body(dest_ref, pad_ref, u_ref, xs_ref, zero_ref, sem, *, blk, nsub):
    i = pl.program_id(0)

    def wait_rows(count, which):
        pltpu.make_async_copy(xs_ref.at[pl.ds(0, count), :], xs_ref.at[pl.ds(0, count), :], sem.at[which]).wait()

    @pl.when(i == 0)
    def _():
        zero_ref[...] = jnp.zeros_like(zero_ref)
        group = zero_ref.shape[0]

        def fill_row(r, c):
            _row_copy(zero_ref, 0, xs_ref, r, sem.at[1]).start()
            return c

        def fill_group(lo8):
            def f(r, c):
                row0 = pl.multiple_of(lo8 + r * group, group)
                pltpu.make_async_copy(zero_ref, xs_ref.at[pl.ds(row0, group), :], sem.at[1]).start()
                return c
            return f

        nrow = jnp.int32(0)
        ngroup = jnp.int32(0)
        for e in range(MOE_EXPERTS + 1):
            lo = pad_ref[0, e]
            hi = pad_ref[1, e]
            lo8 = jnp.minimum((lo + group - 1) // group * group, hi)
            lax.fori_loop(lo, lo8, fill_row, 0)
            lax.fori_loop(0, (hi - lo8) // group, fill_group(lo8), 0)
            nrow = nrow + (lo8 - lo)
            ngroup = ngroup + (hi - lo8) // group

        def wait_row(r, c):
            _row_copy(zero_ref, 0, xs_ref, 0, sem.at[1]).wait()
            return c
        lax.fori_loop(0, nrow, wait_row, 0)

        def wait_group(r, c):
            wait_rows(group, 1)
            return c
        lax.fori_loop(0, ngroup, wait_group, 0)

    for s in range(nsub):
        def send(j, c, s=s):
            _row_copy(u_ref, s * blk + j, xs_ref, dest_ref[s, 0, j], sem.at[0]).start()
            _row_copy(u_ref, s * blk + j, xs_ref, dest_ref[s, 1, j], sem.at[0]).start()
            return c
        lax.fori_loop(0, blk, send, 0)
    for _ in range(MOE_TOP_K * nsub):
        wait_rows(blk, 0)


def _moe_dispatch(u, dest, pad, *, rows, blk):
    m = u.shape[0]
    nblk = m // blk
    nsub = max(d for d in range(1, 6) if nblk % d == 0)
    return pl.pallas_call(
        functools.partial(_dispatch_body, blk=blk, nsub=nsub), grid=(nblk // nsub,),
        in_specs=[pl.BlockSpec((nsub, 2, blk), lambda i: (i, 0, 0), memory_space=pltpu.SMEM),
                  pl.BlockSpec(memory_space=pltpu.SMEM),
                  pl.BlockSpec((nsub * blk, D_MODEL), lambda i: (i, 0))],
        out_specs=pl.BlockSpec(memory_space=pl.ANY),
        out_shape=jax.ShapeDtypeStruct((rows, D_MODEL), f32),
        scratch_shapes=[pltpu.VMEM((8, D_MODEL), f32), pltpu.SemaphoreType.DMA((2,))],
        compiler_params=_params("arbitrary"), name="moe_dispatch")(dest, pad, u)


def _gmm_body(te_ref, xs_ref, wg_ref, wu_ref, wd_ref, ys_ref, wgb_ref, wub_ref, wdb_ref):
    i = pl.program_id(0)
    n_used = te_ref[1, 0]

    @pl.when(i < n_used)
    def _():
        @pl.when((i == 0) | (te_ref[0, i] != te_ref[0, jnp.maximum(i - 1, 0)]))
        def _():
            wgb_ref[...] = wg_ref[...].astype(bf16)
            wub_ref[...] = wu_ref[...].astype(bf16)
            wdb_ref[...] = wd_ref[...].astype(bf16)

        x = xs_ref[...].astype(bf16)
        hg = jnp.dot(x, wgb_ref[...], preferred_element_type=f32)
        hu = jnp.dot(x, wub_ref[...], preferred_element_type=f32)
        h = ((hg * _sig(hg)) * hu).astype(bf16)
        ys_ref[...] = jnp.dot(h, wdb_ref[...], preferred_element_type=f32)

    @pl.when(i >= n_used)
    def _():
        ys_ref[...] = jnp.zeros_like(ys_ref)


def _moe_gmm(te, xs, wg, wu, wd, l):
    rows = xs.shape[0]
    ntiles = rows // MOE_TILE
    tile = lambda i, te: (jnp.minimum(i, te[1, 0] - 1), 0)
    up = lambda i, te: (l, te[0, i], 0, 0)
    return pl.pallas_call(
        _gmm_body,
        grid_spec=pltpu.PrefetchScalarGridSpec(
            num_scalar_prefetch=1, grid=(ntiles,),
            in_specs=[pl.BlockSpec((MOE_TILE, D_MODEL), tile),
                      pl.BlockSpec((None, None, D_MODEL, MOE_D_FF), up),
                      pl.BlockSpec((None, None, D_MODEL, MOE_D_FF), up),
                      pl.BlockSpec((None, None, MOE_D_FF, D_MODEL), up)],
            out_specs=pl.BlockSpec((MOE_TILE, D_MODEL), lambda i, te: (i, 0)),
            scratch_shapes=[pltpu.VMEM((D_MODEL, MOE_D_FF), bf16), pltpu.VMEM((D_MODEL, MOE_D_FF), bf16),
                            pltpu.VMEM((MOE_D_FF, D_MODEL), bf16)]),
        out_shape=jax.ShapeDtypeStruct((rows, D_MODEL), f32),
        compiler_params=_params("arbitrary"), name="moe_gmm")(te, xs, wg, wu, wd)


def _combine_body(dcur_ref, dnext_ref, xn_ref, wts_ref, ys_ref, g_ref, o_ref, un_ref, ya_ref, yb_ref, sem,
                  *, blk, nblk):
    i = pl.program_id(0)
    slot = i % 2

    def gather(dref, s):
        def one(j, c):
            _row_copy(ys_ref, dref[0, 0, j], ya_ref.at[s], j, sem.at[s]).start()
            _row_copy(ys_ref, dref[0, 1, j], yb_ref.at[s], j, sem.at[s]).start()
            return c
        lax.fori_loop(0, blk, one, 0)

    @pl.when(i == 0)
    def _():
        gather(dcur_ref, 0)

    @pl.when(i + 1 < nblk)
    def _():
        gather(dnext_ref, 1 - slot)

    pltpu.make_async_copy(ys_ref.at[pl.ds(0, blk), :], ya_ref.at[slot], sem.at[slot]).wait()
    pltpu.make_async_copy(ys_ref.at[pl.ds(0, blk), :], yb_ref.at[slot], sem.at[slot]).wait()
    w = wts_ref[...]
    x = xn_ref[...] + w[:, 0:1] * ya_ref[slot] + w[:, 1:2] * yb_ref[slot]
    o_ref[...] = x
    un_ref[...] = (x * lax.rsqrt(jnp.mean(x * x, axis=-1, keepdims=True) + EPS) * g_ref[...]).astype(bf16)


def _moe_combine(dest, xn, wts, ys, g_next, *, blk):
    m = xn.shape[0]
    nblk = m // blk
    return pl.pallas_call(
        functools.partial(_combine_body, blk=blk, nblk=nblk), grid=(nblk,),
        in_specs=[pl.BlockSpec((1, 2, blk), lambda i: (i, 0, 0), memory_space=pltpu.SMEM),
                  pl.BlockSpec((1, 2, blk), lambda i: (jnp.minimum(i + 1, nblk - 1), 0, 0),
                               memory_space=pltpu.SMEM),
                  pl.BlockSpec((blk, D_MODEL), lambda i: (i, 0)),
                  pl.BlockSpec((blk, LANES), lambda i: (i, 0)),
                  pl.BlockSpec(memory_space=pl.ANY),
                  pl.BlockSpec((1, D_MODEL), lambda i: (0, 0))],
        out_specs=[pl.BlockSpec((blk, D_MODEL), lambda i: (i, 0)), pl.BlockSpec((blk, D_MODEL), lambda i: (i, 0))],
        out_shape=[jax.ShapeDtypeStruct((m, D_MODEL), f32), jax.ShapeDtypeStruct((m, D_MODEL), bf16)],
        scratch_shapes=[pltpu.VMEM((2, blk, D_MODEL), f32), pltpu.VMEM((2, blk, D_MODEL), f32),
                        pltpu.SemaphoreType.DMA((2,))],
        compiler_params=_params("arbitrary"), name="moe_combine")(dest, dest, xn, wts, ys, g_next)


def _moe(u, cw, oh, cnt, xn, wg, wu, wd, l, g_next):
    m = u.shape[0]
    blk = LANES
    rows = -(-(MOE_TOP_K * m + MOE_EXPERTS * (MOE_TILE - 1)) // MOE_TILE) * MOE_TILE
    assert rows // MOE_TILE <= LANES
    dest, wts, te, pad = _moe_plan(oh, cw, cnt, blk=blk, rows=rows)
    xs = _moe_dispatch(u, dest, pad, rows=rows, blk=blk)
    ys = _moe_gmm(te, xs, wg, wu, wd, l)
    return _moe_combine(dest, xn, wts, ys, g_next.reshape(1, D_MODEL), blk=blk)


def _gla_body(q_ref, k_ref, v_ref, r_ref, sm_ref, w2_ref, bg_ref, gn_ref, s0_ref, prev_ref,
              o_ref, sout_ref, st_ref, b_ref, att_ref, *, chunk, nchunks):
    del prev_ref
    t = pl.program_id(1)

    @pl.when(t == 0)
    def _():
        st_ref[...] = s0_ref[0]

    glr = sm_ref[:, S_GLR:S_GLR + GLA_GATE_RANK].astype(bf16)
    pre = jnp.dot(glr, w2_ref[...].astype(bf16), preferred_element_type=f32) + bg_ref[...]
    b_ref[...] = _cumsum_rows(_log_sigmoid(pre) * (LOG2E / GLA_GATE_TAU))

    sub_row = lax.broadcasted_iota(jnp.int32, (GLA_SUB, GLA_SUB), 0)
    sub_col = lax.broadcasted_iota(jnp.int32, (GLA_SUB, GLA_SUB), 1)
    nt = (((1,), (1,)), ((), ()))
    tn = (((0,), (0,)), ((), ()))
    for h in range(GLA_HEADS):
        ks = slice(h * GLA_DK, (h + 1) * GLA_DK)
        vs = slice(h * GLA_DV, (h + 1) * GLA_DV)
        q = q_ref[:, ks] * (GLA_DK ** -0.5)
        k = k_ref[:, ks]
        vb = v_ref[:, vs].astype(bf16)
        b = b_ref[:, ks]
        b_last = b_ref[chunk - 1:chunk, ks]
        state = st_ref[h]
        o = jnp.dot((q * jnp.exp2(b)).astype(bf16), state.astype(bf16), preferred_element_type=f32)

        att_ref[h] = jnp.zeros((chunk, chunk), f32)
        for blk in range(chunk // GLA_SUB):
            r0 = blk * GLA_SUB
            q_i = q[r0:r0 + GLA_SUB]
            b_i = b[r0:r0 + GLA_SUB]
            if blk > 0:
                c_i = b_ref[r0 - 1:r0, ks]
                q_dec = (q_i * jnp.exp2(b_i - c_i)).astype(bf16)
                k_dec = (k[:r0] * jnp.exp2(c_i - b[:r0])).astype(bf16)
                att_ref[h, r0:r0 + GLA_SUB, 0:r0] = lax.dot_general(q_dec, k_dec, nt,
                                                                   preferred_element_type=f32)
            diag = jnp.zeros((GLA_SUB, GLA_SUB), f32)
            for j in range(GLA_SUB):
                b_j = b_ref[r0 + j:r0 + j + 1, ks]
                k_j = k_ref[r0 + j:r0 + j + 1, ks]
                a_j = jnp.sum(q_i * jnp.exp2(b_i - b_j) * k_j, axis=1, keepdims=True)
                diag = jnp.where(sub_col == j, a_j, diag)
            att_ref[h, r0:r0 + GLA_SUB, r0:r0 + GLA_SUB] = jnp.where(sub_row >= sub_col, diag, 0.0)

        o = o + jnp.dot(att_ref[h].astype(bf16), vb, preferred_element_type=f32)

        k_end = (k * jnp.exp2(b_last - b)).astype(bf16)
        upd = lax.dot_general(k_end, vb, tn, preferred_element_type=f32)
        dec_col = jnp.transpose(jnp.broadcast_to(jnp.exp2(b_last), (8, GLA_DK)))[:, 0:1]
        st_ref[h] = state * dec_col + upd

        o = o * lax.rsqrt(jnp.mean(o * o, axis=1, keepdims=True) + EPS) * gn_ref[...]
        r = r_ref[:, vs]
        o_ref[:, vs] = (o * (r * _sig(r))).astype(o_ref.dtype)

    @pl.when(t == nchunks - 1)
    def _():
        sout_ref[0] = st_ref[...]


def _gla(proj, small, w2, bg, gn, l, s0, l_s0, prev, *, row0, nseq, seqlen, chunk):
    nchunks = seqlen // chunk
    blk0 = row0 // chunk
    rows = lambda b, t: blk0 + b * nchunks + t
    qk, vw = GLA_QK, GLA_VW
    hd = (GLA_HEADS, GLA_DK, GLA_DV)
    return pl.pallas_call(
        functools.partial(_gla_body, chunk=chunk, nchunks=nchunks),
        grid=(nseq, nchunks),
        in_specs=[pl.BlockSpec((chunk, qk), lambda b, t: (rows(b, t), P_Q // qk)),
                  pl.BlockSpec((chunk, qk), lambda b, t: (rows(b, t), P_K // qk)),
                  pl.BlockSpec((chunk, vw), lambda b, t: (rows(b, t), P_V // vw)),
                  pl.BlockSpec((chunk, vw), lambda b, t: (rows(b, t), P_R // vw)),
                  pl.BlockSpec((chunk, LANES), lambda b, t: (rows(b, t), 0)),
                  pl.BlockSpec((None, GLA_GATE_RANK, qk), lambda b, t: (l, 0, 0)),
                  pl.BlockSpec((None, 1, qk), lambda b, t: (l, 0, 0)),
                  pl.BlockSpec((None, 1, GLA_DV), lambda b, t: (l, 0, 0)),
                  pl.BlockSpec((None, 1) + hd, lambda b, t: (l_s0, b, 0, 0, 0)),
                  pl.BlockSpec(memory_space=pl.ANY)],
        out_specs=[pl.BlockSpec((chunk, vw), lambda b, t: (rows(b, t), 0)),
                   pl.BlockSpec((1,) + hd, lambda b, t: (b, 0, 0, 0))],
        out_shape=[jax.ShapeDtypeStruct(prev.shape, bf16),
                   jax.ShapeDtypeStruct((nseq,) + hd, f32)],
        scratch_shapes=[pltpu.VMEM(hd, f32), pltpu.VMEM((chunk, qk), f32),
                        pltpu.VMEM((GLA_HEADS, chunk, chunk), f32)],
        input_output_aliases={9: 0},
        compiler_params=_params("arbitrary", "arbitrary"),
        name="gla")(proj, proj, proj, proj, small, w2, bg, gn, s0, prev)


def _ssd_body(z_ref, x_ref, bm_ref, cm_ref, sm_ref, wx_ref, wb_ref, wc_ref, bx_ref, bb_ref, bc_ref,
              sx_ref, sb_ref, sc_ref, dtb_ref, alog_ref, dskip_ref, nrm_ref, h0_ref, prev_ref,
              y_ref, hout_ref, cxo_ref, cbo_ref, cco_ref,
              h_ref, xpx_ref, xpb_ref, xpc_ref, xs_ref, *, chunk, nchunks):
    del prev_ref
    t = pl.program_id(1)
    hist = CONV_PAD - (SSM_CONV - 1)

    @pl.when(t == 0)
    def _():
        h_ref[...] = h0_ref[0]
        xpx_ref[hist:CONV_PAD, :] = sx_ref[0]
        xpb_ref[hist:CONV_PAD, :] = sb_ref[0]
        xpc_ref[hist:CONV_PAD, :] = sc_ref[0]

    def conv_silu(xp_ref, cur_ref, w_ref, bias_ref):
        x = cur_ref[...]
        taps = SSM_CONV - 1
        acc = bias_ref[...] + x * w_ref[taps:taps + 1, :]
        for s in range(1, SSM_CONV):
            acc = acc + pltpu.roll(x, s, axis=0) * w_ref[taps - s:taps - s + 1, :]
        xp_ref[CONV_PAD:2 * CONV_PAD, :] = x[0:CONV_PAD]
        head = bias_ref[...] + xp_ref[pl.ds(hist, CONV_PAD), :] * w_ref[0:1, :]
        for i in range(1, SSM_CONV):
            head = head + xp_ref[pl.ds(hist + i, CONV_PAD), :] * w_ref[i:i + 1, :]
        xp_ref[hist:CONV_PAD, :] = x[chunk - taps:chunk]
        acc = jnp.concatenate([head, acc[CONV_PAD:]], axis=0)
        return acc * _sig(acc)

    xs_ref[...] = conv_silu(xpx_ref, x_ref, wx_ref, bx_ref)
    bmat = conv_silu(xpb_ref, bm_ref, wb_ref, bb_ref).astype(bf16)
    cmat = conv_silu(xpc_ref, cm_ref, wc_ref, bc_ref).astype(bf16)

    @pl.when(t == nchunks - 1)
    def _():
        cxo_ref[0] = xpx_ref[hist:CONV_PAD, :]
        cbo_ref[0] = xpb_ref[hist:CONV_PAD, :]
        cco_ref[0] = xpc_ref[hist:CONV_PAD, :]

    dt = _softplus(sm_ref[...] + dtb_ref[...])
    cum = _cumsum_rows(dt * -jnp.exp(alog_ref[...]))
    last = cum[chunk - 1:chunk, :]
    w_end = jnp.exp(last - cum) * dt
    e_cum = jnp.exp(cum)
    e_last = jnp.exp(last)
    cum2 = cum * LOG2E
    adj2_t = ((cum - jnp.log(dt)) * LOG2E).T

    row = lax.broadcasted_iota(jnp.int32, (chunk, chunk), 0)
    col = lax.broadcasted_iota(jnp.int32, (chunk, chunk), 1)
    causal = row >= col
    head = lax.broadcasted_iota(jnp.int32, (LANES, SSM_GROUP_W), 0)
    chan = lax.broadcasted_iota(jnp.int32, (LANES, SSM_GROUP_W), 1)
    nt = (((1,), (1,)), ((), ()))
    tn = (((0,), (0,)), ((), ()))
    for g in range(SSM_GROUPS):
        cs = slice(g * SSM_GROUP_W, (g + 1) * SSM_GROUP_W)
        ns = slice(g * SSM_D_STATE, (g + 1) * SSM_D_STATE)
        xs = xs_ref[:, cs]
        xsb = xs.astype(bf16)
        b_g = bmat[:, ns]
        c_g = cmat[:, ns]
        widen = (head == g * SSM_HPG + chan // SSM_HEAD_DIM).astype(bf16)
        w_full = _dot_exact01(w_end, widen, 2)
        ecum_full = _dot_exact01(e_cum, widen, 2)

        cb = lax.dot_general(c_g, b_g, nt, preferred_element_type=f32)
        state = h_ref[g]
        y = lax.dot_general(c_g, state.astype(bf16), nt, preferred_element_type=f32) * ecum_full
        parts = []
        for j in range(SSM_HPG):
            hd = g * SSM_HPG + j
            seg = jnp.exp2(jnp.where(causal, cum2[:, hd:hd + 1] - adj2_t[hd:hd + 1, :], -jnp.inf))
            parts.append(jnp.dot((cb * seg).astype(bf16), xsb[:, j * SSM_HEAD_DIM:(j + 1) * SSM_HEAD_DIM],
                                 preferred_element_type=f32))
        y = y + jnp.concatenate(parts, axis=1)

        upd = lax.dot_general((xs * w_full).astype(bf16), b_g, tn, preferred_element_type=f32)
        for j in range(SSM_HPG):
            hd = g * SSM_HPG + j
            rs = slice(j * SSM_HEAD_DIM, (j + 1) * SSM_HEAD_DIM)
            h_ref[g, rs, :] = state[rs, :] * e_last[:, hd:hd + 1] + upd[rs, :]

        y = y + dskip_ref[:, cs] * xs
        z = z_ref[:, cs]
        y = y * (z * _sig(z))
        y = y * lax.rsqrt(jnp.mean(y * y, axis=1, keepdims=True) + EPS) * nrm_ref[:, cs]
        y_ref[:, cs] = y.astype(y_ref.dtype)

    @pl.when(t == nchunks - 1)
    def _():
        hout_ref[0] = h_ref[...]


def _ssd(proj, small, conv_w, conv_b, l, conv_s, l_cs, dtb, alog, dskip, nrm, h0, l_h0, prev,
         *, row0, nseq, seqlen, chunk):
    nchunks = seqlen // chunk
    blk0 = row0 // chunk
    di, bc = SSM_D_INNER, SSM_BC
    rows = lambda b, t: blk0 + b * nchunks + t
    taps, hist = SSM_CONV, SSM_CONV - 1
    xb, bb, cb = 0, di // bc, di // bc + 1
    hs = (SSM_GROUPS, SSM_GROUP_W, SSM_D_STATE)
    fix3 = lambda b, t: (l, 0, 0)
    in_specs = [
        pl.BlockSpec((chunk, di), lambda b, t: (rows(b, t), P_Z // di)),
        pl.BlockSpec((chunk, di), lambda b, t: (rows(b, t), P_X // di)),
        pl.BlockSpec((chunk, bc), lambda b, t: (rows(b, t), P_B // bc)),
        pl.BlockSpec((chunk, bc), lambda b, t: (rows(b, t), P_C // bc)),
        pl.BlockSpec((chunk, LANES), lambda b, t: (rows(b, t), 0)),
        pl.BlockSpec((None, taps, di), lambda b, t: (l, 0, xb)),
        pl.BlockSpec((None, taps, bc), lambda b, t: (l, 0, bb)),
        pl.BlockSpec((None, taps, bc), lambda b, t: (l, 0, cb)),
        pl.BlockSpec((None, 1, di), lambda b, t: (l, 0, xb)),
        pl.BlockSpec((None, 1, bc), lambda b, t: (l, 0, bb)),
        pl.BlockSpec((None, 1, bc), lambda b, t: (l, 0, cb)),
        pl.BlockSpec((None, 1, hist, di), lambda b, t: (l_cs, b, 0, xb)),
        pl.BlockSpec((None, 1, hist, bc), lambda b, t: (l_cs, b, 0, bb)),
        pl.BlockSpec((None, 1, hist, bc), lambda b, t: (l_cs, b, 0, cb)),
        pl.BlockSpec((None, 1, LANES), fix3),
        pl.BlockSpec((None, 1, LANES), fix3),
        pl.BlockSpec((None, 1, di), fix3),
        pl.BlockSpec((None, 1, di), fix3),
        pl.BlockSpec((None, 1) + hs, lambda b, t: (l_h0, b, 0, 0, 0)),
        pl.BlockSpec(memory_space=pl.ANY),
    ]
    return pl.pallas_call(
        functools.partial(_ssd_body, chunk=chunk, nchunks=nchunks),
        grid=(nseq, nchunks),
        in_specs=in_specs,
        out_specs=[pl.BlockSpec((chunk, di), lambda b, t: (rows(b, t), 0)),
                   pl.BlockSpec((1,) + hs, lambda b, t: (b, 0, 0, 0)),
                   pl.BlockSpec((1, hist, di), lambda b, t: (b, 0, 0)),
                   pl.BlockSpec((1, hist, bc), lambda b, t: (b, 0, 0)),
                   pl.BlockSpec((1, hist, bc), lambda b, t: (b, 0, 0))],
        out_shape=[jax.ShapeDtypeStruct(prev.shape, bf16),
                   jax.ShapeDtypeStruct((nseq,) + hs, f32),
                   jax.ShapeDtypeStruct((nseq, hist, di), f32),
                   jax.ShapeDtypeStruct((nseq, hist, bc), f32),
                   jax.ShapeDtypeStruct((nseq, hist, bc), f32)],
        scratch_shapes=[pltpu.VMEM(hs, f32), pltpu.VMEM((2 * CONV_PAD, di), f32),
                        pltpu.VMEM((2 * CONV_PAD, bc), f32), pltpu.VMEM((2 * CONV_PAD, bc), f32),
                        pltpu.VMEM((chunk, di), f32)],
        input_output_aliases={19: 0},
        compiler_params=_params("arbitrary", "arbitrary"),
        name="ssd")(proj, proj, proj, proj, small, conv_w, conv_w, conv_w, conv_b, conv_b, conv_b,
                    conv_s, conv_s, conv_s, dtb, alog, dskip, nrm, h0, prev)


def _pad_lanes(v):
    return jnp.pad(v, ((0, 0), (0, LANES - v.shape[-1])))[:, None, :]


def kernel(x_prompt, x_sample, state_gla, state_ssm, state_conv, norm_mix, w_in, gla_w_gate2, gla_b_gate,
           gla_norm, gla_w_out, ssm_conv_w, ssm_conv_b, ssm_dt_bias, ssm_a_log, ssm_d, ssm_norm, ssm_w_out,
           w_o, norm_ffn, router_w1, router_b1, router_w2, router_b2, exp_w_gate, exp_w_up, exp_w_down,
           norm_final):
    n_p, t_p = x_prompt.shape[:2]
    n_s, t_s = x_sample.shape[:2]
    assert n_p == 1
    rows_p, rows_s = n_p * t_p, n_s * t_s
    m = rows_p + rows_s
    tm = 832 if m % 832 == 0 else 128
    tm_out = 416 if m % 416 == 0 else 128
    gla_chunk_p, ssd_chunk_p = min(64, t_p), min(256, t_p)

    x = jnp.concatenate([x_prompt.reshape(rows_p, D_MODEL), x_sample.reshape(rows_s, D_MODEL)], axis=0)
    zeros_gla = jnp.zeros((1, n_p, GLA_HEADS, GLA_DK, GLA_DV), f32)
    zeros_ssm = jnp.zeros((1, n_p, SSM_GROUPS, SSM_GROUP_W, SSM_D_STATE), f32)
    zeros_conv = jnp.zeros((1, n_p, SSM_CONV - 1, SSM_CONV_DIM), f32)
    state_ssm_g = state_ssm.reshape(DEPTH, n_s, SSM_GROUPS, SSM_GROUP_W, SSM_D_STATE)

    bg3 = gla_b_gate[:, None, :]
    gn3 = gla_norm[:, None, :]
    conv_b3 = ssm_conv_b[:, None, :]
    dtb3 = _pad_lanes(ssm_dt_bias)
    alog3 = _pad_lanes(ssm_a_log)
    dskip3 = jnp.repeat(ssm_d, SSM_HEAD_DIM, axis=1)[:, None, :]
    nrm3 = ssm_norm[:, None, :]

    w_in_t = jnp.swapaxes(w_in, 1, 2)
    oa = jnp.zeros((m, GLA_VW), bf16)
    ob = jnp.zeros((m, SSM_D_INNER), bf16)

    outs = {k: [] for k in ("gla_p", "ssm_p", "conv_p", "gla_s", "ssm_s", "conv_s")}
    for l in range(DEPTH):
        w_small_t = jnp.concatenate(
            [w_in_t[l, W_DT:W_MG], w_in_t[l, W_GLR:W_Z],
             jnp.zeros((LANES - SSM_HEADS - GLA_GATE_RANK, D_MODEL), f32)], axis=0)

        if l == 0:
            u = _rmsnorm(x, norm_mix[l], rows=m, row0=0, tm=tm, out_dtype=bf16)
        proj = _inproj(u, w_in_t, l, tm=tm)
        small = _matmul_nt(u, w_small_t, tm=tm, name="inproj_small")

        gla_args = (proj, small, gla_w_gate2, bg3, gn3, l)
        oa, g_p = _gla(*gla_args, zeros_gla, 0, oa, row0=0, nseq=n_p, seqlen=t_p, chunk=gla_chunk_p)
        oa, g_s = _gla(*gla_args, state_gla, l, oa, row0=rows_p, nseq=n_s, seqlen=t_s, chunk=t_s)

        ssd_args = (proj, small, ssm_conv_w, conv_b3, l)
        ssd_tail = (dtb3, alog3, dskip3, nrm3)
        ob, h_p, *conv_p = _ssd(*ssd_args, zeros_conv, 0, *ssd_tail, zeros_ssm, 0, ob,
                                row0=0, nseq=n_p, seqlen=t_p, chunk=ssd_chunk_p)
        ob, h_s, *conv_s = _ssd(*ssd_args, state_conv, l, *ssd_tail, state_ssm_g, l, ob,
                                row0=rows_p, nseq=n_s, seqlen=t_s, chunk=t_s)

        merged = _merge(oa, ob, proj, gla_w_out, ssm_w_out, l, tm=tm_out, tn=512)

        wr = jnp.concatenate([router_w1[l], router_w2[l],
                              jnp.zeros((D_MODEL, LANES - MOE_GROUPS - MOE_EXPERTS), f32)], axis=1)
        br = jnp.pad(jnp.concatenate([router_b1[l], router_b2[l]]),
                     (0, LANES - MOE_GROUPS - MOE_EXPERTS)).reshape(1, LANES)
        xn, u2, cw, oh, cnt = _outproj(merged, w_o, l, x, norm_ffn[l].reshape(1, -1), wr, br, tm=tm_out)
        x, u = _moe(u2, cw, oh, cnt, xn, exp_w_gate, exp_w_up, exp_w_down, l, norm_mix[(l + 1) % DEPTH])

        outs["gla_p"].append(g_p)
        outs["gla_s"].append(g_s)
        outs["ssm_p"].append(h_p.reshape(n_p, SSM_HEADS, SSM_HEAD_DIM, SSM_D_STATE))
        outs["ssm_s"].append(h_s.reshape(n_s, SSM_HEADS, SSM_HEAD_DIM, SSM_D_STATE))
        outs["conv_p"].append(jnp.concatenate(conv_p, axis=-1))
        outs["conv_s"].append(jnp.concatenate(conv_s, axis=-1))

    tm_p = 512 if rows_p % 512 == 0 else 128
    tm_s = rows_s if rows_p % rows_s == 0 else 8
    y_p = _rmsnorm(x, norm_final, rows=rows_p, row0=0, tm=tm_p, out_dtype=f32)
    y_s = _rmsnorm(x, norm_final, rows=rows_s, row0=rows_p, tm=tm_s, out_dtype=f32)
    return (y_p.reshape(n_p, t_p, D_MODEL), y_s.reshape(n_s, t_s, D_MODEL),
            jnp.stack(outs["gla_p"]), jnp.stack(outs["ssm_p"]), jnp.stack(outs["conv_p"]),
            jnp.stack(outs["gla_s"]), jnp.stack(outs["ssm_s"]), jnp.stack(outs["conv_s"]))
```

```python
import functools
import math

import jax
import jax.numpy as jnp
from jax import lax
from jax.experimental import pallas as pl
from jax.experimental.pallas import tpu as pltpu

f32 = jnp.float32
bf16 = jnp.bfloat16
HIGHEST = lax.Precision.HIGHEST
LOG2E = math.log2(math.e)

D_MODEL = 2048
DEPTH = 2
EPS = 1e-6
GLA_HEADS = 4
GLA_DK = 256
GLA_DV = 512
GLA_QK = GLA_HEADS * GLA_DK
GLA_VW = GLA_HEADS * GLA_DV
GLA_GATE_RANK = 16
GLA_GATE_TAU = 16.0
GLA_SUB = 16
SSM_D_INNER = 4096
SSM_HEAD_DIM = 64
SSM_HEADS = 64
SSM_GROUPS = 8
SSM_HPG = 8
SSM_GROUP_W = SSM_HPG * SSM_HEAD_DIM
SSM_D_STATE = 128
SSM_BC = SSM_GROUPS * SSM_D_STATE
SSM_CONV = 4
SSM_CONV_DIM = 6144
MOE_GROUPS = 4
MOE_PER_GROUP = 4
MOE_EXPERTS = 16
MOE_D_FF = 512
MOE_TOP_K = 2
MOE_LANE0 = MOE_GROUPS
MOE_TILE = 256
LANES = 128
CONV_PAD = 8

W_GLR = 6144
W_Z = 6160
W_DT = 16400
W_MG = 16464
IN_TN = 1024
IN_TILES = 20480 // IN_TN
IN_ALIGNED_TILES = W_GLR // IN_TN
IN_MG_TILE0 = (W_DT - W_Z + W_GLR) // IN_TN
IN_SHIFT_Z = W_Z - W_GLR
IN_SHIFT_MG = W_MG - IN_MG_TILE0 * IN_TN
P_Z, P_X, P_Q, P_K, P_V, P_R, P_B, P_C, P_GA, P_GB, P_END = (
    0, 4096, 8192, 9216, 10240, 12288, 14336, 15360, 16384, 18432, 20480)
S_GLR = 64

VMEM_LIMIT = 56 * 1024 * 1024


def _sig(x):
    return 1.0 / (1.0 + jnp.exp(-x))


def _softplus(x):
    return jnp.maximum(x, 0.0) + jnp.log1p(jnp.exp(-jnp.abs(x)))


def _log_sigmoid(x):
    return jnp.minimum(x, 0.0) - jnp.log1p(jnp.exp(-jnp.abs(x)))


def _params(*sem):
    return pltpu.CompilerParams(dimension_semantics=sem, vmem_limit_bytes=VMEM_LIMIT)


def _split_bf16(x, pieces):
    out = []
    for _ in range(pieces - 1):
        p = x.astype(bf16)
        out.append(p)
        x = x - p.astype(f32)
    out.append(x.astype(bf16))
    return out


def _dot_exact01(a, b01, pieces, dims=None):
    acc = None
    for p in _split_bf16(a, pieces):
        if dims is None:
            d = jnp.dot(p, b01, preferred_element_type=f32)
        else:
            d = lax.dot_general(p, b01, dims, preferred_element_type=f32)
        acc = d if acc is None else acc + d
    return acc


def _cumsum_rows(x):
    n = x.shape[0]
    row = lax.broadcasted_iota(jnp.int32, (n, n), 0)
    col = lax.broadcasted_iota(jnp.int32, (n, n), 1)
    tri = (row >= col).astype(bf16)
    acc = None
    for p in _split_bf16(x, 3):
        d = jnp.dot(tri, p, preferred_element_type=f32)
        acc = d if acc is None else acc + d
    return acc


def _norm_body(x_ref, g_ref, o_ref):
    x = x_ref[...]
    y = x * lax.rsqrt(jnp.mean(x * x, axis=-1, keepdims=True) + EPS)
    o_ref[...] = (y * g_ref[...]).astype(o_ref.dtype)


def _rmsnorm(x, g, *, rows, row0, tm, out_dtype):
    blk0 = row0 // tm
    return pl.pallas_call(
        _norm_body, grid=(rows // tm,),
        in_specs=[pl.BlockSpec((tm, D_MODEL), lambda m: (blk0 + m, 0)),
                  pl.BlockSpec((1, D_MODEL), lambda m: (0, 0))],
        out_specs=pl.BlockSpec((tm, D_MODEL), lambda m: (m, 0)),
        out_shape=jax.ShapeDtypeStruct((rows, D_MODEL), out_dtype),
        compiler_params=_params("arbitrary"), name="rmsnorm")(x, g.reshape(1, D_MODEL))


def _inproj_body(u_ref, wa_ref, wb_ref, o_ref, wbf_ref):
    n = pl.program_id(0)

    @pl.when(pl.program_id(1) == 0)
    def _():
        def fill(shift):
            def f():
                if shift == 0:
                    wbf_ref[...] = wa_ref[...].astype(bf16)
                else:
                    wbf_ref[:IN_TN - shift, :] = wa_ref[shift:, :].astype(bf16)
                    wbf_ref[IN_TN - shift:, :] = wb_ref[:shift, :].astype(bf16)
            return f
        pl.when(n < IN_ALIGNED_TILES)(fill(0))
        pl.when((n >= IN_ALIGNED_TILES) & (n < IN_MG_TILE0))(fill(IN_SHIFT_Z))
        pl.when(n >= IN_MG_TILE0)(fill(IN_SHIFT_MG))

    o_ref[...] = lax.dot_general(u_ref[...], wbf_ref[...], (((1,), (1,)), ((), ())),
                                 preferred_element_type=f32)


def _inproj_out_tile(n):
    lead = IN_ALIGNED_TILES
    zx = (P_Q - P_Z) // IN_TN
    return jnp.where(n < lead, n + zx, jnp.where(n < lead + zx, n - lead, n))


def _inproj(u, w_in_t, l, *, tm):
    m = u.shape[0]
    return pl.pallas_call(
        _inproj_body, grid=(IN_TILES, m // tm),
        in_specs=[pl.BlockSpec((tm, D_MODEL), lambda n, i: (i, 0)),
                  pl.BlockSpec((None, IN_TN, D_MODEL), lambda n, i: (l, n, 0)),
                  pl.BlockSpec((None, LANES, D_MODEL), lambda n, i: (l, (n + 1) * (IN_TN // LANES), 0))],
        out_specs=pl.BlockSpec((tm, IN_TN), lambda n, i: (i, _inproj_out_tile(n))),
        out_shape=jax.ShapeDtypeStruct((m, P_END), f32),
        scratch_shapes=[pltpu.VMEM((IN_TN, D_MODEL), bf16)],
        compiler_params=_params("arbitrary", "arbitrary"), name="inproj")(u, w_in_t, w_in_t)


def _mm_nt_body(a_ref, w_ref, o_ref):
    o_ref[...] = lax.dot_general(a_ref[...], w_ref[...].astype(bf16), (((1,), (1,)), ((), ())),
                                 preferred_element_type=f32)


def _matmul_nt(a, w_t, *, tm, name):
    m, k = a.shape
    n = w_t.shape[0]
    return pl.pallas_call(
        _mm_nt_body, grid=(m // tm,),
        in_specs=[pl.BlockSpec((tm, k), lambda i: (i, 0)),
                  pl.BlockSpec((n, k), lambda i: (0, 0))],
        out_specs=pl.BlockSpec((tm, n), lambda i: (i, 0)),
        out_shape=jax.ShapeDtypeStruct((m, n), f32),
        compiler_params=_params("arbitrary"), name=name)(a, w_t)


def _merge_body(oa_ref, ob_ref, ga_ref, gb_ref, wa_ref, wb_ref, o_ref, wabf_ref, wbbf_ref):
    @pl.when(pl.program_id(1) == 0)
    def _():
        wabf_ref[...] = wa_ref[...].astype(bf16)
        wbbf_ref[...] = wb_ref[...].astype(bf16)

    ya = jnp.dot(oa_ref[...], wabf_ref[...], preferred_element_type=f32)
    yb = jnp.dot(ob_ref[...], wbbf_ref[...], preferred_element_type=f32)
    o_ref[...] = (_sig(ga_ref[...]) * ya + _sig(gb_ref[...]) * yb).astype(o_ref.dtype)


def _merge(oa, ob, proj, wa, wb, l, *, tm, tn):
    m = oa.shape[0]
    ga0, gb0 = P_GA // tn, P_GB // tn
    return pl.pallas_call(
        _merge_body, grid=(D_MODEL // tn, m // tm),
        in_specs=[pl.BlockSpec((tm, GLA_VW), lambda j, i: (i, 0)),
                  pl.BlockSpec((tm, SSM_D_INNER), lambda j, i: (i, 0)),
                  pl.BlockSpec((tm, tn), lambda j, i: (i, ga0 + j)),
                  pl.BlockSpec((tm, tn), lambda j, i: (i, gb0 + j)),
                  pl.BlockSpec((None, GLA_VW, tn), lambda j, i: (l, 0, j)),
                  pl.BlockSpec((None, SSM_D_INNER, tn), lambda j, i: (l, 0, j))],
        out_specs=pl.BlockSpec((tm, tn), lambda j, i: (i, j)),
        out_shape=jax.ShapeDtypeStruct((m, D_MODEL), bf16),
        scratch_shapes=[pltpu.VMEM((GLA_VW, tn), bf16), pltpu.VMEM((SSM_D_INNER, tn), bf16)],
        compiler_params=_params("arbitrary", "arbitrary"), name="merge")(oa, ob, proj, proj, wa, wb)


def _route(logits):
    c = lax.broadcasted_iota(jnp.int32, logits.shape, 1)
    big = jnp.int32(1 << 20)
    is_group = c < MOE_GROUPS
    lg = jnp.where(is_group, logits, -jnp.inf)
    top = jnp.max(lg, axis=1, keepdims=True)
    gsel = jnp.min(jnp.where(is_group & (logits == top), c, big), axis=1, keepdims=True)
    p_sel = 1.0 / jnp.sum(jnp.exp(lg - top), axis=1, keepdims=True)
    in_group = ((c >= MOE_LANE0) & (c < MOE_LANE0 + MOE_EXPERTS)
                & (((c - MOE_LANE0) // MOE_PER_GROUP) == gsel))
    le = jnp.where(in_group, logits, -jnp.inf)
    tv0 = jnp.max(le, axis=1, keepdims=True)
    i0 = jnp.min(jnp.where(in_group & (logits == tv0), c, big), axis=1, keepdims=True)
    rest = in_group & (c != i0)
    le1 = jnp.where(rest, logits, -jnp.inf)
    tv1 = jnp.max(le1, axis=1, keepdims=True)
    i1 = jnp.min(jnp.where(rest & (logits == tv1), c, big), axis=1, keepdims=True)
    t = jnp.exp(tv1 - tv0)
    w0 = p_sel / (1.0 + t)
    w1 = p_sel * t / (1.0 + t)
    cw = jnp.where(c == i0, w0, jnp.where(c == i1, w1, 0.0))
    return cw, ((c == i0) | (c == i1)).astype(f32)


def _outproj_body(m_ref, wo_ref, x_ref, g_ref, wr_ref, br_ref, xn_ref, u_ref, cw_ref, oh_ref, cnt_ref,
                  wobf_ref):
    @pl.when(pl.program_id(0) == 0)
    def _():
        wobf_ref[...] = wo_ref[...].astype(bf16)
        cnt_ref[...] = jnp.zeros_like(cnt_ref)

    w_hi, w_lo = _split_bf16(wr_ref[...], 2)
    tm = x_ref.shape[0]
    parts = 2 if tm % 32 == 0 else 1
    for p in range(parts):
        rs = slice(p * (tm // parts), (p + 1) * (tm // parts))
        xn = x_ref[rs, :] + jnp.dot(m_ref[rs, :], wobf_ref[...], preferred_element_type=f32)
        xn_ref[rs, :] = xn
        u = xn * lax.rsqrt(jnp.mean(xn * xn, axis=-1, keepdims=True) + EPS) * g_ref[...]
        u_ref[rs, :] = u
        u_hi, u_lo = _split_bf16(u, 2)
        logits = (jnp.dot(u_hi, w_hi, preferred_element_type=f32)
                  + (jnp.dot(u_hi, w_lo, preferred_element_type=f32)
                     + jnp.dot(u_lo, w_hi, preferred_element_type=f32))) + br_ref[...]
        cw, oh = _route(logits)
        cw_ref[rs, :] = cw
        oh_ref[rs, :] = oh
        cnt_ref[...] += jnp.sum(oh, axis=0, keepdims=True)


def _outproj(merged, wo, l, x, g, wr, br, *, tm):
    m = x.shape[0]
    row = lambda i: (i, 0)
    fix = lambda i: (0, 0)
    return pl.pallas_call(
        _outproj_body, grid=(m // tm,),
        in_specs=[pl.BlockSpec((tm, D_MODEL), row),
                  pl.BlockSpec((None, D_MODEL, D_MODEL), lambda i: (l, 0, 0), pipeline_mode=pl.Buffered(1)),
                  pl.BlockSpec((tm, D_MODEL), row), pl.BlockSpec((1, D_MODEL), fix),
                  pl.BlockSpec((D_MODEL, LANES), fix), pl.BlockSpec((1, LANES), fix)],
        out_specs=[pl.BlockSpec((tm, D_MODEL), row), pl.BlockSpec((tm, D_MODEL), row),
                   pl.BlockSpec((tm, LANES), row), pl.BlockSpec((tm, LANES), row), pl.BlockSpec((1, LANES), fix)],
        out_shape=[jax.ShapeDtypeStruct((m, D_MODEL), f32), jax.ShapeDtypeStruct((m, D_MODEL), f32),
                   jax.ShapeDtypeStruct((m, LANES), f32), jax.ShapeDtypeStruct((m, LANES), f32),
                   jax.ShapeDtypeStruct((1, LANES), f32)],
        scratch_shapes=[pltpu.VMEM((D_MODEL, D_MODEL), bf16)],
        compiler_params=_params("arbitrary"), name="outproj")(merged, wo, x, g, wr, br)


def _plan_body(oh_ref, cw_ref, cnt_ref, dest_ref, wts_ref, te_ref, pad_ref, base_ref, carry_ref, *, rows, blk):
    lane = lax.broadcasted_iota(jnp.int32, (8, LANES), 1)
    is_expert = (lane >= MOE_LANE0) & (lane < MOE_LANE0 + MOE_EXPERTS)

    @pl.when(pl.program_id(0) == 0)
    def _():
        cnt = jnp.broadcast_to(cnt_ref[...], (8, LANES))
        padded = jnp.floor((cnt + (MOE_TILE - 1)) * (1.0 / MOE_TILE)) * MOE_TILE
        src = lax.broadcasted_iota(jnp.int32, (LANES, LANES), 0)
        dst = lax.broadcasted_iota(jnp.int32, (LANES, LANES), 1)
        base = _dot_exact01(padded, (src < dst).astype(bf16), 3)
        base_ref[...] = base[0:1]
        carry_ref[...] = jnp.zeros_like(carry_ref)
        end = base + padded
        total = jnp.max(jnp.where(is_expert, end, 0.0), axis=1, keepdims=True)
        end_col = jnp.transpose(jnp.where(is_expert, end, jnp.inf))[:, 0:1]
        tile_start = (lax.broadcasted_iota(jnp.int32, (LANES, LANES), 1) * MOE_TILE).astype(f32)
        tile_exp = jnp.sum((end_col <= tile_start).astype(f32), axis=0, keepdims=True)
        tile_exp = jnp.minimum(tile_exp, MOE_EXPERTS - 1.0)
        te_ref[0:1, :] = tile_exp.astype(jnp.int32)
        te_ref[1:2, :] = jnp.broadcast_to(total[0:1] * (1.0 / MOE_TILE), (1, LANES)).astype(jnp.int32)
        shift = (src == dst + MOE_LANE0).astype(bf16)
        is_tail = lane[0:1] == MOE_EXPERTS
        pad_lo = jnp.where(is_tail, total[0:1], _dot_exact01(base + cnt, shift, 3)[0:1])
        pad_hi = jnp.where(is_tail, float(rows), _dot_exact01(end, shift, 3)[0:1])
        pad_ref[0:1, :] = pad_lo.astype(jnp.int32)
        pad_ref[1:2, :] = pad_hi.astype(jnp.int32)

    n = blk
    row = lax.broadcasted_iota(jnp.int32, (n, n), 0)
    col = lax.broadcasted_iota(jnp.int32, (n, n), 1)
    earlier = (row > col).astype(bf16)
    lanes = lax.broadcasted_iota(jnp.int32, (n, LANES), 1)
    for s in range(oh_ref.shape[0] // blk):
        rs = slice(s * blk, (s + 1) * blk)
        oh = oh_ref[rs, :]
        rank = jnp.dot(earlier, oh.astype(bf16), preferred_element_type=f32)
        pos = base_ref[...] + carry_ref[...] + rank
        carry_ref[...] += jnp.sum(oh, axis=0, keepdims=True)
        chosen = oh > 0.0
        d_lo = jnp.min(jnp.where(chosen, pos, jnp.inf), axis=1, keepdims=True)
        d_hi = jnp.max(jnp.where(chosen, pos, -1.0), axis=1, keepdims=True)
        cw = cw_ref[rs, :]
        w_lo = jnp.sum(jnp.where(chosen & (pos == d_lo), cw, 0.0), axis=1, keepdims=True)
        w_hi = jnp.sum(jnp.where(chosen & (pos == d_hi), cw, 0.0), axis=1, keepdims=True)
        wts_ref[rs, :] = jnp.where(lanes == 0, w_lo, jnp.where(lanes == 1, w_hi, 0.0))
        dests = jnp.where(lanes == 0, d_lo, jnp.where(lanes == 1, d_hi, 0.0))
        dest_ref[s] = jnp.transpose(dests)[0:2, :].astype(jnp.int32)


def _token_blocks_per_step(nblk):
    return max(d for d in range(1, 6) if nblk % d == 0)


def _moe_plan(oh, cw, cnt, *, blk, rows):
    m = oh.shape[0]
    nblk = m // blk
    nsub = _token_blocks_per_step(nblk)
    fix = lambda i: (0, 0)
    return pl.pallas_call(
        functools.partial(_plan_body, rows=rows, blk=blk), grid=(nblk // nsub,),
        in_specs=[pl.BlockSpec((nsub * blk, LANES), lambda i: (i, 0)),
                  pl.BlockSpec((nsub * blk, LANES), lambda i: (i, 0)),
                  pl.BlockSpec((1, LANES), fix)],
        out_specs=[pl.BlockSpec((nsub, 2, blk), lambda i: (i, 0, 0)),
                   pl.BlockSpec((nsub * blk, LANES), lambda i: (i, 0)),
                   pl.BlockSpec((2, LANES), fix), pl.BlockSpec((2, LANES), fix)],
        out_shape=[jax.ShapeDtypeStruct((nblk, 2, blk), jnp.int32), jax.ShapeDtypeStruct((m, LANES), f32),
                   jax.ShapeDtypeStruct((2, LANES), jnp.int32), jax.ShapeDtypeStruct((2, LANES), jnp.int32)],
        scratch_shapes=[pltpu.VMEM((1, LANES), f32), pltpu.VMEM((1, LANES), f32)],
        compiler_params=_params("arbitrary"), name="moe_plan")(oh, cw, cnt)


def _row_copy(src_ref, s, dst_ref, d, sem):
    return pltpu.make_async_copy(src_ref.at[pl.ds(s, 1), :], dst_ref.at[pl.ds(d, 1), :], sem)


def _dispatch_body(dest_ref, pad_ref, u_ref, xs_ref, zero_ref, sem, *, blk, nsub):
    i = pl.program_id(0)

    def wait_rows(count, which):
        pltpu.make_async_copy(xs_ref.at[pl.ds(0, count), :], xs_ref.at[pl.ds(0, count), :], sem.at[which]).wait()

    @pl.when(i == 0)
    def _():
        zero_ref[...] = jnp.zeros_like(zero_ref)
        group = zero_ref.shape[0]

        def fill_row(r, c):
            _row_copy(zero_ref, 0, xs_ref, r, sem.at[1]).start()
            return c

        def fill_group(lo8):
            def f(r, c):
                row0 = pl.multiple_of(lo8 + r * group, group)
                pltpu.make_async_copy(zero_ref, xs_ref.at[pl.ds(row0, group), :], sem.at[1]).start()
                return c
            return f

        nrow = jnp.int32(0)
        ngroup = jnp.int32(0)
        for e in range(MOE_EXPERTS + 1):
            lo = pad_ref[0, e]
            hi = pad_ref[1, e]
            lo8 = jnp.minimum((lo + group - 1) // group * group, hi)
            lax.fori_loop(lo, lo8, fill_row, 0)
            lax.fori_loop(0, (hi - lo8) // group, fill_group(lo8), 0)
            nrow = nrow + (lo8 - lo)
            ngroup = ngroup + (hi - lo8) // group

        def wait_row(r, c):
            _row_copy(zero_ref, 0, xs_ref, 0, sem.at[1]).wait()
            return c
        lax.fori_loop(0, nrow, wait_row, 0)

        def wait_group(r, c):
            wait_rows(group, 1)
            return c
        lax.fori_loop(0, ngroup, wait_group, 0)

    for s in range(nsub):
        def send(j, c, s=s):
            _row_copy(u_ref, s * blk + j, xs_ref, dest_ref[s, 0, j], sem.at[0]).start()
            _row_copy(u_ref, s * blk + j, xs_ref, dest_ref[s, 1, j], sem.at[0]).start()
            return c
        lax.fori_loop(0, blk, send, 0)
    for _ in range(MOE_TOP_K * nsub):
        wait_rows(blk, 0)


def _moe_dispatch(u, dest, pad, *, rows, blk):
    m = u.shape[0]
    nblk = m // blk
    nsub = _token_blocks_per_step(nblk)
    return pl.pallas_call(
        functools.partial(_dispatch_body, blk=blk, nsub=nsub), grid=(nblk // nsub,),
        in_specs=[pl.BlockSpec((nsub, 2, blk), lambda i: (i, 0, 0), memory_space=pltpu.SMEM),
                  pl.BlockSpec(memory_space=pltpu.SMEM),
                  pl.BlockSpec((nsub * blk, D_MODEL), lambda i: (i, 0))],
        out_specs=pl.BlockSpec(memory_space=pl.ANY),
        out_shape=jax.ShapeDtypeStruct((rows, D_MODEL), f32),
        scratch_shapes=[pltpu.VMEM((8, D_MODEL), f32), pltpu.SemaphoreType.DMA((2,))],
        compiler_params=_params("arbitrary"), name="moe_dispatch")(dest, pad, u)


def _gmm_body(te_ref, xs_ref, wg_ref, wu_ref, wd_ref, ys_ref, wgb_ref, wub_ref, wdb_ref):
    i = pl.program_id(0)
    n_used = te_ref[1, 0]

    @pl.when(i < n_used)
    def _():
        @pl.when((i == 0) | (te_ref[0, i] != te_ref[0, jnp.maximum(i - 1, 0)]))
        def _():
            wgb_ref[...] = wg_ref[...].astype(bf16)
            wub_ref[...] = wu_ref[...].astype(bf16)
            wdb_ref[...] = wd_ref[...].astype(bf16)

        x = xs_ref[...].astype(bf16)
        hg = jnp.dot(x, wgb_ref[...], preferred_element_type=f32)
        hu = jnp.dot(x, wub_ref[...], preferred_element_type=f32)
        h = ((hg * _sig(hg)) * hu).astype(bf16)
        ys_ref[...] = jnp.dot(h, wdb_ref[...], preferred_element_type=f32)

    @pl.when(i >= n_used)
    def _():
        ys_ref[...] = jnp.zeros_like(ys_ref)


def _moe_gmm(te, xs, wg, wu, wd, l):
    rows = xs.shape[0]
    ntiles = rows // MOE_TILE
    tile = lambda i, te: (jnp.minimum(i, te[1, 0] - 1), 0)
    up = lambda i, te: (l, te[0, i], 0, 0)
    return pl.pallas_call(
        _gmm_body,
        grid_spec=pltpu.PrefetchScalarGridSpec(
            num_scalar_prefetch=1, grid=(ntiles,),
            in_specs=[pl.BlockSpec((MOE_TILE, D_MODEL), tile),
                      pl.BlockSpec((None, None, D_MODEL, MOE_D_FF), up),
                      pl.BlockSpec((None, None, D_MODEL, MOE_D_FF), up),
                      pl.BlockSpec((None, None, MOE_D_FF, D_MODEL), up)],
            out_specs=pl.BlockSpec((MOE_TILE, D_MODEL), lambda i, te: (i, 0)),
            scratch_shapes=[pltpu.VMEM((D_MODEL, MOE_D_FF), bf16), pltpu.VMEM((D_MODEL, MOE_D_FF), bf16),
                            pltpu.VMEM((MOE_D_FF, D_MODEL), bf16)]),
        out_shape=jax.ShapeDtypeStruct((rows, D_MODEL), f32),
        compiler_params=_params("arbitrary"), name="moe_gmm")(te, xs, wg, wu, wd)


def _combine_body(dcur_ref, dnext_ref, xn_ref, wts_ref, ys_ref, g_ref, o_ref, un_ref, ya_ref, yb_ref, sem,
                  *, blk, nblk):
    i = pl.program_id(0)
    slot = i % 2

    def gather(dref, s):
        def one(j, c):
            _row_copy(ys_ref, dref[0, 0, j], ya_ref.at[s], j, sem.at[s]).start()
            _row_copy(ys_ref, dref[0, 1, j], yb_ref.at[s], j, sem.at[s]).start()
            return c
        lax.fori_loop(0, blk, one, 0)

    @pl.when(i == 0)
    def _():
        gather(dcur_ref, 0)

    @pl.when(i + 1 < nblk)
    def _():
        gather(dnext_ref, 1 - slot)

    pltpu.make_async_copy(ys_ref.at[pl.ds(0, blk), :], ya_ref.at[slot], sem.at[slot]).wait()
    pltpu.make_async_copy(ys_ref.at[pl.ds(0, blk), :], yb_ref.at[slot], sem.at[slot]).wait()
    w = wts_ref[...]
    x = xn_ref[...] + w[:, 0:1] * ya_ref[slot] + w[:, 1:2] * yb_ref[slot]
    o_ref[...] = x
    un_ref[...] = (x * lax.rsqrt(jnp.mean(x * x, axis=-1, keepdims=True) + EPS) * g_ref[...]).astype(bf16)


def _moe_combine(dest, xn, wts, ys, g_next, *, blk):
    m = xn.shape[0]
    nblk = m // blk
    return pl.pallas_call(
        functools.partial(_combine_body, blk=blk, nblk=nblk), grid=(nblk,),
        in_specs=[pl.BlockSpec((1, 2, blk), lambda i: (i, 0, 0), memory_space=pltpu.SMEM),
                  pl.BlockSpec((1, 2, blk), lambda i: (jnp.minimum(i + 1, nblk - 1), 0, 0),
                               memory_space=pltpu.SMEM),
                  pl.BlockSpec((blk, D_MODEL), lambda i: (i, 0)),
                  pl.BlockSpec((blk, LANES), lambda i: (i, 0)),
                  pl.BlockSpec(memory_space=pl.ANY),
                  pl.BlockSpec((1, D_MODEL), lambda i: (0, 0))],
        out_specs=[pl.BlockSpec((blk, D_MODEL), lambda i: (i, 0)), pl.BlockSpec((blk, D_MODEL), lambda i: (i, 0))],
        out_shape=[jax.ShapeDtypeStruct((m, D_MODEL), f32), jax.ShapeDtypeStruct((m, D_MODEL), bf16)],
        scratch_shapes=[pltpu.VMEM((2, blk, D_MODEL), f32), pltpu.VMEM((2, blk, D_MODEL), f32),
                        pltpu.SemaphoreType.DMA((2,))],
        compiler_params=_params("arbitrary"), name="moe_combine")(dest, dest, xn, wts, ys, g_next)


def _moe(u, cw, oh, cnt, xn, wg, wu, wd, l, g_next):
    m = u.shape[0]
    blk = LANES
    rows = -(-(MOE_TOP_K * m + MOE_EXPERTS * (MOE_TILE - 1)) // MOE_TILE) * MOE_TILE
    assert rows // MOE_TILE <= LANES
    dest, wts, te, pad = _moe_plan(oh, cw, cnt, blk=blk, rows=rows)
    xs = _moe_dispatch(u, dest, pad, rows=rows, blk=blk)
    ys = _moe_gmm(te, xs, wg, wu, wd, l)
    return _moe_combine(dest, xn, wts, ys, g_next.reshape(1, D_MODEL), blk=blk)


def _gla_body(q_ref, k_ref, v_ref, r_ref, sm_ref, w2_ref, bg_ref, gn_ref, s0_ref, prev_ref,
              o_ref, sout_ref, st_ref, b_ref, att_ref, *, chunk, nchunks):
    del prev_ref
    t = pl.program_id(1)

    @pl.when(t == 0)
    def _():
        st_ref[...] = s0_ref[0]

    glr = sm_ref[:, S_GLR:S_GLR + GLA_GATE_RANK].astype(bf16)
    pre = jnp.dot(glr, w2_ref[...].astype(bf16), preferred_element_type=f32) + bg_ref[...]
    b_ref[...] = _cumsum_rows(_log_sigmoid(pre) * (1.0 / GLA_GATE_TAU))

    sub_row = lax.broadcasted_iota(jnp.int32, (GLA_SUB, GLA_DK), 0)
    sub_col = lax.broadcasted_iota(jnp.int32, (GLA_SUB, GLA_SUB), 1)
    nt = (((1,), (1,)), ((), ()))
    tn = (((0,), (0,)), ((), ()))
    for h in range(GLA_HEADS):
        ks = slice(h * GLA_DK, (h + 1) * GLA_DK)
        vs = slice(h * GLA_DV, (h + 1) * GLA_DV)
        q = q_ref[:, ks] * (GLA_DK ** -0.5)
        k = k_ref[:, ks]
        vb = v_ref[:, vs].astype(bf16)
        b = b_ref[:, ks]
        b_last = b_ref[chunk - 1:chunk, ks]
        state = st_ref[h]
        o = jnp.dot((q * jnp.exp(b)).astype(bf16), state.astype(bf16), preferred_element_type=f32)

        att_ref[h] = jnp.zeros((chunk, chunk), f32)
        for blk in range(chunk // GLA_SUB):
            r0 = blk * GLA_SUB
            q_i = q[r0:r0 + GLA_SUB]
            b_i = b[r0:r0 + GLA_SUB]
            if blk > 0:
                c_i = b_ref[r0 - 1:r0, ks]
                q_dec = (q_i * jnp.exp(b_i - c_i)).astype(bf16)
                k_dec = (k[:r0] * jnp.exp(c_i - b[:r0])).astype(bf16)
                att_ref[h, r0:r0 + GLA_SUB, 0:r0] = lax.dot_general(q_dec, k_dec, nt,
                                                                   preferred_element_type=f32)
            diag = jnp.zeros((GLA_SUB, GLA_SUB), f32)
            for j in range(GLA_SUB):
                b_j = b_ref[r0 + j:r0 + j + 1, ks]
                k_j = k_ref[r0 + j:r0 + j + 1, ks]
                decay = jnp.exp(jnp.where(sub_row >= j, b_i - b_j, -jnp.inf))
                a_j = jnp.sum(q_i * decay * k_j, axis=1, keepdims=True)
                diag = jnp.where(sub_col == j, a_j, diag)
            att_ref[h, r0:r0 + GLA_SUB, r0:r0 + GLA_SUB] = diag

        o = o + jnp.dot(att_ref[h].astype(bf16), vb, preferred_element_type=f32)

        k_end = (k * jnp.exp(b_last - b)).astype(bf16)
        upd = lax.dot_general(k_end, vb, tn, preferred_element_type=f32)
        dec_col = jnp.transpose(jnp.broadcast_to(jnp.exp(b_last), (8, GLA_DK)))[:, 0:1]
        st_ref[h] = state * dec_col + upd

        o = o * lax.rsqrt(jnp.mean(o * o, axis=1, keepdims=True) + EPS) * gn_ref[...]
        r = r_ref[:, vs]
        o_ref[:, vs] = (o * (r * _sig(r))).astype(o_ref.dtype)

    @pl.when(t == nchunks - 1)
    def _():
        sout_ref[0] = st_ref[...]


def _gla(proj, small, w2, bg, gn, l, s0, l_s0, prev, *, row0, nseq, seqlen, chunk):
    nchunks = seqlen // chunk
    blk0 = row0 // chunk
    rows = lambda b, t: blk0 + b * nchunks + t
    qk, vw = GLA_QK, GLA_VW
    hd = (GLA_HEADS, GLA_DK, GLA_DV)
    return pl.pallas_call(
        functools.partial(_gla_body, chunk=chunk, nchunks=nchunks),
        grid=(nseq, nchunks),
        in_specs=[pl.BlockSpec((chunk, qk), lambda b, t: (rows(b, t), P_Q // qk)),
                  pl.BlockSpec((chunk, qk), lambda b, t: (rows(b, t), P_K // qk)),
                  pl.BlockSpec((chunk, vw), lambda b, t: (rows(b, t), P_V // vw)),
                  pl.BlockSpec((chunk, vw), lambda b, t: (rows(b, t), P_R // vw)),
                  pl.BlockSpec((chunk, LANES), lambda b, t: (rows(b, t), 0)),
                  pl.BlockSpec((None, GLA_GATE_RANK, qk), lambda b, t: (l, 0, 0)),
                  pl.BlockSpec((None, 1, qk), lambda b, t: (l, 0, 0)),
                  pl.BlockSpec((None, 1, GLA_DV), lambda b, t: (l, 0, 0)),
                  pl.BlockSpec((None, 1) + hd, lambda b, t: (l_s0, b, 0, 0, 0)),
                  pl.BlockSpec(memory_space=pl.ANY)],
        out_specs=[pl.BlockSpec((chunk, vw), lambda b, t: (rows(b, t), 0)),
                   pl.BlockSpec((1,) + hd, lambda b, t: (b, 0, 0, 0))],
        out_shape=[jax.ShapeDtypeStruct(prev.shape, bf16),
                   jax.ShapeDtypeStruct((nseq,) + hd, f32)],
        scratch_shapes=[pltpu.VMEM(hd, f32), pltpu.VMEM((chunk, qk), f32),
                        pltpu.VMEM((GLA_HEADS, chunk, chunk), f32)],
        input_output_aliases={9: 0},
        compiler_params=_params("arbitrary", "arbitrary"),
        name="gla")(proj, proj, proj, proj, small, w2, bg, gn, s0, prev)


def _ssd_body(z_ref, x_ref, bm_ref, cm_ref, sm_ref, wx_ref, wb_ref, wc_ref, bx_ref, bb_ref, bc_ref,
              sx_ref, sb_ref, sc_ref, dtb_ref, alog_ref, dskip_ref, nrm_ref, h0_ref, prev_ref,
              y_ref, hout_ref, cxo_ref, cbo_ref, cco_ref,
              h_ref, xpx_ref, xpb_ref, xpc_ref, xs_ref, *, chunk, nchunks):
    del prev_ref
    t = pl.program_id(1)
    hist = CONV_PAD - (SSM_CONV - 1)

    @pl.when(t == 0)
    def _():
        h_ref[...] = h0_ref[0]
        xpx_ref[hist:CONV_PAD, :] = sx_ref[0]
        xpb_ref[hist:CONV_PAD, :] = sb_ref[0]
        xpc_ref[hist:CONV_PAD, :] = sc_ref[0]

    def conv_silu(xp_ref, cur_ref, w_ref, bias_ref):
        x = cur_ref[...]
        taps = SSM_CONV - 1
        acc = bias_ref[...] + x * w_ref[taps:taps + 1, :]
        for s in range(1, SSM_CONV):
            acc = acc + pltpu.roll(x, s, axis=0) * w_ref[taps - s:taps - s + 1, :]
        xp_ref[CONV_PAD:2 * CONV_PAD, :] = x[0:CONV_PAD]
        head = bias_ref[...] + xp_ref[pl.ds(hist, CONV_PAD), :] * w_ref[0:1, :]
        for i in range(1, SSM_CONV):
            head = head + xp_ref[pl.ds(hist + i, CONV_PAD), :] * w_ref[i:i + 1, :]
        xp_ref[hist:CONV_PAD, :] = x[chunk - taps:chunk]
        acc = jnp.concatenate([head, acc[CONV_PAD:]], axis=0)
        return acc * _sig(acc)

    xs_ref[...] = conv_silu(xpx_ref, x_ref, wx_ref, bx_ref)
    bmat = conv_silu(xpb_ref, bm_ref, wb_ref, bb_ref).astype(bf16)
    cmat = conv_silu(xpc_ref, cm_ref, wc_ref, bc_ref).astype(bf16)

    @pl.when(t == nchunks - 1)
    def _():
        cxo_ref[0] = xpx_ref[hist:CONV_PAD, :]
        cbo_ref[0] = xpb_ref[hist:CONV_PAD, :]
        cco_ref[0] = xpc_ref[hist:CONV_PAD, :]

    dt = _softplus(sm_ref[...] + dtb_ref[...])
    cum = _cumsum_rows(dt * -jnp.exp(alog_ref[...]))
    last = cum[chunk - 1:chunk, :]
    w_end = jnp.exp(last - cum) * dt
    e_cum = jnp.exp(cum)
    e_last = jnp.exp(last)
    cum2 = cum * LOG2E
    adj2_t = ((cum - jnp.log(dt)) * LOG2E).T

    row = lax.broadcasted_iota(jnp.int32, (chunk, chunk), 0)
    col = lax.broadcasted_iota(jnp.int32, (chunk, chunk), 1)
    causal = row >= col
    head = lax.broadcasted_iota(jnp.int32, (LANES, SSM_GROUP_W), 0)
    chan = lax.broadcasted_iota(jnp.int32, (LANES, SSM_GROUP_W), 1)
    nt = (((1,), (1,)), ((), ()))
    tn = (((0,), (0,)), ((), ()))
    for g in range(SSM_GROUPS):
        cs = slice(g * SSM_GROUP_W, (g + 1) * SSM_GROUP_W)
        ns = slice(g * SSM_D_STATE, (g + 1) * SSM_D_STATE)
        xs = xs_ref[:, cs]
        xsb = xs.astype(bf16)
        b_g = bmat[:, ns]
        c_g = cmat[:, ns]
        widen = (head == g * SSM_HPG + chan // SSM_HEAD_DIM).astype(bf16)
        w_full = _dot_exact01(w_end, widen, 2)
        ecum_full = _dot_exact01(e_cum, widen, 2)

        cb = lax.dot_general(c_g, b_g, nt, preferred_element_type=f32)
        state = h_ref[g]
        y = lax.dot_general(c_g, state.astype(bf16), nt, preferred_element_type=f32) * ecum_full
        parts = []
        for j in range(SSM_HPG):
            hd = g * SSM_HPG + j
            seg = jnp.exp2(jnp.where(causal, cum2[:, hd:hd + 1] - adj2_t[hd:hd + 1, :], -jnp.inf))
            parts.append(jnp.dot((cb * seg).astype(bf16), xsb[:, j * SSM_HEAD_DIM:(j + 1) * SSM_HEAD_DIM],
                                 preferred_element_type=f32))
        y = y + jnp.concatenate(parts, axis=1)

        upd = lax.dot_general((xs * w_full).astype(bf16), b_g, tn, preferred_element_type=f32)
        for j in range(SSM_HPG):
            hd = g * SSM_HPG + j
            rs = slice(j * SSM_HEAD_DIM, (j + 1) * SSM_HEAD_DIM)
            h_ref[g, rs, :] = state[rs, :] * e_last[:, hd:hd + 1] + upd[rs, :]

        y = y + dskip_ref[:, cs] * xs
        z = z_ref[:, cs]
        y = y * (z * _sig(z))
        y = y * lax.rsqrt(jnp.mean(y * y, axis=1, keepdims=True) + EPS) * nrm_ref[:, cs]
        y_ref[:, cs] = y.astype(y_ref.dtype)

    @pl.when(t == nchunks - 1)
    def _():
        hout_ref[0] = h_ref[...]


def _ssd(proj, small, conv_w, conv_b, l, conv_s, l_cs, dtb, alog, dskip, nrm, h0, l_h0, prev,
         *, row0, nseq, seqlen, chunk):
    nchunks = seqlen // chunk
    blk0 = row0 // chunk
    di, bc = SSM_D_INNER, SSM_BC
    rows = lambda b, t: blk0 + b * nchunks + t
    taps, hist = SSM_CONV, SSM_CONV - 1
    xb, bb, cb = 0, di // bc, di // bc + 1
    hs = (SSM_GROUPS, SSM_GROUP_W, SSM_D_STATE)
    fix3 = lambda b, t: (l, 0, 0)
    in_specs = [
        pl.BlockSpec((chunk, di), lambda b, t: (rows(b, t), P_Z // di)),
        pl.BlockSpec((chunk, di), lambda b, t: (rows(b, t), P_X // di)),
        pl.BlockSpec((chunk, bc), lambda b, t: (rows(b, t), P_B // bc)),
        pl.BlockSpec((chunk, bc), lambda b, t: (rows(b, t), P_C // bc)),
        pl.BlockSpec((chunk, LANES), lambda b, t: (rows(b, t), 0)),
        pl.BlockSpec((None, taps, di), lambda b, t: (l, 0, xb)),
        pl.BlockSpec((None, taps, bc), lambda b, t: (l, 0, bb)),
        pl.BlockSpec((None, taps, bc), lambda b, t: (l, 0, cb)),
        pl.BlockSpec((None, 1, di), lambda b, t: (l, 0, xb)),
        pl.BlockSpec((None, 1, bc), lambda b, t: (l, 0, bb)),
        pl.BlockSpec((None, 1, bc), lambda b, t: (l, 0, cb)),
        pl.BlockSpec((None, 1, hist, di), lambda b, t: (l_cs, b, 0, xb)),
        pl.BlockSpec((None, 1, hist, bc), lambda b, t: (l_cs, b, 0, bb)),
        pl.BlockSpec((None, 1, hist, bc), lambda b, t: (l_cs, b, 0, cb)),
        pl.BlockSpec((None, 1, LANES), fix3),
        pl.BlockSpec((None, 1, LANES), fix3),
        pl.BlockSpec((None, 1, di), fix3),
        pl.BlockSpec((None, 1, di), fix3),
        pl.BlockSpec((None, 1) + hs, lambda b, t: (l_h0, b, 0, 0, 0)),
        pl.BlockSpec(memory_space=pl.ANY),
    ]
    return pl.pallas_call(
        functools.partial(_ssd_body, chunk=chunk, nchunks=nchunks),
        grid=(nseq, nchunks),
        in_specs=in_specs,
        out_specs=[pl.BlockSpec((chunk, di), lambda b, t: (rows(b, t), 0)),
                   pl.BlockSpec((1,) + hs, lambda b, t: (b, 0, 0, 0)),
                   pl.BlockSpec((1, hist, di), lambda b, t: (b, 0, 0)),
                   pl.BlockSpec((1, hist, bc), lambda b, t: (b, 0, 0)),
                   pl.BlockSpec((1, hist, bc), lambda b, t: (b, 0, 0))],
        out_shape=[jax.ShapeDtypeStruct(prev.shape, bf16),
                   jax.ShapeDtypeStruct((nseq,) + hs, f32),
                   jax.ShapeDtypeStruct((nseq, hist, di), f32),
                   jax.ShapeDtypeStruct((nseq, hist, bc), f32),
                   jax.ShapeDtypeStruct((nseq, hist, bc), f32)],
        scratch_shapes=[pltpu.VMEM(hs, f32), pltpu.VMEM((2 * CONV_PAD, di), f32),
                        pltpu.VMEM((2 * CONV_PAD, bc), f32), pltpu.VMEM((2 * CONV_PAD, bc), f32),
                        pltpu.VMEM((chunk, di), f32)],
        input_output_aliases={19: 0},
        compiler_params=_params("arbitrary", "arbitrary"),
        name="ssd")(proj, proj, proj, proj, small, conv_w, conv_w, conv_w, conv_b, conv_b, conv_b,
                    conv_s, conv_s, conv_s, dtb, alog, dskip, nrm, h0, prev)


def _pad_lanes(v):
    return jnp.pad(v, ((0, 0), (0, LANES - v.shape[-1])))[:, None, :]


def kernel(x_prompt, x_sample, state_gla, state_ssm, state_conv, norm_mix, w_in, gla_w_gate2, gla_b_gate,
           gla_norm, gla_w_out, ssm_conv_w, ssm_conv_b, ssm_dt_bias, ssm_a_log, ssm_d, ssm_norm, ssm_w_out,
           w_o, norm_ffn, router_w1, router_b1, router_w2, router_b2, exp_w_gate, exp_w_up, exp_w_down,
           norm_final):
    n_p, t_p = x_prompt.shape[:2]
    n_s, t_s = x_sample.shape[:2]
    assert n_p == 1
    rows_p, rows_s = n_p * t_p, n_s * t_s
    m = rows_p + rows_s
    tm = 832 if m % 832 == 0 else 128
    tm_out = 416 if m % 416 == 0 else 128
    gla_chunk_p, ssd_chunk_p = min(64, t_p), min(256, t_p)

    x = jnp.concatenate([x_prompt.reshape(rows_p, D_MODEL), x_sample.reshape(rows_s, D_MODEL)], axis=0)
    zeros_gla = jnp.zeros((1, n_p, GLA_HEADS, GLA_DK, GLA_DV), f32)
    zeros_ssm = jnp.zeros((1, n_p, SSM_GROUPS, SSM_GROUP_W, SSM_D_STATE), f32)
    zeros_conv = jnp.zeros((1, n_p, SSM_CONV - 1, SSM_CONV_DIM), f32)
    state_ssm_g = state_ssm.reshape(DEPTH, n_s, SSM_GROUPS, SSM_GROUP_W, SSM_D_STATE)

    bg3 = gla_b_gate[:, None, :]
    gn3 = gla_norm[:, None, :]
    conv_b3 = ssm_conv_b[:, None, :]
    dtb3 = _pad_lanes(ssm_dt_bias)
    alog3 = _pad_lanes(ssm_a_log)
    dskip3 = jnp.repeat(ssm_d, SSM_HEAD_DIM, axis=1)[:, None, :]
    nrm3 = ssm_norm[:, None, :]

    w_in_t = jnp.swapaxes(w_in, 1, 2)
    oa = jnp.zeros((m, GLA_VW), bf16)
    ob = jnp.zeros((m, SSM_D_INNER), bf16)

    outs = {k: [] for k in ("gla_p", "ssm_p", "conv_p", "gla_s", "ssm_s", "conv_s")}
    for l in range(DEPTH):
        w_small_t = jnp.concatenate(
            [w_in_t[l, W_DT:W_MG], w_in_t[l, W_GLR:W_Z],
             jnp.zeros((LANES - SSM_HEADS - GLA_GATE_RANK, D_MODEL), f32)], axis=0)

        if l == 0:
            u = _rmsnorm(x, norm_mix[l], rows=m, row0=0, tm=tm, out_dtype=bf16)
        proj = _inproj(u, w_in_t, l, tm=tm)
        small = _matmul_nt(u, w_small_t, tm=tm, name="inproj_small")

        gla_args = (proj, small, gla_w_gate2, bg3, gn3, l)
        oa, g_p = _gla(*gla_args, zeros_gla, 0, oa, row0=0, nseq=n_p, seqlen=t_p, chunk=gla_chunk_p)
        oa, g_s = _gla(*gla_args, state_gla, l, oa, row0=rows_p, nseq=n_s, seqlen=t_s, chunk=t_s)

        ssd_args = (proj, small, ssm_conv_w, conv_b3, l)
        ssd_tail = (dtb3, alog3, dskip3, nrm3)
        ob, h_p, *conv_p = _ssd(*ssd_args, zeros_conv, 0, *ssd_tail, zeros_ssm, 0, ob,
                                row0=0, nseq=n_p, seqlen=t_p, chunk=ssd_chunk_p)
        ob, h_s, *conv_s = _ssd(*ssd_args, state_conv, l, *ssd_tail, state_ssm_g, l, ob,
                                row0=rows_p, nseq=n_s, seqlen=t_s, chunk=t_s)

        merged = _merge(oa, ob, proj, gla_w_out, ssm_w_out, l, tm=tm_out, tn=512)

        wr = jnp.concatenate([router_w1[l], router_w2[l],
                              jnp.zeros((D_MODEL, LANES - MOE_GROUPS - MOE_EXPERTS), f32)], axis=1)
        br = jnp.pad(jnp.concatenate([router_b1[l], router_b2[l]]),
                     (0, LANES - MOE_GROUPS - MOE_EXPERTS)).reshape(1, LANES)
        xn, u2, cw, oh, cnt = _outproj(merged, w_o, l, x, norm_ffn[l].reshape(1, -1), wr, br, tm=tm_out)
        x, u = _moe(u2, cw, oh, cnt, xn, exp_w_gate, exp_w_up, exp_w_down, l, norm_mix[(l + 1) % DEPTH])

        outs["gla_p"].append(g_p)
        outs["gla_s"].append(g_s)
        outs["ssm_p"].append(h_p.reshape(n_p, SSM_HEADS, SSM_HEAD_DIM, SSM_D_STATE))
        outs["ssm_s"].append(h_s.reshape(n_s, SSM_HEADS, SSM_HEAD_DIM, SSM_D_STATE))
        outs["conv_p"].append(jnp.concatenate(conv_p, axis=-1))
        outs["conv_s"].append(jnp.concatenate(conv_s, axis=-1))

    tm_p = 512 if rows_p % 512 == 0 else 128
    tm_s = rows_s if rows_p % rows_s == 0 else 8
    y_p = _rmsnorm(x, norm_final, rows=rows_p, row0=0, tm=tm_p, out_dtype=f32)
    y_s = _rmsnorm(x, norm_final, rows=rows_s, row0=rows_p, tm=tm_s, out_dtype=f32)
    return (y_p.reshape(n_p, t_p, D_MODEL), y_s.reshape(n_s, t_s, D_MODEL),
            jnp.stack(outs["gla_p"]), jnp.stack(outs["ssm_p"]), jnp.stack(outs["conv_p"]),
            jnp.stack(outs["gla_s"]), jnp.stack(outs["ssm_s"]), jnp.stack(outs["conv_s"]))
```

```python
import functools
import math

import jax
import jax.numpy as jnp
from jax import lax
from jax.experimental import pallas as pl
from jax.experimental.pallas import tpu as pltpu

f32 = jnp.float32
bf16 = jnp.bfloat16
HIGHEST = lax.Precision.HIGHEST
LOG2E = math.log2(math.e)

D_MODEL = 2048
DEPTH = 2
EPS = 1e-6
GLA_HEADS = 4
GLA_DK = 256
GLA_DV = 512
GLA_QK = GLA_HEADS * GLA_DK
GLA_VW = GLA_HEADS * GLA_DV
GLA_GATE_RANK = 16
GLA_GATE_TAU = 16.0
GLA_SUB = 16
SSM_D_INNER = 4096
SSM_HEAD_DIM = 64
SSM_HEADS = 64
SSM_GROUPS = 8
SSM_HPG = 8
SSM_GROUP_W = SSM_HPG * SSM_HEAD_DIM
SSM_D_STATE = 128
SSM_BC = SSM_GROUPS * SSM_D_STATE
SSM_CONV = 4
SSM_CONV_DIM = 6144
MOE_GROUPS = 4
MOE_PER_GROUP = 4
MOE_EXPERTS = 16
MOE_D_FF = 512
MOE_TOP_K = 2
MOE_LANE0 = MOE_GROUPS
MOE_TILE = 256
LANES = 128
CONV_PAD = 8

W_GLR = 6144
W_Z = 6160
W_DT = 16400
W_MG = 16464
IN_TN = 1024
IN_TILES = 20480 // IN_TN
IN_ALIGNED_TILES = W_GLR // IN_TN
IN_MG_TILE0 = (W_DT - W_Z + W_GLR) // IN_TN
IN_SHIFT_Z = W_Z - W_GLR
IN_SHIFT_MG = W_MG - IN_MG_TILE0 * IN_TN
P_Z, P_X, P_Q, P_K, P_V, P_R, P_B, P_C, P_GA, P_GB, P_END = (
    0, 4096, 8192, 9216, 10240, 12288, 14336, 15360, 16384, 18432, 20480)
S_GLR = 64

VMEM_LIMIT = 56 * 1024 * 1024


def _sig(x):
    return 1.0 / (1.0 + jnp.exp(-x))


def _softplus(x):
    return jnp.maximum(x, 0.0) + jnp.log1p(jnp.exp(-jnp.abs(x)))


def _log_sigmoid(x):
    return jnp.minimum(x, 0.0) - jnp.log1p(jnp.exp(-jnp.abs(x)))


def _params(*sem):
    return pltpu.CompilerParams(dimension_semantics=sem, vmem_limit_bytes=VMEM_LIMIT)


def _split_bf16(x, pieces):
    out = []
    for _ in range(pieces - 1):
        p = x.astype(bf16)
        out.append(p)
        x = x - p.astype(f32)
    out.append(x.astype(bf16))
    return out


def _dot_exact01(a, b01, pieces, dims=None):
    acc = None
    for p in _split_bf16(a, pieces):
        if dims is None:
            d = jnp.dot(p, b01, preferred_element_type=f32)
        else:
            d = lax.dot_general(p, b01, dims, preferred_element_type=f32)
        acc = d if acc is None else acc + d
    return acc


def _cumsum_rows(x):
    n = x.shape[0]
    row = lax.broadcasted_iota(jnp.int32, (n, n), 0)
    col = lax.broadcasted_iota(jnp.int32, (n, n), 1)
    tri = (row >= col).astype(bf16)
    acc = None
    for p in _split_bf16(x, 3):
        d = jnp.dot(tri, p, preferred_element_type=f32)
        acc = d if acc is None else acc + d
    return acc


def _norm_body(x_ref, g_ref, o_ref):
    x = x_ref[...]
    y = x * lax.rsqrt(jnp.mean(x * x, axis=-1, keepdims=True) + EPS)
    o_ref[...] = (y * g_ref[...]).astype(o_ref.dtype)


def _rmsnorm(x, g, *, rows, row0, tm, out_dtype):
    blk0 = row0 // tm
    return pl.pallas_call(
        _norm_body, grid=(rows // tm,),
        in_specs=[pl.BlockSpec((tm, D_MODEL), lambda m: (blk0 + m, 0)),
                  pl.BlockSpec((1, D_MODEL), lambda m: (0, 0))],
        out_specs=pl.BlockSpec((tm, D_MODEL), lambda m: (m, 0)),
        out_shape=jax.ShapeDtypeStruct((rows, D_MODEL), out_dtype),
        compiler_params=_params("arbitrary"), name="rmsnorm")(x, g.reshape(1, D_MODEL))


def _inproj_body(u_ref, wa_ref, wb_ref, o_ref, wbf_ref):
    n = pl.program_id(0)

    @pl.when(pl.program_id(1) == 0)
    def _():
        def fill(shift):
            def f():
                if shift == 0:
                    wbf_ref[...] = wa_ref[...].astype(bf16)
                else:
                    wbf_ref[:IN_TN - shift, :] = wa_ref[shift:, :].astype(bf16)
                    wbf_ref[IN_TN - shift:, :] = wb_ref[:shift, :].astype(bf16)
            return f
        pl.when(n < IN_ALIGNED_TILES)(fill(0))
        pl.when((n >= IN_ALIGNED_TILES) & (n < IN_MG_TILE0))(fill(IN_SHIFT_Z))
        pl.when(n >= IN_MG_TILE0)(fill(IN_SHIFT_MG))

    o_ref[...] = lax.dot_general(u_ref[...], wbf_ref[...], (((1,), (1,)), ((), ())),
                                 preferred_element_type=f32)


def _inproj_out_tile(n):
    lead = IN_ALIGNED_TILES
    zx = (P_Q - P_Z) // IN_TN
    return jnp.where(n < lead, n + zx, jnp.where(n < lead + zx, n - lead, n))


def _inproj(u, w_in_t, l, *, tm):
    m = u.shape[0]
    return pl.pallas_call(
        _inproj_body, grid=(IN_TILES, m // tm),
        in_specs=[pl.BlockSpec((tm, D_MODEL), lambda n, i: (i, 0)),
                  pl.BlockSpec((None, IN_TN, D_MODEL), lambda n, i: (l, n, 0)),
                  pl.BlockSpec((None, LANES, D_MODEL), lambda n, i: (l, (n + 1) * (IN_TN // LANES), 0))],
        out_specs=pl.BlockSpec((tm, IN_TN), lambda n, i: (i, _inproj_out_tile(n))),
        out_shape=jax.ShapeDtypeStruct((m, P_END), f32),
        scratch_shapes=[pltpu.VMEM((IN_TN, D_MODEL), bf16)],
        compiler_params=_params("arbitrary", "arbitrary"), name="inproj")(u, w_in_t, w_in_t)


def _mm_nt_body(a_ref, w_ref, o_ref):
    o_ref[...] = lax.dot_general(a_ref[...], w_ref[...].astype(bf16), (((1,), (1,)), ((), ())),
                                 preferred_element_type=f32)


def _matmul_nt(a, w_t, *, tm, name):
    m, k = a.shape
    n = w_t.shape[0]
    return pl.pallas_call(
        _mm_nt_body, grid=(m // tm,),
        in_specs=[pl.BlockSpec((tm, k), lambda i: (i, 0)),
                  pl.BlockSpec((n, k), lambda i: (0, 0))],
        out_specs=pl.BlockSpec((tm, n), lambda i: (i, 0)),
        out_shape=jax.ShapeDtypeStruct((m, n), f32),
        compiler_params=_params("arbitrary"), name=name)(a, w_t)


def _merge_body(oa_ref, ob_ref, ga_ref, gb_ref, wa_ref, wb_ref, o_ref, wabf_ref, wbbf_ref):
    @pl.when(pl.program_id(1) == 0)
    def _():
        wabf_ref[...] = wa_ref[...].astype(bf16)
        wbbf_ref[...] = wb_ref[...].astype(bf16)

    ya = jnp.dot(oa_ref[...], wabf_ref[...], preferred_element_type=f32)
    yb = jnp.dot(ob_ref[...], wbbf_ref[...], preferred_element_type=f32)
    o_ref[...] = (_sig(ga_ref[...]) * ya + _sig(gb_ref[...]) * yb).astype(o_ref.dtype)


def _merge(oa, ob, proj, wa, wb, l, *, tm, tn):
    m = oa.shape[0]
    ga0, gb0 = P_GA // tn, P_GB // tn
    return pl.pallas_call(
        _merge_body, grid=(D_MODEL // tn, m // tm),
        in_specs=[pl.BlockSpec((tm, GLA_VW), lambda j, i: (i, 0)),
                  pl.BlockSpec((tm, SSM_D_INNER), lambda j, i: (i, 0)),
                  pl.BlockSpec((tm, tn), lambda j, i: (i, ga0 + j)),
                  pl.BlockSpec((tm, tn), lambda j, i: (i, gb0 + j)),
                  pl.BlockSpec((None, GLA_VW, tn), lambda j, i: (l, 0, j)),
                  pl.BlockSpec((None, SSM_D_INNER, tn), lambda j, i: (l, 0, j))],
        out_specs=pl.BlockSpec((tm, tn), lambda j, i: (i, j)),
        out_shape=jax.ShapeDtypeStruct((m, D_MODEL), bf16),
        scratch_shapes=[pltpu.VMEM((GLA_VW, tn), bf16), pltpu.VMEM((SSM_D_INNER, tn), bf16)],
        compiler_params=_params("arbitrary", "arbitrary"), name="merge")(oa, ob, proj, proj, wa, wb)


def _route(logits):
    c = lax.broadcasted_iota(jnp.int32, logits.shape, 1)
    big = jnp.int32(1 << 20)
    is_group = c < MOE_GROUPS
    lg = jnp.where(is_group, logits, -jnp.inf)
    top = jnp.max(lg, axis=1, keepdims=True)
    gsel = jnp.min(jnp.where(is_group & (logits == top), c, big), axis=1, keepdims=True)
    p_sel = 1.0 / jnp.sum(jnp.exp(lg - top), axis=1, keepdims=True)
    in_group = ((c >= MOE_LANE0) & (c < MOE_LANE0 + MOE_EXPERTS)
                & (((c - MOE_LANE0) // MOE_PER_GROUP) == gsel))
    le = jnp.where(in_group, logits, -jnp.inf)
    tv0 = jnp.max(le, axis=1, keepdims=True)
    i0 = jnp.min(jnp.where(in_group & (logits == tv0), c, big), axis=1, keepdims=True)
    rest = in_group & (c != i0)
    le1 = jnp.where(rest, logits, -jnp.inf)
    tv1 = jnp.max(le1, axis=1, keepdims=True)
    i1 = jnp.min(jnp.where(rest & (logits == tv1), c, big), axis=1, keepdims=True)
    t = jnp.exp(tv1 - tv0)
    w0 = p_sel / (1.0 + t)
    w1 = p_sel * t / (1.0 + t)
    cw = jnp.where(c == i0, w0, jnp.where(c == i1, w1, 0.0))
    return cw, ((c == i0) | (c == i1)).astype(f32)


def _outproj_body(m_ref, wo_ref, x_ref, g_ref, wr_ref, br_ref, xn_ref, u_ref, cw_ref, oh_ref, cnt_ref,
                  wobf_ref):
    @pl.when(pl.program_id(0) == 0)
    def _():
        wobf_ref[...] = wo_ref[...].astype(bf16)
        cnt_ref[...] = jnp.zeros_like(cnt_ref)

    w_hi, w_lo = _split_bf16(wr_ref[...], 2)
    tm = x_ref.shape[0]
    parts = 2 if tm % 32 == 0 else 1
    for p in range(parts):
        rs = slice(p * (tm // parts), (p + 1) * (tm // parts))
        xn = x_ref[rs, :] + jnp.dot(m_ref[rs, :], wobf_ref[...], preferred_element_type=f32)
        xn_ref[rs, :] = xn
        u = xn * lax.rsqrt(jnp.mean(xn * xn, axis=-1, keepdims=True) + EPS) * g_ref[...]
        u_ref[rs, :] = u
        u_hi, u_lo = _split_bf16(u, 2)
        logits = (jnp.dot(u_hi, w_hi, preferred_element_type=f32)
                  + (jnp.dot(u_hi, w_lo, preferred_element_type=f32)
                     + jnp.dot(u_lo, w_hi, preferred_element_type=f32))) + br_ref[...]
        cw, oh = _route(logits)
        cw_ref[rs, :] = cw
        oh_ref[rs, :] = oh
        cnt_ref[...] += jnp.sum(oh, axis=0, keepdims=True)


def _outproj(merged, wo, l, x, g, wr, br, *, tm):
    m = x.shape[0]
    row = lambda i: (i, 0)
    fix = lambda i: (0, 0)
    return pl.pallas_call(
        _outproj_body, grid=(m // tm,),
        in_specs=[pl.BlockSpec((tm, D_MODEL), row),
                  pl.BlockSpec((None, D_MODEL, D_MODEL), lambda i: (l, 0, 0), pipeline_mode=pl.Buffered(1)),
                  pl.BlockSpec((tm, D_MODEL), row), pl.BlockSpec((1, D_MODEL), fix),
                  pl.BlockSpec((D_MODEL, LANES), fix), pl.BlockSpec((1, LANES), fix)],
        out_specs=[pl.BlockSpec((tm, D_MODEL), row), pl.BlockSpec((tm, D_MODEL), row),
                   pl.BlockSpec((tm, LANES), row), pl.BlockSpec((tm, LANES), row), pl.BlockSpec((1, LANES), fix)],
        out_shape=[jax.ShapeDtypeStruct((m, D_MODEL), f32), jax.ShapeDtypeStruct((m, D_MODEL), f32),
                   jax.ShapeDtypeStruct((m, LANES), f32), jax.ShapeDtypeStruct((m, LANES), f32),
                   jax.ShapeDtypeStruct((1, LANES), f32)],
        scratch_shapes=[pltpu.VMEM((D_MODEL, D_MODEL), bf16)],
        compiler_params=_params("arbitrary"), name="outproj")(merged, wo, x, g, wr, br)


def _plan_body(oh_ref, cw_ref, cnt_ref, dest_ref, wts_ref, te_ref, pad_ref, base_ref, carry_ref, *, rows, blk):
    lane = lax.broadcasted_iota(jnp.int32, (8, LANES), 1)
    is_expert = (lane >= MOE_LANE0) & (lane < MOE_LANE0 + MOE_EXPERTS)

    @pl.when(pl.program_id(0) == 0)
    def _():
        cnt = jnp.broadcast_to(cnt_ref[...], (8, LANES))
        padded = jnp.floor((cnt + (MOE_TILE - 1)) * (1.0 / MOE_TILE)) * MOE_TILE
        src = lax.broadcasted_iota(jnp.int32, (LANES, LANES), 0)
        dst = lax.broadcasted_iota(jnp.int32, (LANES, LANES), 1)
        base = _dot_exact01(padded, (src < dst).astype(bf16), 3)
        base_ref[...] = base[0:1]
        carry_ref[...] = jnp.zeros_like(carry_ref)
        end = base + padded
        total = jnp.max(jnp.where(is_expert, end, 0.0), axis=1, keepdims=True)
        end_col = jnp.transpose(jnp.where(is_expert, end, jnp.inf))[:, 0:1]
        tile_start = (lax.broadcasted_iota(jnp.int32, (LANES, LANES), 1) * MOE_TILE).astype(f32)
        tile_exp = jnp.sum((end_col <= tile_start).astype(f32), axis=0, keepdims=True)
        tile_exp = jnp.minimum(tile_exp, MOE_EXPERTS - 1.0)
        te_ref[0:1, :] = tile_exp.astype(jnp.int32)
        te_ref[1:2, :] = jnp.broadcast_to(total[0:1] * (1.0 / MOE_TILE), (1, LANES)).astype(jnp.int32)
        shift = (src == dst + MOE_LANE0).astype(bf16)
        is_tail = lane[0:1] == MOE_EXPERTS
        pad_lo = jnp.where(is_tail, total[0:1], _dot_exact01(base + cnt, shift, 3)[0:1])
        pad_hi = jnp.where(is_tail, float(rows), _dot_exact01(end, shift, 3)[0:1])
        pad_ref[0:1, :] = pad_lo.astype(jnp.int32)
        pad_ref[1:2, :] = pad_hi.astype(jnp.int32)

    n = blk
    row = lax.broadcasted_iota(jnp.int32, (n, n), 0)
    col = lax.broadcasted_iota(jnp.int32, (n, n), 1)
    earlier = (row > col).astype(bf16)
    lanes = lax.broadcasted_iota(jnp.int32, (n, LANES), 1)
    for s in range(oh_ref.shape[0] // blk):
        rs = slice(s * blk, (s + 1) * blk)
        oh = oh_ref[rs, :]
        rank = jnp.dot(earlier, oh.astype(bf16), preferred_element_type=f32)
        pos = base_ref[...] + carry_ref[...] + rank
        carry_ref[...] += jnp.sum(oh, axis=0, keepdims=True)
        chosen = oh > 0.0
        d_lo = jnp.min(jnp.where(chosen, pos, jnp.inf), axis=1, keepdims=True)
        d_hi = jnp.max(jnp.where(chosen, pos, -1.0), axis=1, keepdims=True)
        cw = cw_ref[rs, :]
        w_lo = jnp.sum(jnp.where(chosen & (pos == d_lo), cw, 0.0), axis=1, keepdims=True)
        w_hi = jnp.sum(jnp.where(chosen & (pos == d_hi), cw, 0.0), axis=1, keepdims=True)
        wts_ref[rs, :] = jnp.where(lanes == 0, w_lo, jnp.where(lanes == 1, w_hi, 0.0))
        dests = jnp.where(lanes == 0, d_lo, jnp.where(lanes == 1, d_hi, 0.0))
        dest_ref[s] = jnp.transpose(dests)[0:2, :].astype(jnp.int32)


def _token_blocks_per_step(nblk):
    return max(d for d in range(1, 6) if nblk % d == 0)


def _moe_plan(oh, cw, cnt, *, blk, rows):
    m = oh.shape[0]
    nblk = m // blk
    nsub = _token_blocks_per_step(nblk)
    fix = lambda i: (0, 0)
    return pl.pallas_call(
        functools.partial(_plan_body, rows=rows, blk=blk), grid=(nblk // nsub,),
        in_specs=[pl.BlockSpec((nsub * blk, LANES), lambda i: (i, 0)),
                  pl.BlockSpec((nsub * blk, LANES), lambda i: (i, 0)),
                  pl.BlockSpec((1, LANES), fix)],
        out_specs=[pl.BlockSpec((nsub, 2, blk), lambda i: (i, 0, 0)),
                   pl.BlockSpec((nsub * blk, LANES), lambda i: (i, 0)),
                   pl.BlockSpec((2, LANES), fix), pl.BlockSpec((2, LANES), fix)],
        out_shape=[jax.ShapeDtypeStruct((nblk, 2, blk), jnp.int32), jax.ShapeDtypeStruct((m, LANES), f32),
                   jax.ShapeDtypeStruct((2, LANES), jnp.int32), jax.ShapeDtypeStruct((2, LANES), jnp.int32)],
        scratch_shapes=[pltpu.VMEM((1, LANES), f32), pltpu.VMEM((1, LANES), f32)],
        compiler_params=_params("arbitrary"), name="moe_plan")(oh, cw, cnt)


def _row_copy(src_ref, s, dst_ref, d, sem):
    return pltpu.make_async_copy(src_ref.at[pl.ds(s, 1), :], dst_ref.at[pl.ds(d, 1), :], sem)


def _dispatch_body(dest_ref, pad_ref, u_ref, xs_ref, zero_ref, sem, *, blk, nsub):
    i = pl.program_id(0)

    def wait_rows(count, which):
        pltpu.make_async_copy(xs_ref.at[pl.ds(0, count), :], xs_ref.at[pl.ds(0, count), :], sem.at[which]).wait()

    @pl.when(i == 0)
    def _():
        zero_ref[...] = jnp.zeros_like(zero_ref)
        group = zero_ref.shape[0]

        def fill_row(r, c):
            _row_copy(zero_ref, 0, xs_ref, r, sem.at[1]).start()
            return c

        def fill_group(lo8):
            def f(r, c):
                row0 = pl.multiple_of(lo8 + r * group, group)
                pltpu.make_async_copy(zero_ref, xs_ref.at[pl.ds(row0, group), :], sem.at[1]).start()
                return c
            return f

        nrow = jnp.int32(0)
        ngroup = jnp.int32(0)
        for e in range(MOE_EXPERTS + 1):
            lo = pad_ref[0, e]
            hi = pad_ref[1, e]
            lo8 = jnp.minimum((lo + group - 1) // group * group, hi)
            lax.fori_loop(lo, lo8, fill_row, 0)
            lax.fori_loop(0, (hi - lo8) // group, fill_group(lo8), 0)
            nrow = nrow + (lo8 - lo)
            ngroup = ngroup + (hi - lo8) // group

        def wait_row(r, c):
            _row_copy(zero_ref, 0, xs_ref, 0, sem.at[1]).wait()
            return c
        lax.fori_loop(0, nrow, wait_row, 0)

        def wait_group(r, c):
            wait_rows(group, 1)
            return c
        lax.fori_loop(0, ngroup, wait_group, 0)

    for s in range(nsub):
        def send(j, c, s=s):
            _row_copy(u_ref, s * blk + j, xs_ref, dest_ref[s, 0, j], sem.at[0]).start()
            _row_copy(u_ref, s * blk + j, xs_ref, dest_ref[s, 1, j], sem.at[0]).start()
            return c
        lax.fori_loop(0, blk, send, 0)
    for _ in range(MOE_TOP_K * nsub):
        wait_rows(blk, 0)


def _moe_dispatch(u, dest, pad, *, rows, blk):
    m = u.shape[0]
    nblk = m // blk
    nsub = _token_blocks_per_step(nblk)
    return pl.pallas_call(
        functools.partial(_dispatch_body, blk=blk, nsub=nsub), grid=(nblk // nsub,),
        in_specs=[pl.BlockSpec((nsub, 2, blk), lambda i: (i, 0, 0), memory_space=pltpu.SMEM),
                  pl.BlockSpec(memory_space=pltpu.SMEM),
                  pl.BlockSpec((nsub * blk, D_MODEL), lambda i: (i, 0))],
        out_specs=pl.BlockSpec(memory_space=pl.ANY),
        out_shape=jax.ShapeDtypeStruct((rows, D_MODEL), f32),
        scratch_shapes=[pltpu.VMEM((8, D_MODEL), f32), pltpu.SemaphoreType.DMA((2,))],
        compiler_params=_params("arbitrary"), name="moe_dispatch")(dest, pad, u)


def _gmm_body(te_ref, xs_ref, wg_ref, wu_ref, wd_ref, ys_ref, wgb_ref, wub_ref, wdb_ref):
    i = pl.program_id(0)
    n_used = te_ref[1, 0]

    @pl.when(i < n_used)
    def _():
        @pl.when((i == 0) | (te_ref[0, i] != te_ref[0, jnp.maximum(i - 1, 0)]))
        def _():
            wgb_ref[...] = wg_ref[...].astype(bf16)
            wub_ref[...] = wu_ref[...].astype(bf16)
            wdb_ref[...] = wd_ref[...].astype(bf16)

        x = xs_ref[...].astype(bf16)
        hg = jnp.dot(x, wgb_ref[...], preferred_element_type=f32)
        hu = jnp.dot(x, wub_ref[...], preferred_element_type=f32)
        h = ((hg * _sig(hg)) * hu).astype(bf16)
        ys_ref[...] = jnp.dot(h, wdb_ref[...], preferred_element_type=f32)

    @pl.when(i >= n_used)
    def _():
        ys_ref[...] = jnp.zeros_like(ys_ref)


def _moe_gmm(te, xs, wg, wu, wd, l):
    rows = xs.shape[0]
    ntiles = rows // MOE_TILE
    tile = lambda i, te: (jnp.minimum(i, te[1, 0] - 1), 0)
    up = lambda i, te: (l, te[0, i], 0, 0)
    return pl.pallas_call(
        _gmm_body,
        grid_spec=pltpu.PrefetchScalarGridSpec(
            num_scalar_prefetch=1, grid=(ntiles,),
            in_specs=[pl.BlockSpec((MOE_TILE, D_MODEL), tile),
                      pl.BlockSpec((None, None, D_MODEL, MOE_D_FF), up),
                      pl.BlockSpec((None, None, D_MODEL, MOE_D_FF), up),
                      pl.BlockSpec((None, None, MOE_D_FF, D_MODEL), up)],
            out_specs=pl.BlockSpec((MOE_TILE, D_MODEL), lambda i, te: (i, 0)),
            scratch_shapes=[pltpu.VMEM((D_MODEL, MOE_D_FF), bf16), pltpu.VMEM((D_MODEL, MOE_D_FF), bf16),
                            pltpu.VMEM((MOE_D_FF, D_MODEL), bf16)]),
        out_shape=jax.ShapeDtypeStruct((rows, D_MODEL), f32),
        compiler_params=_params("arbitrary"), name="moe_gmm")(te, xs, wg, wu, wd)


def _combine_body(dcur_ref, dnext_ref, xn_ref, wts_ref, ys_ref, g_ref, o_ref, un_ref, ya_ref, yb_ref, sem,
                  *, blk, nblk):
    i = pl.program_id(0)
    slot = i % 2

    def gather(dref, s):
        def one(j, c):
            _row_copy(ys_ref, dref[0, 0, j], ya_ref.at[s], j, sem.at[s]).start()
            _row_copy(ys_ref, dref[0, 1, j], yb_ref.at[s], j, sem.at[s]).start()
            return c
        lax.fori_loop(0, blk, one, 0)

    @pl.when(i == 0)
    def _():
        gather(dcur_ref, 0)

    @pl.when(i + 1 < nblk)
    def _():
        gather(dnext_ref, 1 - slot)

    pltpu.make_async_copy(ys_ref.at[pl.ds(0, blk), :], ya_ref.at[slot], sem.at[slot]).wait()
    pltpu.make_async_copy(ys_ref.at[pl.ds(0, blk), :], yb_ref.at[slot], sem.at[slot]).wait()
    w = wts_ref[...]
    x = xn_ref[...] + w[:, 0:1] * ya_ref[slot] + w[:, 1:2] * yb_ref[slot]
    o_ref[...] = x
    un_ref[...] = (x * lax.rsqrt(jnp.mean(x * x, axis=-1, keepdims=True) + EPS) * g_ref[...]).astype(bf16)


def _moe_combine(dest, xn, wts, ys, g_next, *, blk):
    m = xn.shape[0]
    nblk = m // blk
    return pl.pallas_call(
        functools.partial(_combine_body, blk=blk, nblk=nblk), grid=(nblk,),
        in_specs=[pl.BlockSpec((1, 2, blk), lambda i: (i, 0, 0), memory_space=pltpu.SMEM),
                  pl.BlockSpec((1, 2, blk), lambda i: (jnp.minimum(i + 1, nblk - 1), 0, 0),
                               memory_space=pltpu.SMEM),
                  pl.BlockSpec((blk, D_MODEL), lambda i: (i, 0)),
                  pl.BlockSpec((blk, LANES), lambda i: (i, 0)),
                  pl.BlockSpec(memory_space=pl.ANY),
                  pl.BlockSpec((1, D_MODEL), lambda i: (0, 0))],
        out_specs=[pl.BlockSpec((blk, D_MODEL), lambda i: (i, 0)), pl.BlockSpec((blk, D_MODEL), lambda i: (i, 0))],
        out_shape=[jax.ShapeDtypeStruct((m, D_MODEL), f32), jax.ShapeDtypeStruct((m, D_MODEL), bf16)],
        scratch_shapes=[pltpu.VMEM((2, blk, D_MODEL), f32), pltpu.VMEM((2, blk, D_MODEL), f32),
                        pltpu.SemaphoreType.DMA((2,))],
        compiler_params=_params("arbitrary"), name="moe_combine")(dest, dest, xn, wts, ys, g_next)


def _moe(u, cw, oh, cnt, xn, wg, wu, wd, l, g_next):
    m = u.shape[0]
    blk = LANES
    rows = -(-(MOE_TOP_K * m + MOE_EXPERTS * (MOE_TILE - 1)) // MOE_TILE) * MOE_TILE
    assert rows // MOE_TILE <= LANES
    dest, wts, te, pad = _moe_plan(oh, cw, cnt, blk=blk, rows=rows)
    xs = _moe_dispatch(u, dest, pad, rows=rows, blk=blk)
    ys = _moe_gmm(te, xs, wg, wu, wd, l)
    return _moe_combine(dest, xn, wts, ys, g_next.reshape(1, D_MODEL), blk=blk)


def _gla_body(q_ref, k_ref, v_ref, r_ref, sm_ref, w2_ref, bg_ref, gn_ref, s0_ref, prev_ref,
              o_ref, sout_ref, st_ref, b_ref, att_ref, *, chunk, nsteps):
    del prev_ref
    t = pl.program_id(1)

    @pl.when(t == 0)
    def _():
        st_ref[...] = s0_ref[0]

    for c0 in range(0, q_ref.shape[0], chunk):
        _gla_chunk(q_ref, k_ref, v_ref, r_ref, sm_ref, w2_ref, bg_ref, gn_ref, o_ref, st_ref, b_ref, att_ref,
                   c0=c0, chunk=chunk)

    @pl.when(t == nsteps - 1)
    def _():
        sout_ref[0] = st_ref[...]


def _gla_chunk(q_ref, k_ref, v_ref, r_ref, sm_ref, w2_ref, bg_ref, gn_ref, o_ref, st_ref, b_ref, att_ref,
               *, c0, chunk):
    cr = slice(c0, c0 + chunk)
    glr = sm_ref[cr, S_GLR:S_GLR + GLA_GATE_RANK].astype(bf16)
    pre = jnp.dot(glr, w2_ref[...].astype(bf16), preferred_element_type=f32) + bg_ref[...]
    b_ref[...] = _cumsum_rows(_log_sigmoid(pre) * (1.0 / GLA_GATE_TAU))

    sub_row = lax.broadcasted_iota(jnp.int32, (GLA_SUB, GLA_DK), 0)
    sub_col = lax.broadcasted_iota(jnp.int32, (GLA_SUB, GLA_SUB), 1)
    nt = (((1,), (1,)), ((), ()))
    tn = (((0,), (0,)), ((), ()))
    for h in range(GLA_HEADS):
        ks = slice(h * GLA_DK, (h + 1) * GLA_DK)
        vs = slice(h * GLA_DV, (h + 1) * GLA_DV)
        q = q_ref[cr, ks] * (GLA_DK ** -0.5)
        k = k_ref[cr, ks]
        vb = v_ref[cr, vs].astype(bf16)
        b = b_ref[:, ks]
        b_last = b_ref[chunk - 1:chunk, ks]
        state = st_ref[h]
        o = jnp.dot((q * jnp.exp(b)).astype(bf16), state.astype(bf16), preferred_element_type=f32)

        att_ref[h] = jnp.zeros((chunk, chunk), f32)
        for blk in range(chunk // GLA_SUB):
            r0 = blk * GLA_SUB
            q_i = q[r0:r0 + GLA_SUB]
            b_i = b[r0:r0 + GLA_SUB]
            if blk > 0:
                c_i = b_ref[r0 - 1:r0, ks]
                q_dec = (q_i * jnp.exp(b_i - c_i)).astype(bf16)
                k_dec = (k[:r0] * jnp.exp(c_i - b[:r0])).astype(bf16)
                att_ref[h, r0:r0 + GLA_SUB, 0:r0] = lax.dot_general(q_dec, k_dec, nt,
                                                                   preferred_element_type=f32)
            diag = jnp.zeros((GLA_SUB, GLA_SUB), f32)
            for j in range(GLA_SUB):
                b_j = b_ref[r0 + j:r0 + j + 1, ks]
                k_j = k_ref[c0 + r0 + j:c0 + r0 + j + 1, ks]
                decay = jnp.exp(jnp.where(sub_row >= j, b_i - b_j, -jnp.inf))
                a_j = jnp.sum(q_i * decay * k_j, axis=1, keepdims=True)
                diag = jnp.where(sub_col == j, a_j, diag)
            att_ref[h, r0:r0 + GLA_SUB, r0:r0 + GLA_SUB] = diag

        o = o + jnp.dot(att_ref[h].astype(bf16), vb, preferred_element_type=f32)

        k_end = (k * jnp.exp(b_last - b)).astype(bf16)
        upd = lax.dot_general(k_end, vb, tn, preferred_element_type=f32)
        dec_col = jnp.transpose(jnp.broadcast_to(jnp.exp(b_last), (8, GLA_DK)))[:, 0:1]
        st_ref[h] = state * dec_col + upd

        o = o * lax.rsqrt(jnp.mean(o * o, axis=1, keepdims=True) + EPS) * gn_ref[...]
        r = r_ref[cr, vs]
        o_ref[cr, vs] = (o * (r * _sig(r))).astype(o_ref.dtype)


def _gla(proj, small, w2, bg, gn, l, s0, l_s0, prev, *, row0, nseq, seqlen, chunk):
    nchunks = seqlen // chunk
    cps = 2 if nchunks % 2 == 0 else 1
    blk = cps * chunk
    nsteps = nchunks // cps
    blk0 = row0 // blk
    rows = lambda b, t: blk0 + b * nsteps + t
    qk, vw = GLA_QK, GLA_VW
    hd = (GLA_HEADS, GLA_DK, GLA_DV)
    return pl.pallas_call(
        functools.partial(_gla_body, chunk=chunk, nsteps=nsteps),
        grid=(nseq, nsteps),
        in_specs=[pl.BlockSpec((blk, qk), lambda b, t: (rows(b, t), P_Q // qk)),
                  pl.BlockSpec((blk, qk), lambda b, t: (rows(b, t), P_K // qk)),
                  pl.BlockSpec((blk, vw), lambda b, t: (rows(b, t), P_V // vw)),
                  pl.BlockSpec((blk, vw), lambda b, t: (rows(b, t), P_R // vw)),
                  pl.BlockSpec((blk, LANES), lambda b, t: (rows(b, t), 0)),
                  pl.BlockSpec((None, GLA_GATE_RANK, qk), lambda b, t: (l, 0, 0)),
                  pl.BlockSpec((None, 1, qk), lambda b, t: (l, 0, 0)),
                  pl.BlockSpec((None, 1, GLA_DV), lambda b, t: (l, 0, 0)),
                  pl.BlockSpec((None, 1) + hd, lambda b, t: (l_s0, b, 0, 0, 0)),
                  pl.BlockSpec(memory_space=pl.ANY)],
        out_specs=[pl.BlockSpec((blk, vw), lambda b, t: (rows(b, t), 0)),
                   pl.BlockSpec((1,) + hd, lambda b, t: (b, 0, 0, 0))],
        out_shape=[jax.ShapeDtypeStruct(prev.shape, bf16),
                   jax.ShapeDtypeStruct((nseq,) + hd, f32)],
        scratch_shapes=[pltpu.VMEM(hd, f32), pltpu.VMEM((chunk, qk), f32),
                        pltpu.VMEM((GLA_HEADS, chunk, chunk), f32)],
        input_output_aliases={9: 0},
        compiler_params=_params("arbitrary", "arbitrary"),
        name="gla")(proj, proj, proj, proj, small, w2, bg, gn, s0, prev)


def _ssd_body(z_ref, x_ref, bm_ref, cm_ref, sm_ref, wx_ref, wb_ref, wc_ref, bx_ref, bb_ref, bc_ref,
              sx_ref, sb_ref, sc_ref, dtb_ref, alog_ref, dskip_ref, nrm_ref, h0_ref, prev_ref,
              y_ref, hout_ref, cxo_ref, cbo_ref, cco_ref,
              h_ref, xpx_ref, xpb_ref, xpc_ref, xs_ref, *, chunk, nchunks):
    del prev_ref
    t = pl.program_id(1)
    hist = CONV_PAD - (SSM_CONV - 1)

    @pl.when(t == 0)
    def _():
        h_ref[...] = h0_ref[0]
        xpx_ref[hist:CONV_PAD, :] = sx_ref[0]
        xpb_ref[hist:CONV_PAD, :] = sb_ref[0]
        xpc_ref[hist:CONV_PAD, :] = sc_ref[0]

    def conv_silu(xp_ref, cur_ref, w_ref, bias_ref):
        x = cur_ref[...]
        taps = SSM_CONV - 1
        acc = bias_ref[...] + x * w_ref[taps:taps + 1, :]
        for s in range(1, SSM_CONV):
            acc = acc + pltpu.roll(x, s, axis=0) * w_ref[taps - s:taps - s + 1, :]
        xp_ref[CONV_PAD:2 * CONV_PAD, :] = x[0:CONV_PAD]
        head = bias_ref[...] + xp_ref[pl.ds(hist, CONV_PAD), :] * w_ref[0:1, :]
        for i in range(1, SSM_CONV):
            head = head + xp_ref[pl.ds(hist + i, CONV_PAD), :] * w_ref[i:i + 1, :]
        xp_ref[hist:CONV_PAD, :] = x[chunk - taps:chunk]
        acc = jnp.concatenate([head, acc[CONV_PAD:]], axis=0)
        return acc * _sig(acc)

    xs_ref[...] = conv_silu(xpx_ref, x_ref, wx_ref, bx_ref)
    bmat = conv_silu(xpb_ref, bm_ref, wb_ref, bb_ref).astype(bf16)
    cmat = conv_silu(xpc_ref, cm_ref, wc_ref, bc_ref).astype(bf16)

    @pl.when(t == nchunks - 1)
    def _():
        cxo_ref[0] = xpx_ref[hist:CONV_PAD, :]
        cbo_ref[0] = xpb_ref[hist:CONV_PAD, :]
        cco_ref[0] = xpc_ref[hist:CONV_PAD, :]

    dt = _softplus(sm_ref[...] + dtb_ref[...])
    cum = _cumsum_rows(dt * -jnp.exp(alog_ref[...]))
    last = cum[chunk - 1:chunk, :]
    w_end = jnp.exp(last - cum) * dt
    e_cum = jnp.exp(cum)
    e_last = jnp.exp(last)
    cum2 = cum * LOG2E
    adj2_t = ((cum - jnp.log(dt)) * LOG2E).T

    row = lax.broadcasted_iota(jnp.int32, (chunk, chunk), 0)
    col = lax.broadcasted_iota(jnp.int32, (chunk, chunk), 1)
    causal = row >= col
    head = lax.broadcasted_iota(jnp.int32, (LANES, SSM_GROUP_W), 0)
    chan = lax.broadcasted_iota(jnp.int32, (LANES, SSM_GROUP_W), 1)
    nt = (((1,), (1,)), ((), ()))
    tn = (((0,), (0,)), ((), ()))
    for g in range(SSM_GROUPS):
        cs = slice(g * SSM_GROUP_W, (g + 1) * SSM_GROUP_W)
        ns = slice(g * SSM_D_STATE, (g + 1) * SSM_D_STATE)
        xs = xs_ref[:, cs]
        xsb = xs.astype(bf16)
        b_g = bmat[:, ns]
        c_g = cmat[:, ns]
        widen = (head == g * SSM_HPG + chan // SSM_HEAD_DIM).astype(bf16)
        w_full = _dot_exact01(w_end, widen, 2)
        ecum_full = _dot_exact01(e_cum, widen, 2)

        cb = lax.dot_general(c_g, b_g, nt, preferred_element_type=f32)
        state = h_ref[g]
        y = lax.dot_general(c_g, state.astype(bf16), nt, preferred_element_type=f32) * ecum_full
        parts = []
        for j in range(SSM_HPG):
            hd = g * SSM_HPG + j
            seg = jnp.exp2(jnp.where(causal, cum2[:, hd:hd + 1] - adj2_t[hd:hd + 1, :], -jnp.inf))
            parts.append(jnp.dot((cb * seg).astype(bf16), xsb[:, j * SSM_HEAD_DIM:(j + 1) * SSM_HEAD_DIM],
                                 preferred_element_type=f32))
        y = y + jnp.concatenate(parts, axis=1)

        upd = lax.dot_general((xs * w_full).astype(bf16), b_g, tn, preferred_element_type=f32)
        for j in range(SSM_HPG):
            hd = g * SSM_HPG + j
            rs = slice(j * SSM_HEAD_DIM, (j + 1) * SSM_HEAD_DIM)
            h_ref[g, rs, :] = state[rs, :] * e_last[:, hd:hd + 1] + upd[rs, :]

        y = y + dskip_ref[:, cs] * xs
        z = z_ref[:, cs]
        y = y * (z * _sig(z))
        y = y * lax.rsqrt(jnp.mean(y * y, axis=1, keepdims=True) + EPS) * nrm_ref[:, cs]
        y_ref[:, cs] = y.astype(y_ref.dtype)

    @pl.when(t == nchunks - 1)
    def _():
        hout_ref[0] = h_ref[...]


def _ssd(proj, small, conv_w, conv_b, l, conv_s, l_cs, dtb, alog, dskip, nrm, h0, l_h0, prev,
         *, row0, nseq, seqlen, chunk):
    nchunks = seqlen // chunk
    blk0 = row0 // chunk
    di, bc = SSM_D_INNER, SSM_BC
    rows = lambda b, t: blk0 + b * nchunks + t
    taps, hist = SSM_CONV, SSM_CONV - 1
    xb, bb, cb = 0, di // bc, di // bc + 1
    hs = (SSM_GROUPS, SSM_GROUP_W, SSM_D_STATE)
    fix3 = lambda b, t: (l, 0, 0)
    in_specs = [
        pl.BlockSpec((chunk, di), lambda b, t: (rows(b, t), P_Z // di)),
        pl.BlockSpec((chunk, di), lambda b, t: (rows(b, t), P_X // di)),
        pl.BlockSpec((chunk, bc), lambda b, t: (rows(b, t), P_B // bc)),
        pl.BlockSpec((chunk, bc), lambda b, t: (rows(b, t), P_C // bc)),
        pl.BlockSpec((chunk, LANES), lambda b, t: (rows(b, t), 0)),
        pl.BlockSpec((None, taps, di), lambda b, t: (l, 0, xb)),
        pl.BlockSpec((None, taps, bc), lambda b, t: (l, 0, bb)),
        pl.BlockSpec((None, taps, bc), lambda b, t: (l, 0, cb)),
        pl.BlockSpec((None, 1, di), lambda b, t: (l, 0, xb)),
        pl.BlockSpec((None, 1, bc), lambda b, t: (l, 0, bb)),
        pl.BlockSpec((None, 1, bc), lambda b, t: (l, 0, cb)),
        pl.BlockSpec((None, 1, hist, di), lambda b, t: (l_cs, b, 0, xb)),
        pl.BlockSpec((None, 1, hist, bc), lambda b, t: (l_cs, b, 0, bb)),
        pl.BlockSpec((None, 1, hist, bc), lambda b, t: (l_cs, b, 0, cb)),
        pl.BlockSpec((None, 1, LANES), fix3),
        pl.BlockSpec((None, 1, LANES), fix3),
        pl.BlockSpec((None, 1, di), fix3),
        pl.BlockSpec((None, 1, di), fix3),
        pl.BlockSpec((None, 1) + hs, lambda b, t: (l_h0, b, 0, 0, 0)),
        pl.BlockSpec(memory_space=pl.ANY),
    ]
    return pl.pallas_call(
        functools.partial(_ssd_body, chunk=chunk, nchunks=nchunks),
        grid=(nseq, nchunks),
        in_specs=in_specs,
        out_specs=[pl.BlockSpec((chunk, di), lambda b, t: (rows(b, t), 0)),
                   pl.BlockSpec((1,) + hs, lambda b, t: (b, 0, 0, 0)),
                   pl.BlockSpec((1, hist, di), lambda b, t: (b, 0, 0)),
                   pl.BlockSpec((1, hist, bc), lambda b, t: (b, 0, 0)),
                   pl.BlockSpec((1, hist, bc), lambda b, t: (b, 0, 0))],
        out_shape=[jax.ShapeDtypeStruct(prev.shape, bf16),
                   jax.ShapeDtypeStruct((nseq,) + hs, f32),
                   jax.ShapeDtypeStruct((nseq, hist, di), f32),
                   jax.ShapeDtypeStruct((nseq, hist, bc), f32),
                   jax.ShapeDtypeStruct((nseq, hist, bc), f32)],
        scratch_shapes=[pltpu.VMEM(hs, f32), pltpu.VMEM((2 * CONV_PAD, di), f32),
                        pltpu.VMEM((2 * CONV_PAD, bc), f32), pltpu.VMEM((2 * CONV_PAD, bc), f32),
                        pltpu.VMEM((chunk, di), f32)],
        input_output_aliases={19: 0},
        compiler_params=_params("arbitrary", "arbitrary"),
        name="ssd")(proj, proj, proj, proj, small, conv_w, conv_w, conv_w, conv_b, conv_b, conv_b,
                    conv_s, conv_s, conv_s, dtb, alog, dskip, nrm, h0, prev)


def _pad_lanes(v):
    return jnp.pad(v, ((0, 0), (0, LANES - v.shape[-1])))[:, None, :]


def kernel(x_prompt, x_sample, state_gla, state_ssm, state_conv, norm_mix, w_in, gla_w_gate2, gla_b_gate,
           gla_norm, gla_w_out, ssm_conv_w, ssm_conv_b, ssm_dt_bias, ssm_a_log, ssm_d, ssm_norm, ssm_w_out,
           w_o, norm_ffn, router_w1, router_b1, router_w2, router_b2, exp_w_gate, exp_w_up, exp_w_down,
           norm_final):
    n_p, t_p = x_prompt.shape[:2]
    n_s, t_s = x_sample.shape[:2]
    assert n_p == 1
    rows_p, rows_s = n_p * t_p, n_s * t_s
    m = rows_p + rows_s
    tm = 832 if m % 832 == 0 else 128
    tm_out = 416 if m % 416 == 0 else 128
    gla_chunk_p, ssd_chunk_p = min(64, t_p), min(256, t_p)

    x = jnp.concatenate([x_prompt.reshape(rows_p, D_MODEL), x_sample.reshape(rows_s, D_MODEL)], axis=0)
    zeros_gla = jnp.zeros((1, n_p, GLA_HEADS, GLA_DK, GLA_DV), f32)
    zeros_ssm = jnp.zeros((1, n_p, SSM_GROUPS, SSM_GROUP_W, SSM_D_STATE), f32)
    zeros_conv = jnp.zeros((1, n_p, SSM_CONV - 1, SSM_CONV_DIM), f32)
    state_ssm_g = state_ssm.reshape(DEPTH, n_s, SSM_GROUPS, SSM_GROUP_W, SSM_D_STATE)

    bg3 = gla_b_gate[:, None, :]
    gn3 = gla_norm[:, None, :]
    conv_b3 = ssm_conv_b[:, None, :]
    dtb3 = _pad_lanes(ssm_dt_bias)
    alog3 = _pad_lanes(ssm_a_log)
    dskip3 = jnp.repeat(ssm_d, SSM_HEAD_DIM, axis=1)[:, None, :]
    nrm3 = ssm_norm[:, None, :]

    w_in_t = jnp.swapaxes(w_in, 1, 2)
    oa = jnp.zeros((m, GLA_VW), bf16)
    ob = jnp.zeros((m, SSM_D_INNER), bf16)

    outs = {k: [] for k in ("gla_p", "ssm_p", "conv_p", "gla_s", "ssm_s", "conv_s")}
    for l in range(DEPTH):
        w_small_t = jnp.concatenate(
            [w_in_t[l, W_DT:W_MG], w_in_t[l, W_GLR:W_Z],
             jnp.zeros((LANES - SSM_HEADS - GLA_GATE_RANK, D_MODEL), f32)], axis=0)

        if l == 0:
            u = _rmsnorm(x, norm_mix[l], rows=m, row0=0, tm=tm, out_dtype=bf16)
        proj = _inproj(u, w_in_t, l, tm=tm)
        small = _matmul_nt(u, w_small_t, tm=tm, name="inproj_small")

        gla_args = (proj, small, gla_w_gate2, bg3, gn3, l)
        oa, g_p = _gla(*gla_args, zeros_gla, 0, oa, row0=0, nseq=n_p, seqlen=t_p, chunk=gla_chunk_p)
        oa, g_s = _gla(*gla_args, state_gla, l, oa, row0=rows_p, nseq=n_s, seqlen=t_s, chunk=t_s)

        ssd_args = (proj, small, ssm_conv_w, conv_b3, l)
        ssd_tail = (dtb3, alog3, dskip3, nrm3)
        ob, h_p, *conv_p = _ssd(*ssd_args, zeros_conv, 0, *ssd_tail, zeros_ssm, 0, ob,
                                row0=0, nseq=n_p, seqlen=t_p, chunk=ssd_chunk_p)
        ob, h_s, *conv_s = _ssd(*ssd_args, state_conv, l, *ssd_tail, state_ssm_g, l, ob,
                                row0=rows_p, nseq=n_s, seqlen=t_s, chunk=t_s)

        merged = _merge(oa, ob, proj, gla_w_out, ssm_w_out, l, tm=tm_out, tn=512)

        wr = jnp.concatenate([router_w1[l], router_w2[l],
                              jnp.zeros((D_MODEL, LANES - MOE_GROUPS - MOE_EXPERTS), f32)], axis=1)
        br = jnp.pad(jnp.concatenate([router_b1[l], router_b2[l]]),
                     (0, LANES - MOE_GROUPS - MOE_EXPERTS)).reshape(1, LANES)
        xn, u2, cw, oh, cnt = _outproj(merged, w_o, l, x, norm_ffn[l].reshape(1, -1), wr, br, tm=tm_out)
        x, u = _moe(u2, cw, oh, cnt, xn, exp_w_gate, exp_w_up, exp_w_down, l, norm_mix[(l + 1) % DEPTH])

        outs["gla_p"].append(g_p)
        outs["gla_s"].append(g_s)
        outs["ssm_p"].append(h_p.reshape(n_p, SSM_HEADS, SSM_HEAD_DIM, SSM_D_STATE))
        outs["ssm_s"].append(h_s.reshape(n_s, SSM_HEADS, SSM_HEAD_DIM, SSM_D_STATE))
        outs["conv_p"].append(jnp.concatenate(conv_p, axis=-1))
        outs["conv_s"].append(jnp.concatenate(conv_s, axis=-1))

    tm_p = 512 if rows_p % 512 == 0 else 128
    tm_s = rows_s if rows_p % rows_s == 0 else 8
    y_p = _rmsnorm(x, norm_final, rows=rows_p, row0=0, tm=tm_p, out_dtype=f32)
    y_s = _rmsnorm(x, norm_final, rows=rows_s, row0=rows_p, tm=tm_s, out_dtype=f32)
    return (y_p.reshape(n_p, t_p, D_MODEL), y_s.reshape(n_s, t_s, D_MODEL),
            jnp.stack(outs["gla_p"]), jnp.stack(outs["ssm_p"]), jnp.stack(outs["conv_p"]),
            jnp.stack(outs["gla_s"]), jnp.stack(outs["ssm_s"]), jnp.stack(outs["conv_s"]))
```

```python
import functools
import math

import jax
import jax.numpy as jnp
from jax import lax
from jax.experimental import pallas as pl
from jax.experimental.pallas import tpu as pltpu

f32 = jnp.float32
bf16 = jnp.bfloat16
HIGHEST = lax.Precision.HIGHEST
LOG2E = math.log2(math.e)

D_MODEL = 2048
DEPTH = 2
EPS = 1e-6
GLA_HEADS = 4
GLA_DK = 256
GLA_DV = 512
GLA_QK = GLA_HEADS * GLA_DK
GLA_VW = GLA_HEADS * GLA_DV
GLA_GATE_RANK = 16
GLA_GATE_TAU = 16.0
GLA_SUB = 16
SSM_D_INNER = 4096
SSM_HEAD_DIM = 64
SSM_HEADS = 64
SSM_GROUPS = 8
SSM_HPG = 8
SSM_GROUP_W = SSM_HPG * SSM_HEAD_DIM
SSM_D_STATE = 128
SSM_BC = SSM_GROUPS * SSM_D_STATE
SSM_CONV = 4
SSM_CONV_DIM = 6144
MOE_GROUPS = 4
MOE_PER_GROUP = 4
MOE_EXPERTS = 16
MOE_D_FF = 512
MOE_TOP_K = 2
MOE_LANE0 = MOE_GROUPS
MOE_TILE = 256
LANES = 128
CONV_PAD = 8

W_GLR = 6144
W_Z = 6160
W_DT = 16400
W_MG = 16464
IN_TN = 1024
IN_TILES = 20480 // IN_TN
IN_ALIGNED_TILES = W_GLR // IN_TN
IN_MG_TILE0 = (W_DT - W_Z + W_GLR) // IN_TN
IN_SHIFT_Z = W_Z - W_GLR
IN_SHIFT_MG = W_MG - IN_MG_TILE0 * IN_TN
P_Z, P_X, P_Q, P_K, P_V, P_R, P_B, P_C, P_GA, P_GB, P_END = (
    0, 4096, 8192, 9216, 10240, 12288, 14336, 15360, 16384, 18432, 20480)
S_GLR = 64

VMEM_LIMIT = 56 * 1024 * 1024


def _sig(x):
    return 1.0 / (1.0 + jnp.exp(-x))


def _softplus(x):
    return jnp.maximum(x, 0.0) + jnp.log1p(jnp.exp(-jnp.abs(x)))


def _log_sigmoid(x):
    return jnp.minimum(x, 0.0) - jnp.log1p(jnp.exp(-jnp.abs(x)))


def _params(*sem):
    return pltpu.CompilerParams(dimension_semantics=sem, vmem_limit_bytes=VMEM_LIMIT)


def _split_bf16(x, pieces):
    out = []
    for _ in range(pieces - 1):
        p = x.astype(bf16)
        out.append(p)
        x = x - p.astype(f32)
    out.append(x.astype(bf16))
    return out


def _dot_exact01(a, b01, pieces, dims=None):
    acc = None
    for p in _split_bf16(a, pieces):
        if dims is None:
            d = jnp.dot(p, b01, preferred_element_type=f32)
        else:
            d = lax.dot_general(p, b01, dims, preferred_element_type=f32)
        acc = d if acc is None else acc + d
    return acc


def _cumsum_rows(x):
    n = x.shape[0]
    row = lax.broadcasted_iota(jnp.int32, (n, n), 0)
    col = lax.broadcasted_iota(jnp.int32, (n, n), 1)
    tri = (row >= col).astype(bf16)
    acc = None
    for p in _split_bf16(x, 3):
        d = jnp.dot(tri, p, preferred_element_type=f32)
        acc = d if acc is None else acc + d
    return acc


def _norm_body(x_ref, g_ref, o_ref):
    x = x_ref[...]
    y = x * lax.rsqrt(jnp.mean(x * x, axis=-1, keepdims=True) + EPS)
    o_ref[...] = (y * g_ref[...]).astype(o_ref.dtype)


def _rmsnorm(x, g, *, rows, row0, tm, out_dtype):
    blk0 = row0 // tm
    return pl.pallas_call(
        _norm_body, grid=(rows // tm,),
        in_specs=[pl.BlockSpec((tm, D_MODEL), lambda m: (blk0 + m, 0)),
                  pl.BlockSpec((1, D_MODEL), lambda m: (0, 0))],
        out_specs=pl.BlockSpec((tm, D_MODEL), lambda m: (m, 0)),
        out_shape=jax.ShapeDtypeStruct((rows, D_MODEL), out_dtype),
        compiler_params=_params("arbitrary"), name="rmsnorm")(x, g.reshape(1, D_MODEL))


def _inproj_body(u_ref, wa_ref, wb_ref, o_ref, wbf_ref):
    n = pl.program_id(0)

    @pl.when(pl.program_id(1) == 0)
    def _():
        def fill(shift):
            def f():
                if shift == 0:
                    wbf_ref[...] = wa_ref[...].astype(bf16)
                else:
                    wbf_ref[:IN_TN - shift, :] = wa_ref[shift:, :].astype(bf16)
                    wbf_ref[IN_TN - shift:, :] = wb_ref[:shift, :].astype(bf16)
            return f
        pl.when(n < IN_ALIGNED_TILES)(fill(0))
        pl.when((n >= IN_ALIGNED_TILES) & (n < IN_MG_TILE0))(fill(IN_SHIFT_Z))
        pl.when(n >= IN_MG_TILE0)(fill(IN_SHIFT_MG))

    o_ref[...] = lax.dot_general(u_ref[...], wbf_ref[...], (((1,), (1,)), ((), ())),
                                 preferred_element_type=f32)


def _inproj_out_tile(n):
    lead = IN_ALIGNED_TILES
    zx = (P_Q - P_Z) // IN_TN
    return jnp.where(n < lead, n + zx, jnp.where(n < lead + zx, n - lead, n))


def _inproj(u, w_in_t, l, *, tm):
    m = u.shape[0]
    return pl.pallas_call(
        _inproj_body, grid=(IN_TILES, m // tm),
        in_specs=[pl.BlockSpec((tm, D_MODEL), lambda n, i: (i, 0)),
                  pl.BlockSpec((None, IN_TN, D_MODEL), lambda n, i: (l, n, 0)),
                  pl.BlockSpec((None, LANES, D_MODEL), lambda n, i: (l, (n + 1) * (IN_TN // LANES), 0))],
        out_specs=pl.BlockSpec((tm, IN_TN), lambda n, i: (i, _inproj_out_tile(n))),
        out_shape=jax.ShapeDtypeStruct((m, P_END), f32),
        scratch_shapes=[pltpu.VMEM((IN_TN, D_MODEL), bf16)],
        compiler_params=_params("arbitrary", "arbitrary"), name="inproj")(u, w_in_t, w_in_t)


def _mm_nt_body(a_ref, w_ref, o_ref):
    o_ref[...] = lax.dot_general(a_ref[...], w_ref[...].astype(bf16), (((1,), (1,)), ((), ())),
                                 preferred_element_type=f32)


def _matmul_nt(a, w_t, *, tm, name):
    m, k = a.shape
    n = w_t.shape[0]
    return pl.pallas_call(
        _mm_nt_body, grid=(m // tm,),
        in_specs=[pl.BlockSpec((tm, k), lambda i: (i, 0)),
                  pl.BlockSpec((n, k), lambda i: (0, 0))],
        out_specs=pl.BlockSpec((tm, n), lambda i: (i, 0)),
        out_shape=jax.ShapeDtypeStruct((m, n), f32),
        compiler_params=_params("arbitrary"), name=name)(a, w_t)


def _merge_body(oa_ref, ob_ref, ga_ref, gb_ref, wa_ref, wb_ref, o_ref, wabf_ref, wbbf_ref):
    @pl.when(pl.program_id(1) == 0)
    def _():
        wabf_ref[...] = wa_ref[...].astype(bf16)
        wbbf_ref[...] = wb_ref[...].astype(bf16)

    ya = jnp.dot(oa_ref[...], wabf_ref[...], preferred_element_type=f32)
    yb = jnp.dot(ob_ref[...], wbbf_ref[...], preferred_element_type=f32)
    o_ref[...] = (_sig(ga_ref[...]) * ya + _sig(gb_ref[...]) * yb).astype(o_ref.dtype)


def _merge(oa, ob, proj, wa, wb, l, *, tm, tn):
    m = oa.shape[0]
    ga0, gb0 = P_GA // tn, P_GB // tn
    return pl.pallas_call(
        _merge_body, grid=(D_MODEL // tn, m // tm),
        in_specs=[pl.BlockSpec((tm, GLA_VW), lambda j, i: (i, 0)),
                  pl.BlockSpec((tm, SSM_D_INNER), lambda j, i: (i, 0)),
                  pl.BlockSpec((tm, tn), lambda j, i: (i, ga0 + j)),
                  pl.BlockSpec((tm, tn), lambda j, i: (i, gb0 + j)),
                  pl.BlockSpec((None, GLA_VW, tn), lambda j, i: (l, 0, j)),
                  pl.BlockSpec((None, SSM_D_INNER, tn), lambda j, i: (l, 0, j))],
        out_specs=pl.BlockSpec((tm, tn), lambda j, i: (i, j)),
        out_shape=jax.ShapeDtypeStruct((m, D_MODEL), bf16),
        scratch_shapes=[pltpu.VMEM((GLA_VW, tn), bf16), pltpu.VMEM((SSM_D_INNER, tn), bf16)],
        compiler_params=_params("arbitrary", "arbitrary"), name="merge")(oa, ob, proj, proj, wa, wb)


def _route(logits):
    c = lax.broadcasted_iota(jnp.int32, logits.shape, 1)
    big = jnp.int32(1 << 20)
    is_group = c < MOE_GROUPS
    lg = jnp.where(is_group, logits, -jnp.inf)
    top = jnp.max(lg, axis=1, keepdims=True)
    gsel = jnp.min(jnp.where(is_group & (logits == top), c, big), axis=1, keepdims=True)
    p_sel = 1.0 / jnp.sum(jnp.exp(lg - top), axis=1, keepdims=True)
    in_group = ((c >= MOE_LANE0) & (c < MOE_LANE0 + MOE_EXPERTS)
                & (((c - MOE_LANE0) // MOE_PER_GROUP) == gsel))
    le = jnp.where(in_group, logits, -jnp.inf)
    tv0 = jnp.max(le, axis=1, keepdims=True)
    i0 = jnp.min(jnp.where(in_group & (logits == tv0), c, big), axis=1, keepdims=True)
    rest = in_group & (c != i0)
    le1 = jnp.where(rest, logits, -jnp.inf)
    tv1 = jnp.max(le1, axis=1, keepdims=True)
    i1 = jnp.min(jnp.where(rest & (logits == tv1), c, big), axis=1, keepdims=True)
    t = jnp.exp(tv1 - tv0)
    w0 = p_sel / (1.0 + t)
    w1 = p_sel * t / (1.0 + t)
    cw = jnp.where(c == i0, w0, jnp.where(c == i1, w1, 0.0))
    return cw, ((c == i0) | (c == i1)).astype(f32)


def _outproj_body(m_ref, wo_ref, x_ref, g_ref, wr_ref, br_ref, xn_ref, u_ref, cw_ref, oh_ref, cnt_ref,
                  wobf_ref):
    @pl.when(pl.program_id(0) == 0)
    def _():
        wobf_ref[...] = wo_ref[...].astype(bf16)
        cnt_ref[...] = jnp.zeros_like(cnt_ref)

    w_hi, w_lo = _split_bf16(wr_ref[...], 2)
    tm = x_ref.shape[0]
    parts = 2 if tm % 32 == 0 else 1
    for p in range(parts):
        rs = slice(p * (tm // parts), (p + 1) * (tm // parts))
        xn = x_ref[rs, :] + jnp.dot(m_ref[rs, :], wobf_ref[...], preferred_element_type=f32)
        xn_ref[rs, :] = xn
        u = xn * lax.rsqrt(jnp.mean(xn * xn, axis=-1, keepdims=True) + EPS) * g_ref[...]
        u_ref[rs, :] = u
        u_hi, u_lo = _split_bf16(u, 2)
        logits = (jnp.dot(u_hi, w_hi, preferred_element_type=f32)
                  + (jnp.dot(u_hi, w_lo, preferred_element_type=f32)
                     + jnp.dot(u_lo, w_hi, preferred_element_type=f32))) + br_ref[...]
        cw, oh = _route(logits)
        cw_ref[rs, :] = cw
        oh_ref[rs, :] = oh
        cnt_ref[...] += jnp.sum(oh, axis=0, keepdims=True)


def _outproj(merged, wo, l, x, g, wr, br, *, tm):
    m = x.shape[0]
    row = lambda i: (i, 0)
    fix = lambda i: (0, 0)
    return pl.pallas_call(
        _outproj_body, grid=(m // tm,),
        in_specs=[pl.BlockSpec((tm, D_MODEL), row),
                  pl.BlockSpec((None, D_MODEL, D_MODEL), lambda i: (l, 0, 0), pipeline_mode=pl.Buffered(1)),
                  pl.BlockSpec((tm, D_MODEL), row), pl.BlockSpec((1, D_MODEL), fix),
                  pl.BlockSpec((D_MODEL, LANES), fix), pl.BlockSpec((1, LANES), fix)],
        out_specs=[pl.BlockSpec((tm, D_MODEL), row), pl.BlockSpec((tm, D_MODEL), row),
                   pl.BlockSpec((tm, LANES), row), pl.BlockSpec((tm, LANES), row), pl.BlockSpec((1, LANES), fix)],
        out_shape=[jax.ShapeDtypeStruct((m, D_MODEL), f32), jax.ShapeDtypeStruct((m, D_MODEL), f32),
                   jax.ShapeDtypeStruct((m, LANES), f32), jax.ShapeDtypeStruct((m, LANES), f32),
                   jax.ShapeDtypeStruct((1, LANES), f32)],
        scratch_shapes=[pltpu.VMEM((D_MODEL, D_MODEL), bf16)],
        compiler_params=_params("arbitrary"), name="outproj")(merged, wo, x, g, wr, br)


def _plan_body(oh_ref, cw_ref, cnt_ref, dest_ref, wts_ref, te_ref, pad_ref, base_ref, carry_ref, *, rows, blk):
    lane = lax.broadcasted_iota(jnp.int32, (8, LANES), 1)
    is_expert = (lane >= MOE_LANE0) & (lane < MOE_LANE0 + MOE_EXPERTS)

    @pl.when(pl.program_id(0) == 0)
    def _():
        cnt = jnp.broadcast_to(cnt_ref[...], (8, LANES))
        padded = jnp.floor((cnt + (MOE_TILE - 1)) * (1.0 / MOE_TILE)) * MOE_TILE
        src = lax.broadcasted_iota(jnp.int32, (LANES, LANES), 0)
        dst = lax.broadcasted_iota(jnp.int32, (LANES, LANES), 1)
        base = _dot_exact01(padded, (src < dst).astype(bf16), 3)
        base_ref[...] = base[0:1]
        carry_ref[...] = jnp.zeros_like(carry_ref)
        end = base + padded
        total = jnp.max(jnp.where(is_expert, end, 0.0), axis=1, keepdims=True)
        end_col = jnp.transpose(jnp.where(is_expert, end, jnp.inf))[:, 0:1]
        tile_start = (lax.broadcasted_iota(jnp.int32, (LANES, LANES), 1) * MOE_TILE).astype(f32)
        tile_exp = jnp.sum((end_col <= tile_start).astype(f32), axis=0, keepdims=True)
        tile_exp = jnp.minimum(tile_exp, MOE_EXPERTS - 1.0)
        te_ref[0:1, :] = tile_exp.astype(jnp.int32)
        te_ref[1:2, :] = jnp.broadcast_to(total[0:1] * (1.0 / MOE_TILE), (1, LANES)).astype(jnp.int32)
        shift = (src == dst + MOE_LANE0).astype(bf16)
        is_tail = lane[0:1] == MOE_EXPERTS
        pad_lo = jnp.where(is_tail, total[0:1], _dot_exact01(base + cnt, shift, 3)[0:1])
        pad_hi = jnp.where(is_tail, float(rows), _dot_exact01(end, shift, 3)[0:1])
        pad_ref[0:1, :] = pad_lo.astype(jnp.int32)
        pad_ref[1:2, :] = pad_hi.astype(jnp.int32)

    n = blk
    row = lax.broadcasted_iota(jnp.int32, (n, n), 0)
    col = lax.broadcasted_iota(jnp.int32, (n, n), 1)
    earlier = (row > col).astype(bf16)
    lanes = lax.broadcasted_iota(jnp.int32, (n, LANES), 1)
    for s in range(oh_ref.shape[0] // blk):
        rs = slice(s * blk, (s + 1) * blk)
        oh = oh_ref[rs, :]
        rank = jnp.dot(earlier, oh.astype(bf16), preferred_element_type=f32)
        pos = base_ref[...] + carry_ref[...] + rank
        carry_ref[...] += jnp.sum(oh, axis=0, keepdims=True)
        chosen = oh > 0.0
        d_lo = jnp.min(jnp.where(chosen, pos, jnp.inf), axis=1, keepdims=True)
        d_hi = jnp.max(jnp.where(chosen, pos, -1.0), axis=1, keepdims=True)
        cw = cw_ref[rs, :]
        w_lo = jnp.sum(jnp.where(chosen & (pos == d_lo), cw, 0.0), axis=1, keepdims=True)
        w_hi = jnp.sum(jnp.where(chosen & (pos == d_hi), cw, 0.0), axis=1, keepdims=True)
        wts_ref[rs, :] = jnp.where(lanes == 0, w_lo, jnp.where(lanes == 1, w_hi, 0.0))
        dests = jnp.where(lanes == 0, d_lo, jnp.where(lanes == 1, d_hi, 0.0))
        dest_ref[s] = jnp.transpose(dests)[0:2, :].astype(jnp.int32)


def _token_blocks_per_step(nblk):
    return max(d for d in range(1, 6) if nblk % d == 0)


def _moe_plan(oh, cw, cnt, *, blk, rows):
    m = oh.shape[0]
    nblk = m // blk
    nsub = _token_blocks_per_step(nblk)
    fix = lambda i: (0, 0)
    return pl.pallas_call(
        functools.partial(_plan_body, rows=rows, blk=blk), grid=(nblk // nsub,),
        in_specs=[pl.BlockSpec((nsub * blk, LANES), lambda i: (i, 0)),
                  pl.BlockSpec((nsub * blk, LANES), lambda i: (i, 0)),
                  pl.BlockSpec((1, LANES), fix)],
        out_specs=[pl.BlockSpec((nsub, 2, blk), lambda i: (i, 0, 0)),
                   pl.BlockSpec((nsub * blk, LANES), lambda i: (i, 0)),
                   pl.BlockSpec((2, LANES), fix), pl.BlockSpec((2, LANES), fix)],
        out_shape=[jax.ShapeDtypeStruct((nblk, 2, blk), jnp.int32), jax.ShapeDtypeStruct((m, LANES), f32),
                   jax.ShapeDtypeStruct((2, LANES), jnp.int32), jax.ShapeDtypeStruct((2, LANES), jnp.int32)],
        scratch_shapes=[pltpu.VMEM((1, LANES), f32), pltpu.VMEM((1, LANES), f32)],
        compiler_params=_params("arbitrary"), name="moe_plan")(oh, cw, cnt)


def _row_copy(src_ref, s, dst_ref, d, sem):
    return pltpu.make_async_copy(src_ref.at[pl.ds(s, 1), :], dst_ref.at[pl.ds(d, 1), :], sem)


def _dispatch_body(dest_ref, pad_ref, u_ref, xs_ref, zero_ref, sem, *, blk, nsub):
    i = pl.program_id(0)

    def wait_rows(count, which):
        pltpu.make_async_copy(xs_ref.at[pl.ds(0, count), :], xs_ref.at[pl.ds(0, count), :], sem.at[which]).wait()

    @pl.when(i == 0)
    def _():
        zero_ref[...] = jnp.zeros_like(zero_ref)
        group = zero_ref.shape[0]

        def fill_row(r, c):
            _row_copy(zero_ref, 0, xs_ref, r, sem.at[1]).start()
            return c

        def fill_group(lo8):
            def f(r, c):
                row0 = pl.multiple_of(lo8 + r * group, group)
                pltpu.make_async_copy(zero_ref, xs_ref.at[pl.ds(row0, group), :], sem.at[1]).start()
                return c
            return f

        nrow = jnp.int32(0)
        ngroup = jnp.int32(0)
        for e in range(MOE_EXPERTS + 1):
            lo = pad_ref[0, e]
            hi = pad_ref[1, e]
            lo8 = jnp.minimum((lo + group - 1) // group * group, hi)
            lax.fori_loop(lo, lo8, fill_row, 0)
            lax.fori_loop(0, (hi - lo8) // group, fill_group(lo8), 0)
            nrow = nrow + (lo8 - lo)
            ngroup = ngroup + (hi - lo8) // group

        def wait_row(r, c):
            _row_copy(zero_ref, 0, xs_ref, 0, sem.at[1]).wait()
            return c
        lax.fori_loop(0, nrow, wait_row, 0)

        def wait_group(r, c):
            wait_rows(group, 1)
            return c
        lax.fori_loop(0, ngroup, wait_group, 0)

    for s in range(nsub):
        def send(j, c, s=s):
            _row_copy(u_ref, s * blk + j, xs_ref, dest_ref[s, 0, j], sem.at[0]).start()
            _row_copy(u_ref, s * blk + j, xs_ref, dest_ref[s, 1, j], sem.at[0]).start()
            return c
        lax.fori_loop(0, blk, send, 0)
    for _ in range(MOE_TOP_K * nsub):
        wait_rows(blk, 0)


def _moe_dispatch(u, dest, pad, *, rows, blk):
    m = u.shape[0]
    nblk = m // blk
    nsub = _token_blocks_per_step(nblk)
    return pl.pallas_call(
        functools.partial(_dispatch_body, blk=blk, nsub=nsub), grid=(nblk // nsub,),
        in_specs=[pl.BlockSpec((nsub, 2, blk), lambda i: (i, 0, 0), memory_space=pltpu.SMEM),
                  pl.BlockSpec(memory_space=pltpu.SMEM),
                  pl.BlockSpec((nsub * blk, D_MODEL), lambda i: (i, 0))],
        out_specs=pl.BlockSpec(memory_space=pl.ANY),
        out_shape=jax.ShapeDtypeStruct((rows, D_MODEL), f32),
        scratch_shapes=[pltpu.VMEM((8, D_MODEL), f32), pltpu.SemaphoreType.DMA((2,))],
        compiler_params=_params("arbitrary"), name="moe_dispatch")(dest, pad, u)


def _gmm_body(te_ref, xs_ref, wg_ref, wu_ref, wd_ref, ys_ref, wgb_ref, wub_ref, wdb_ref):
    i = pl.program_id(0)
    n_used = te_ref[1, 0]

    @pl.when(i < n_used)
    def _():
        @pl.when((i == 0) | (te_ref[0, i] != te_ref[0, jnp.maximum(i - 1, 0)]))
        def _():
            wgb_ref[...] = wg_ref[...].astype(bf16)
            wub_ref[...] = wu_ref[...].astype(bf16)
            wdb_ref[...] = wd_ref[...].astype(bf16)

        x = xs_ref[...].astype(bf16)
        hg = jnp.dot(x, wgb_ref[...], preferred_element_type=f32)
        hu = jnp.dot(x, wub_ref[...], preferred_element_type=f32)
        h = ((hg * _sig(hg)) * hu).astype(bf16)
        ys_ref[...] = jnp.dot(h, wdb_ref[...], preferred_element_type=f32)

    @pl.when(i >= n_used)
    def _():
        ys_ref[...] = jnp.zeros_like(ys_ref)


def _moe_gmm(te, xs, wg, wu, wd, l):
    rows = xs.shape[0]
    ntiles = rows // MOE_TILE
    tile = lambda i, te: (jnp.minimum(i, te[1, 0] - 1), 0)
    up = lambda i, te: (l, te[0, i], 0, 0)
    return pl.pallas_call(
        _gmm_body,
        grid_spec=pltpu.PrefetchScalarGridSpec(
            num_scalar_prefetch=1, grid=(ntiles,),
            in_specs=[pl.BlockSpec((MOE_TILE, D_MODEL), tile),
                      pl.BlockSpec((None, None, D_MODEL, MOE_D_FF), up),
                      pl.BlockSpec((None, None, D_MODEL, MOE_D_FF), up),
                      pl.BlockSpec((None, None, MOE_D_FF, D_MODEL), up)],
            out_specs=pl.BlockSpec((MOE_TILE, D_MODEL), lambda i, te: (i, 0)),
            scratch_shapes=[pltpu.VMEM((D_MODEL, MOE_D_FF), bf16), pltpu.VMEM((D_MODEL, MOE_D_FF), bf16),
                            pltpu.VMEM((MOE_D_FF, D_MODEL), bf16)]),
        out_shape=jax.ShapeDtypeStruct((rows, D_MODEL), f32),
        compiler_params=_params("arbitrary"), name="moe_gmm")(te, xs, wg, wu, wd)


def _combine_body(dcur_ref, dnext_ref, xn_ref, wts_ref, ys_ref, g_ref, o_ref, un_ref, ya_ref, yb_ref, sem,
                  *, blk, nsub, nsteps):
    i = pl.program_id(0)
    slot = i % 2

    def gather(dref, s):
        for b in range(nsub):
            def one(j, c, b=b):
                _row_copy(ys_ref, dref[b, 0, j], ya_ref.at[s], b * blk + j, sem.at[s]).start()
                _row_copy(ys_ref, dref[b, 1, j], yb_ref.at[s], b * blk + j, sem.at[s]).start()
                return c
            lax.fori_loop(0, blk, one, 0)

    @pl.when(i == 0)
    def _():
        gather(dcur_ref, 0)

    @pl.when(i + 1 < nsteps)
    def _():
        gather(dnext_ref, 1 - slot)

    for _ in range(MOE_TOP_K * nsub):
        pltpu.make_async_copy(ys_ref.at[pl.ds(0, blk), :], ya_ref.at[slot, pl.ds(0, blk), :],
                              sem.at[slot]).wait()
    w = wts_ref[...]
    x = xn_ref[...] + w[:, 0:1] * ya_ref[slot] + w[:, 1:2] * yb_ref[slot]
    o_ref[...] = x
    un_ref[...] = (x * lax.rsqrt(jnp.mean(x * x, axis=-1, keepdims=True) + EPS) * g_ref[...]).astype(bf16)


def _moe_combine(dest, xn, wts, ys, g_next, *, blk):
    m = xn.shape[0]
    nblk = m // blk
    nsub = _token_blocks_per_step(nblk)
    nsteps = nblk // nsub
    rows = nsub * blk
    return pl.pallas_call(
        functools.partial(_combine_body, blk=blk, nsub=nsub, nsteps=nsteps), grid=(nsteps,),
        in_specs=[pl.BlockSpec((nsub, 2, blk), lambda i: (i, 0, 0), memory_space=pltpu.SMEM),
                  pl.BlockSpec((nsub, 2, blk), lambda i: (jnp.minimum(i + 1, nsteps - 1), 0, 0),
                               memory_space=pltpu.SMEM),
                  pl.BlockSpec((rows, D_MODEL), lambda i: (i, 0)),
                  pl.BlockSpec((rows, LANES), lambda i: (i, 0)),
                  pl.BlockSpec(memory_space=pl.ANY),
                  pl.BlockSpec((1, D_MODEL), lambda i: (0, 0))],
        out_specs=[pl.BlockSpec((rows, D_MODEL), lambda i: (i, 0)), pl.BlockSpec((rows, D_MODEL), lambda i: (i, 0))],
        out_shape=[jax.ShapeDtypeStruct((m, D_MODEL), f32), jax.ShapeDtypeStruct((m, D_MODEL), bf16)],
        scratch_shapes=[pltpu.VMEM((2, rows, D_MODEL), f32), pltpu.VMEM((2, rows, D_MODEL), f32),
                        pltpu.SemaphoreType.DMA((2,))],
        compiler_params=_params("arbitrary"), name="moe_combine")(dest, dest, xn, wts, ys, g_next)


def _moe(u, cw, oh, cnt, xn, wg, wu, wd, l, g_next):
    m = u.shape[0]
    blk = LANES
    rows = -(-(MOE_TOP_K * m + MOE_EXPERTS * (MOE_TILE - 1)) // MOE_TILE) * MOE_TILE
    assert rows // MOE_TILE <= LANES
    dest, wts, te, pad = _moe_plan(oh, cw, cnt, blk=blk, rows=rows)
    xs = _moe_dispatch(u, dest, pad, rows=rows, blk=blk)
    ys = _moe_gmm(te, xs, wg, wu, wd, l)
    return _moe_combine(dest, xn, wts, ys, g_next.reshape(1, D_MODEL), blk=blk)


def _gla_body(q_ref, k_ref, v_ref, r_ref, sm_ref, w2_ref, bg_ref, gn_ref, s0_ref, prev_ref,
              o_ref, sout_ref, st_ref, b_ref, att_ref, *, chunk, nsteps):
    del prev_ref
    t = pl.program_id(1)

    @pl.when(t == 0)
    def _():
        st_ref[...] = s0_ref[0]

    for c0 in range(0, q_ref.shape[0], chunk):
        _gla_chunk(q_ref, k_ref, v_ref, r_ref, sm_ref, w2_ref, bg_ref, gn_ref, o_ref, st_ref, b_ref, att_ref,
                   c0=c0, chunk=chunk)

    @pl.when(t == nsteps - 1)
    def _():
        sout_ref[0] = st_ref[...]


def _gla_chunk(q_ref, k_ref, v_ref, r_ref, sm_ref, w2_ref, bg_ref, gn_ref, o_ref, st_ref, b_ref, att_ref,
               *, c0, chunk):
    cr = slice(c0, c0 + chunk)
    glr = sm_ref[cr, S_GLR:S_GLR + GLA_GATE_RANK].astype(bf16)
    pre = jnp.dot(glr, w2_ref[...].astype(bf16), preferred_element_type=f32) + bg_ref[...]
    b_ref[...] = _cumsum_rows(_log_sigmoid(pre) * (1.0 / GLA_GATE_TAU))

    sub_row = lax.broadcasted_iota(jnp.int32, (GLA_SUB, GLA_DK), 0)
    sub_col = lax.broadcasted_iota(jnp.int32, (GLA_SUB, GLA_SUB), 1)
    nt = (((1,), (1,)), ((), ()))
    tn = (((0,), (0,)), ((), ()))
    for h in range(GLA_HEADS):
        ks = slice(h * GLA_DK, (h + 1) * GLA_DK)
        vs = slice(h * GLA_DV, (h + 1) * GLA_DV)
        q = q_ref[cr, ks] * (GLA_DK ** -0.5)
        k = k_ref[cr, ks]
        vb = v_ref[cr, vs].astype(bf16)
        b = b_ref[:, ks]
        b_last = b_ref[chunk - 1:chunk, ks]
        state = st_ref[h]
        o = jnp.dot((q * jnp.exp(b)).astype(bf16), state.astype(bf16), preferred_element_type=f32)

        att_ref[h] = jnp.zeros((chunk, chunk), f32)
        for blk in range(chunk // GLA_SUB):
            r0 = blk * GLA_SUB
            q_i = q[r0:r0 + GLA_SUB]
            b_i = b[r0:r0 + GLA_SUB]
            if blk > 0:
                c_i = b_ref[r0 - 1:r0, ks]
                q_dec = (q_i * jnp.exp(b_i - c_i)).astype(bf16)
                k_dec = (k[:r0] * jnp.exp(c_i - b[:r0])).astype(bf16)
                att_ref[h, r0:r0 + GLA_SUB, 0:r0] = lax.dot_general(q_dec, k_dec, nt,
                                                                   preferred_element_type=f32)
            diag = jnp.zeros((GLA_SUB, GLA_SUB), f32)
            for j in range(GLA_SUB):
                b_j = b_ref[r0 + j:r0 + j + 1, ks]
                k_j = k_ref[c0 + r0 + j:c0 + r0 + j + 1, ks]
                decay = jnp.exp(jnp.where(sub_row >= j, b_i - b_j, -jnp.inf))
                a_j = jnp.sum(q_i * decay * k_j, axis=1, keepdims=True)
                diag = jnp.where(sub_col == j, a_j, diag)
            att_ref[h, r0:r0 + GLA_SUB, r0:r0 + GLA_SUB] = diag

        o = o + jnp.dot(att_ref[h].astype(bf16), vb, preferred_element_type=f32)

        k_end = (k * jnp.exp(b_last - b)).astype(bf16)
        upd = lax.dot_general(k_end, vb, tn, preferred_element_type=f32)
        dec_col = jnp.transpose(jnp.broadcast_to(jnp.exp(b_last), (8, GLA_DK)))[:, 0:1]
        st_ref[h] = state * dec_col + upd

        o = o * lax.rsqrt(jnp.mean(o * o, axis=1, keepdims=True) + EPS) * gn_ref[...]
        r = r_ref[cr, vs]
        o_ref[cr, vs] = (o * (r * _sig(r))).astype(o_ref.dtype)


def _gla(proj, small, w2, bg, gn, l, s0, l_s0, prev, *, row0, nseq, seqlen, chunk):
    nchunks = seqlen // chunk
    cps = 2 if nchunks % 2 == 0 else 1
    blk = cps * chunk
    nsteps = nchunks // cps
    blk0 = row0 // blk
    rows = lambda b, t: blk0 + b * nsteps + t
    qk, vw = GLA_QK, GLA_VW
    hd = (GLA_HEADS, GLA_DK, GLA_DV)
    return pl.pallas_call(
        functools.partial(_gla_body, chunk=chunk, nsteps=nsteps),
        grid=(nseq, nsteps),
        in_specs=[pl.BlockSpec((blk, qk), lambda b, t: (rows(b, t), P_Q // qk)),
                  pl.BlockSpec((blk, qk), lambda b, t: (rows(b, t), P_K // qk)),
                  pl.BlockSpec((blk, vw), lambda b, t: (rows(b, t), P_V // vw)),
                  pl.BlockSpec((blk, vw), lambda b, t: (rows(b, t), P_R // vw)),
                  pl.BlockSpec((blk, LANES), lambda b, t: (rows(b, t), 0)),
                  pl.BlockSpec((None, GLA_GATE_RANK, qk), lambda b, t: (l, 0, 0)),
                  pl.BlockSpec((None, 1, qk), lambda b, t: (l, 0, 0)),
                  pl.BlockSpec((None, 1, GLA_DV), lambda b, t: (l, 0, 0)),
                  pl.BlockSpec((None, 1) + hd, lambda b, t: (l_s0, b, 0, 0, 0)),
                  pl.BlockSpec(memory_space=pl.ANY)],
        out_specs=[pl.BlockSpec((blk, vw), lambda b, t: (rows(b, t), 0)),
                   pl.BlockSpec((1,) + hd, lambda b, t: (b, 0, 0, 0))],
        out_shape=[jax.ShapeDtypeStruct(prev.shape, bf16),
                   jax.ShapeDtypeStruct((nseq,) + hd, f32)],
        scratch_shapes=[pltpu.VMEM(hd, f32), pltpu.VMEM((chunk, qk), f32),
                        pltpu.VMEM((GLA_HEADS, chunk, chunk), f32)],
        input_output_aliases={9: 0},
        compiler_params=_params("arbitrary", "arbitrary"),
        name="gla")(proj, proj, proj, proj, small, w2, bg, gn, s0, prev)


def _ssd_body(z_ref, x_ref, bm_ref, cm_ref, sm_ref, wx_ref, wb_ref, wc_ref, bx_ref, bb_ref, bc_ref,
              sx_ref, sb_ref, sc_ref, dtb_ref, alog_ref, dskip_ref, nrm_ref, h0_ref, prev_ref,
              y_ref, hout_ref, cxo_ref, cbo_ref, cco_ref,
              h_ref, xpx_ref, xpb_ref, xpc_ref, xs_ref, *, chunk, nchunks):
    del prev_ref
    t = pl.program_id(1)
    hist = CONV_PAD - (SSM_CONV - 1)

    @pl.when(t == 0)
    def _():
        h_ref[...] = h0_ref[0]
        xpx_ref[hist:CONV_PAD, :] = sx_ref[0]
        xpb_ref[hist:CONV_PAD, :] = sb_ref[0]
        xpc_ref[hist:CONV_PAD, :] = sc_ref[0]

    def conv_silu(xp_ref, cur_ref, w_ref, bias_ref):
        x = cur_ref[...]
        taps = SSM_CONV - 1
        acc = bias_ref[...] + x * w_ref[taps:taps + 1, :]
        for s in range(1, SSM_CONV):
            acc = acc + pltpu.roll(x, s, axis=0) * w_ref[taps - s:taps - s + 1, :]
        xp_ref[CONV_PAD:2 * CONV_PAD, :] = x[0:CONV_PAD]
        head = bias_ref[...] + xp_ref[pl.ds(hist, CONV_PAD), :] * w_ref[0:1, :]
        for i in range(1, SSM_CONV):
            head = head + xp_ref[pl.ds(hist + i, CONV_PAD), :] * w_ref[i:i + 1, :]
        xp_ref[hist:CONV_PAD, :] = x[chunk - taps:chunk]
        acc = jnp.concatenate([head, acc[CONV_PAD:]], axis=0)
        return acc * _sig(acc)

    xs_ref[...] = conv_silu(xpx_ref, x_ref, wx_ref, bx_ref)
    bmat = conv_silu(xpb_ref, bm_ref, wb_ref, bb_ref).astype(bf16)
    cmat = conv_silu(xpc_ref, cm_ref, wc_ref, bc_ref).astype(bf16)

    @pl.when(t == nchunks - 1)
    def _():
        cxo_ref[0] = xpx_ref[hist:CONV_PAD, :]
        cbo_ref[0] = xpb_ref[hist:CONV_PAD, :]
        cco_ref[0] = xpc_ref[hist:CONV_PAD, :]

    dt = _softplus(sm_ref[...] + dtb_ref[...])
    cum = _cumsum_rows(dt * -jnp.exp(alog_ref[...]))
    last = cum[chunk - 1:chunk, :]
    w_end = jnp.exp(last - cum) * dt
    e_cum = jnp.exp(cum)
    e_last = jnp.exp(last)
    cum2 = cum * LOG2E
    adj2_t = ((cum - jnp.log(dt)) * LOG2E).T

    row = lax.broadcasted_iota(jnp.int32, (chunk, chunk), 0)
    col = lax.broadcasted_iota(jnp.int32, (chunk, chunk), 1)
    causal = row >= col
    head = lax.broadcasted_iota(jnp.int32, (LANES, SSM_GROUP_W), 0)
    chan = lax.broadcasted_iota(jnp.int32, (LANES, SSM_GROUP_W), 1)
    nt = (((1,), (1,)), ((), ()))
    tn = (((0,), (0,)), ((), ()))
    for g in range(SSM_GROUPS):
        cs = slice(g * SSM_GROUP_W, (g + 1) * SSM_GROUP_W)
        ns = slice(g * SSM_D_STATE, (g + 1) * SSM_D_STATE)
        xs = xs_ref[:, cs]
        xsb = xs.astype(bf16)
        b_g = bmat[:, ns]
        c_g = cmat[:, ns]
        widen = (head == g * SSM_HPG + chan // SSM_HEAD_DIM).astype(bf16)
        w_full = _dot_exact01(w_end, widen, 2)
        ecum_full = _dot_exact01(e_cum, widen, 2)

        cb = lax.dot_general(c_g, b_g, nt, preferred_element_type=f32)
        state = h_ref[g]
        y = lax.dot_general(c_g, state.astype(bf16), nt, preferred_element_type=f32) * ecum_full
        parts = []
        for j in range(SSM_HPG):
            hd = g * SSM_HPG + j
            seg = jnp.exp2(jnp.where(causal, cum2[:, hd:hd + 1] - adj2_t[hd:hd + 1, :], -jnp.inf))
            parts.append(jnp.dot((cb * seg).astype(bf16), xsb[:, j * SSM_HEAD_DIM:(j + 1) * SSM_HEAD_DIM],
                                 preferred_element_type=f32))
        y = y + jnp.concatenate(parts, axis=1)

        upd = lax.dot_general((xs * w_full).astype(bf16), b_g, tn, preferred_element_type=f32)
        for j in range(SSM_HPG):
            hd = g * SSM_HPG + j
            rs = slice(j * SSM_HEAD_DIM, (j + 1) * SSM_HEAD_DIM)
            h_ref[g, rs, :] = state[rs, :] * e_last[:, hd:hd + 1] + upd[rs, :]

        y = y + dskip_ref[:, cs] * xs
        z = z_ref[:, cs]
        y = y * (z * _sig(z))
        y = y * lax.rsqrt(jnp.mean(y * y, axis=1, keepdims=True) + EPS) * nrm_ref[:, cs]
        y_ref[:, cs] = y.astype(y_ref.dtype)

    @pl.when(t == nchunks - 1)
    def _():
        hout_ref[0] = h_ref[...]


def _ssd(proj, small, conv_w, conv_b, l, conv_s, l_cs, dtb, alog, dskip, nrm, h0, l_h0, prev,
         *, row0, nseq, seqlen, chunk):
    nchunks = seqlen // chunk
    blk0 = row0 // chunk
    di, bc = SSM_D_INNER, SSM_BC
    rows = lambda b, t: blk0 + b * nchunks + t
    taps, hist = SSM_CONV, SSM_CONV - 1
    xb, bb, cb = 0, di // bc, di // bc + 1
    hs = (SSM_GROUPS, SSM_GROUP_W, SSM_D_STATE)
    fix3 = lambda b, t: (l, 0, 0)
    in_specs = [
        pl.BlockSpec((chunk, di), lambda b, t: (rows(b, t), P_Z // di)),
        pl.BlockSpec((chunk, di), lambda b, t: (rows(b, t), P_X // di)),
        pl.BlockSpec((chunk, bc), lambda b, t: (rows(b, t), P_B // bc)),
        pl.BlockSpec((chunk, bc), lambda b, t: (rows(b, t), P_C // bc)),
        pl.BlockSpec((chunk, LANES), lambda b, t: (rows(b, t), 0)),
        pl.BlockSpec((None, taps, di), lambda b, t: (l, 0, xb)),
        pl.BlockSpec((None, taps, bc), lambda b, t: (l, 0, bb)),
        pl.BlockSpec((None, taps, bc), lambda b, t: (l, 0, cb)),
        pl.BlockSpec((None, 1, di), lambda b, t: (l, 0, xb)),
        pl.BlockSpec((None, 1, bc), lambda b, t: (l, 0, bb)),
        pl.BlockSpec((None, 1, bc), lambda b, t: (l, 0, cb)),
        pl.BlockSpec((None, 1, hist, di), lambda b, t: (l_cs, b, 0, xb)),
        pl.BlockSpec((None, 1, hist, bc), lambda b, t: (l_cs, b, 0, bb)),
        pl.BlockSpec((None, 1, hist, bc), lambda b, t: (l_cs, b, 0, cb)),
        pl.BlockSpec((None, 1, LANES), fix3),
        pl.BlockSpec((None, 1, LANES), fix3),
        pl.BlockSpec((None, 1, di), fix3),
        pl.BlockSpec((None, 1, di), fix3),
        pl.BlockSpec((None, 1) + hs, lambda b, t: (l_h0, b, 0, 0, 0)),
        pl.BlockSpec(memory_space=pl.ANY),
    ]
    return pl.pallas_call(
        functools.partial(_ssd_body, chunk=chunk, nchunks=nchunks),
        grid=(nseq, nchunks),
        in_specs=in_specs,
        out_specs=[pl.BlockSpec((chunk, di), lambda b, t: (rows(b, t), 0)),
                   pl.BlockSpec((1,) + hs, lambda b, t: (b, 0, 0, 0)),
                   pl.BlockSpec((1, hist, di), lambda b, t: (b, 0, 0)),
                   pl.BlockSpec((1, hist, bc), lambda b, t: (b, 0, 0)),
                   pl.BlockSpec((1, hist, bc), lambda b, t: (b, 0, 0))],
        out_shape=[jax.ShapeDtypeStruct(prev.shape, bf16),
                   jax.ShapeDtypeStruct((nseq,) + hs, f32),
                   jax.ShapeDtypeStruct((nseq, hist, di), f32),
                   jax.ShapeDtypeStruct((nseq, hist, bc), f32),
                   jax.ShapeDtypeStruct((nseq, hist, bc), f32)],
        scratch_shapes=[pltpu.VMEM(hs, f32), pltpu.VMEM((2 * CONV_PAD, di), f32),
                        pltpu.VMEM((2 * CONV_PAD, bc), f32), pltpu.VMEM((2 * CONV_PAD, bc), f32),
                        pltpu.VMEM((chunk, di), f32)],
        input_output_aliases={19: 0},
        compiler_params=_params("arbitrary", "arbitrary"),
        name="ssd")(proj, proj, proj, proj, small, conv_w, conv_w, conv_w, conv_b, conv_b, conv_b,
                    conv_s, conv_s, conv_s, dtb, alog, dskip, nrm, h0, prev)


def _pad_lanes(v):
    return jnp.pad(v, ((0, 0), (0, LANES - v.shape[-1])))[:, None, :]


def kernel(x_prompt, x_sample, state_gla, state_ssm, state_conv, norm_mix, w_in, gla_w_gate2, gla_b_gate,
           gla_norm, gla_w_out, ssm_conv_w, ssm_conv_b, ssm_dt_bias, ssm_a_log, ssm_d, ssm_norm, ssm_w_out,
           w_o, norm_ffn, router_w1, router_b1, router_w2, router_b2, exp_w_gate, exp_w_up, exp_w_down,
           norm_final):
    n_p, t_p = x_prompt.shape[:2]
    n_s, t_s = x_sample.shape[:2]
    assert n_p == 1
    rows_p, rows_s = n_p * t_p, n_s * t_s
    m = rows_p + rows_s
    tm = 832 if m % 832 == 0 else 128
    tm_out = 416 if m % 416 == 0 else 128
    gla_chunk_p, ssd_chunk_p = min(64, t_p), min(256, t_p)

    x = jnp.concatenate([x_prompt.reshape(rows_p, D_MODEL), x_sample.reshape(rows_s, D_MODEL)], axis=0)
    zeros_gla = jnp.zeros((1, n_p, GLA_HEADS, GLA_DK, GLA_DV), f32)
    zeros_ssm = jnp.zeros((1, n_p, SSM_GROUPS, SSM_GROUP_W, SSM_D_STATE), f32)
    zeros_conv = jnp.zeros((1, n_p, SSM_CONV - 1, SSM_CONV_DIM), f32)
    state_ssm_g = state_ssm.reshape(DEPTH, n_s, SSM_GROUPS, SSM_GROUP_W, SSM_D_STATE)

    bg3 = gla_b_gate[:, None, :]
    gn3 = gla_norm[:, None, :]
    conv_b3 = ssm_conv_b[:, None, :]
    dtb3 = _pad_lanes(ssm_dt_bias)
    alog3 = _pad_lanes(ssm_a_log)
    dskip3 = jnp.repeat(ssm_d, SSM_HEAD_DIM, axis=1)[:, None, :]
    nrm3 = ssm_norm[:, None, :]

    w_in_t = jnp.swapaxes(w_in, 1, 2)
    oa = jnp.zeros((m, GLA_VW), bf16)
    ob = jnp.zeros((m, SSM_D_INNER), bf16)

    outs = {k: [] for k in ("gla_p", "ssm_p", "conv_p", "gla_s", "ssm_s", "conv_s")}
    for l in range(DEPTH):
        w_small_t = jnp.concatenate(
            [w_in_t[l, W_DT:W_MG], w_in_t[l, W_GLR:W_Z],
             jnp.zeros((LANES - SSM_HEADS - GLA_GATE_RANK, D_MODEL), f32)], axis=0)

        if l == 0:
            u = _rmsnorm(x, norm_mix[l], rows=m, row0=0, tm=tm, out_dtype=bf16)
        proj = _inproj(u, w_in_t, l, tm=tm)
        small = _matmul_nt(u, w_small_t, tm=tm, name="inproj_small")

        gla_args = (proj, small, gla_w_gate2, bg3, gn3, l)
        oa, g_p = _gla(*gla_args, zeros_gla, 0, oa, row0=0, nseq=n_p, seqlen=t_p, chunk=gla_chunk_p)
        oa, g_s = _gla(*gla_args, state_gla, l, oa, row0=rows_p, nseq=n_s, seqlen=t_s, chunk=t_s)

        ssd_args = (proj, small, ssm_conv_w, conv_b3, l)
        ssd_tail = (dtb3, alog3, dskip3, nrm3)
        ob, h_p, *conv_p = _ssd(*ssd_args, zeros_conv, 0, *ssd_tail, zeros_ssm, 0, ob,
                                row0=0, nseq=n_p, seqlen=t_p, chunk=ssd_chunk_p)
        ob, h_s, *conv_s = _ssd(*ssd_args, state_conv, l, *ssd_tail, state_ssm_g, l, ob,
                                row0=rows_p, nseq=n_s, seqlen=t_s, chunk=t_s)

        merged = _merge(oa, ob, proj, gla_w_out, ssm_w_out, l, tm=tm_out, tn=512)

        wr = jnp.concatenate([router_w1[l], router_w2[l],
                              jnp.zeros((D_MODEL, LANES - MOE_GROUPS - MOE_EXPERTS), f32)], axis=1)
        br = jnp.pad(jnp.concatenate([router_b1[l], router_b2[l]]),
                     (0, LANES - MOE_GROUPS - MOE_EXPERTS)).reshape(1, LANES)
        xn, u2, cw, oh, cnt = _outproj(merged, w_o, l, x, norm_ffn[l].reshape(1, -1), wr, br, tm=tm_out)
        x, u = _moe(u2, cw, oh, cnt, xn, exp_w_gate, exp_w_up, exp_w_down, l, norm_mix[(l + 1) % DEPTH])

        outs["gla_p"].append(g_p)
        outs["gla_s"].append(g_s)
        outs["ssm_p"].append(h_p.reshape(n_p, SSM_HEADS, SSM_HEAD_DIM, SSM_D_STATE))
        outs["ssm_s"].append(h_s.reshape(n_s, SSM_HEADS, SSM_HEAD_DIM, SSM_D_STATE))
        outs["conv_p"].append(jnp.concatenate(conv_p, axis=-1))
        outs["conv_s"].append(jnp.concatenate(conv_s, axis=-1))

    tm_p = 512 if rows_p % 512 == 0 else 128
    tm_s = rows_s if rows_p % rows_s == 0 else 8
    y_p = _rmsnorm(x, norm_final, rows=rows_p, row0=0, tm=tm_p, out_dtype=f32)
    y_s = _rmsnorm(x, norm_final, rows=rows_s, row0=rows_p, tm=tm_s, out_dtype=f32)
    return (y_p.reshape(n_p, t_p, D_MODEL), y_s.reshape(n_s, t_s, D_MODEL),
            jnp.stack(outs["gla_p"]), jnp.stack(outs["ssm_p"]), jnp.stack(outs["conv_p"]),
            jnp.stack(outs["gla_s"]), jnp.stack(outs["ssm_s"]), jnp.stack(outs["conv_s"]))
```

```python
import functools
import math

import jax
import jax.numpy as jnp
from jax import lax
from jax.experimental import pallas as pl
from jax.experimental.pallas import tpu as pltpu

f32 = jnp.float32
bf16 = jnp.bfloat16
HIGHEST = lax.Precision.HIGHEST
LOG2E = math.log2(math.e)

D_MODEL = 2048
DEPTH = 2
EPS = 1e-6
GLA_HEADS = 4
GLA_DK = 256
GLA_DV = 512
GLA_QK = GLA_HEADS * GLA_DK
GLA_VW = GLA_HEADS * GLA_DV
GLA_GATE_RANK = 16
GLA_GATE_TAU = 16.0
GLA_SUB = 16
SSM_D_INNER = 4096
SSM_HEAD_DIM = 64
SSM_HEADS = 64
SSM_GROUPS = 8
SSM_HPG = 8
SSM_GROUP_W = SSM_HPG * SSM_HEAD_DIM
SSM_D_STATE = 128
SSM_BC = SSM_GROUPS * SSM_D_STATE
SSM_CONV = 4
SSM_CONV_DIM = 6144
MOE_GROUPS = 4
MOE_PER_GROUP = 4
MOE_EXPERTS = 16
MOE_D_FF = 512
MOE_TOP_K = 2
MOE_LANE0 = MOE_GROUPS
MOE_TILE = 256
LANES = 128
CONV_PAD = 8

W_GLR = 6144
W_Z = 6160
W_DT = 16400
W_MG = 16464
IN_TN = 1024
IN_TILES = 20480 // IN_TN
IN_ALIGNED_TILES = W_GLR // IN_TN
IN_MG_TILE0 = (W_DT - W_Z + W_GLR) // IN_TN
IN_SHIFT_Z = W_Z - W_GLR
IN_SHIFT_MG = W_MG - IN_MG_TILE0 * IN_TN
P_Z, P_X, P_Q, P_K, P_V, P_R, P_B, P_C, P_GA, P_GB, P_END = (
    0, 4096, 8192, 9216, 10240, 12288, 14336, 15360, 16384, 18432, 20480)
S_GLR = 64

VMEM_LIMIT = 56 * 1024 * 1024


def _sig(x):
    return 1.0 / (1.0 + jnp.exp(-x))


def _softplus(x):
    return jnp.maximum(x, 0.0) + jnp.log1p(jnp.exp(-jnp.abs(x)))


def _log_sigmoid(x):
    return jnp.minimum(x, 0.0) - jnp.log1p(jnp.exp(-jnp.abs(x)))


def _params(*sem):
    return pltpu.CompilerParams(dimension_semantics=sem, vmem_limit_bytes=VMEM_LIMIT)


def _split_bf16(x, pieces):
    out = []
    for _ in range(pieces - 1):
        p = x.astype(bf16)
        out.append(p)
        x = x - p.astype(f32)
    out.append(x.astype(bf16))
    return out


def _dot_exact01(a, b01, pieces, dims=None):
    acc = None
    for p in _split_bf16(a, pieces):
        if dims is None:
            d = jnp.dot(p, b01, preferred_element_type=f32)
        else:
            d = lax.dot_general(p, b01, dims, preferred_element_type=f32)
        acc = d if acc is None else acc + d
    return acc


def _cumsum_rows(x):
    n = x.shape[0]
    row = lax.broadcasted_iota(jnp.int32, (n, n), 0)
    col = lax.broadcasted_iota(jnp.int32, (n, n), 1)
    tri = (row >= col).astype(bf16)
    acc = None
    for p in _split_bf16(x, 3):
        d = jnp.dot(tri, p, preferred_element_type=f32)
        acc = d if acc is None else acc + d
    return acc


def _norm_body(x_ref, g_ref, o_ref):
    x = x_ref[...]
    y = x * lax.rsqrt(jnp.mean(x * x, axis=-1, keepdims=True) + EPS)
    o_ref[...] = (y * g_ref[...]).astype(o_ref.dtype)


def _rmsnorm(x, g, *, rows, row0, tm, out_dtype):
    blk0 = row0 // tm
    return pl.pallas_call(
        _norm_body, grid=(rows // tm,),
        in_specs=[pl.BlockSpec((tm, D_MODEL), lambda m: (blk0 + m, 0)),
                  pl.BlockSpec((1, D_MODEL), lambda m: (0, 0))],
        out_specs=pl.BlockSpec((tm, D_MODEL), lambda m: (m, 0)),
        out_shape=jax.ShapeDtypeStruct((rows, D_MODEL), out_dtype),
        compiler_params=_params("arbitrary"), name="rmsnorm")(x, g.reshape(1, D_MODEL))


def _inproj_body(u_ref, wa_ref, wb_ref, o_ref, wbf_ref):
    n = pl.program_id(0)

    @pl.when(pl.program_id(1) == 0)
    def _():
        def fill(shift):
            def f():
                if shift == 0:
                    wbf_ref[...] = wa_ref[...].astype(bf16)
                else:
                    wbf_ref[:IN_TN - shift, :] = wa_ref[shift:, :].astype(bf16)
                    wbf_ref[IN_TN - shift:, :] = wb_ref[:shift, :].astype(bf16)
            return f
        pl.when(n < IN_ALIGNED_TILES)(fill(0))
        pl.when((n >= IN_ALIGNED_TILES) & (n < IN_MG_TILE0))(fill(IN_SHIFT_Z))
        pl.when(n >= IN_MG_TILE0)(fill(IN_SHIFT_MG))

    o_ref[...] = lax.dot_general(u_ref[...], wbf_ref[...], (((1,), (1,)), ((), ())),
                                 preferred_element_type=f32)


def _inproj_out_tile(n):
    lead = IN_ALIGNED_TILES
    zx = (P_Q - P_Z) // IN_TN
    return jnp.where(n < lead, n + zx, jnp.where(n < lead + zx, n - lead, n))


def _inproj(u, w_in_t, l, *, tm):
    m = u.shape[0]
    return pl.pallas_call(
        _inproj_body, grid=(IN_TILES, m // tm),
        in_specs=[pl.BlockSpec((tm, D_MODEL), lambda n, i: (i, 0)),
                  pl.BlockSpec((None, IN_TN, D_MODEL), lambda n, i: (l, n, 0)),
                  pl.BlockSpec((None, LANES, D_MODEL), lambda n, i: (l, (n + 1) * (IN_TN // LANES), 0))],
        out_specs=pl.BlockSpec((tm, IN_TN), lambda n, i: (i, _inproj_out_tile(n))),
        out_shape=jax.ShapeDtypeStruct((m, P_END), f32),
        scratch_shapes=[pltpu.VMEM((IN_TN, D_MODEL), bf16)],
        compiler_params=_params("arbitrary", "arbitrary"), name="inproj")(u, w_in_t, w_in_t)


def _mm_nt_body(a_ref, w_ref, o_ref):
    o_ref[...] = lax.dot_general(a_ref[...], w_ref[...].astype(bf16), (((1,), (1,)), ((), ())),
                                 preferred_element_type=f32)


def _matmul_nt(a, w_t, *, tm, name):
    m, k = a.shape
    n = w_t.shape[0]
    return pl.pallas_call(
        _mm_nt_body, grid=(m // tm,),
        in_specs=[pl.BlockSpec((tm, k), lambda i: (i, 0)),
                  pl.BlockSpec((n, k), lambda i: (0, 0))],
        out_specs=pl.BlockSpec((tm, n), lambda i: (i, 0)),
        out_shape=jax.ShapeDtypeStruct((m, n), f32),
        compiler_params=_params("arbitrary"), name=name)(a, w_t)


def _merge_body(oa_ref, ob_ref, ga_ref, gb_ref, wa_ref, wb_ref, o_ref, wabf_ref, wbbf_ref):
    @pl.when(pl.program_id(1) == 0)
    def _():
        wabf_ref[...] = wa_ref[...].astype(bf16)
        wbbf_ref[...] = wb_ref[...].astype(bf16)

    ya = jnp.dot(oa_ref[...], wabf_ref[...], preferred_element_type=f32)
    yb = jnp.dot(ob_ref[...], wbbf_ref[...], preferred_element_type=f32)
    o_ref[...] = (_sig(ga_ref[...]) * ya + _sig(gb_ref[...]) * yb).astype(o_ref.dtype)


def _merge(oa, ob, proj, wa, wb, l, *, tm, tn):
    m = oa.shape[0]
    ga0, gb0 = P_GA // tn, P_GB // tn
    return pl.pallas_call(
        _merge_body, grid=(D_MODEL // tn, m // tm),
        in_specs=[pl.BlockSpec((tm, GLA_VW), lambda j, i: (i, 0)),
                  pl.BlockSpec((tm, SSM_D_INNER), lambda j, i: (i, 0)),
                  pl.BlockSpec((tm, tn), lambda j, i: (i, ga0 + j)),
                  pl.BlockSpec((tm, tn), lambda j, i: (i, gb0 + j)),
                  pl.BlockSpec((None, GLA_VW, tn), lambda j, i: (l, 0, j)),
                  pl.BlockSpec((None, SSM_D_INNER, tn), lambda j, i: (l, 0, j))],
        out_specs=pl.BlockSpec((tm, tn), lambda j, i: (i, j)),
        out_shape=jax.ShapeDtypeStruct((m, D_MODEL), bf16),
        scratch_shapes=[pltpu.VMEM((GLA_VW, tn), bf16), pltpu.VMEM((SSM_D_INNER, tn), bf16)],
        compiler_params=_params("arbitrary", "arbitrary"), name="merge")(oa, ob, proj, proj, wa, wb)


def _route(logits):
    c = lax.broadcasted_iota(jnp.int32, logits.shape, 1)
    big = jnp.int32(1 << 20)
    is_group = c < MOE_GROUPS
    lg = jnp.where(is_group, logits, -jnp.inf)
    top = jnp.max(lg, axis=1, keepdims=True)
    gsel = jnp.min(jnp.where(is_group & (logits == top), c, big), axis=1, keepdims=True)
    p_sel = 1.0 / jnp.sum(jnp.exp(lg - top), axis=1, keepdims=True)
    in_group = ((c >= MOE_LANE0) & (c < MOE_LANE0 + MOE_EXPERTS)
                & (((c - MOE_LANE0) // MOE_PER_GROUP) == gsel))
    le = jnp.where(in_group, logits, -jnp.inf)
    tv0 = jnp.max(le, axis=1, keepdims=True)
    i0 = jnp.min(jnp.where(in_group & (logits == tv0), c, big), axis=1, keepdims=True)
    rest = in_group & (c != i0)
    le1 = jnp.where(rest, logits, -jnp.inf)
    tv1 = jnp.max(le1, axis=1, keepdims=True)
    i1 = jnp.min(jnp.where(rest & (logits == tv1), c, big), axis=1, keepdims=True)
    t = jnp.exp(tv1 - tv0)
    w0 = p_sel / (1.0 + t)
    w1 = p_sel * t / (1.0 + t)
    cw = jnp.where(c == i0, w0, jnp.where(c == i1, w1, 0.0))
    return cw, ((c == i0) | (c == i1)).astype(f32)


def _outproj_body(m_ref, wo_ref, x_ref, g_ref, wr_ref, br_ref, xn_ref, u_ref, cw_ref, oh_ref, cnt_ref,
                  wobf_ref):
    @pl.when(pl.program_id(0) == 0)
    def _():
        wobf_ref[...] = wo_ref[...].astype(bf16)
        cnt_ref[...] = jnp.zeros_like(cnt_ref)

    w_hi, w_lo = _split_bf16(wr_ref[...], 2)
    tm = x_ref.shape[0]
    parts = 2 if tm % 32 == 0 else 1
    for p in range(parts):
        rs = slice(p * (tm // parts), (p + 1) * (tm // parts))
        xn = x_ref[rs, :] + jnp.dot(m_ref[rs, :], wobf_ref[...], preferred_element_type=f32)
        xn_ref[rs, :] = xn
        u = xn * lax.rsqrt(jnp.mean(xn * xn, axis=-1, keepdims=True) + EPS) * g_ref[...]
        u_ref[rs, :] = u
        u_hi, u_lo = _split_bf16(u, 2)
        logits = (jnp.dot(u_hi, w_hi, preferred_element_type=f32)
                  + (jnp.dot(u_hi, w_lo, preferred_element_type=f32)
                     + jnp.dot(u_lo, w_hi, preferred_element_type=f32))) + br_ref[...]
        cw, oh = _route(logits)
        cw_ref[rs, :] = cw
        oh_ref[rs, :] = oh
        cnt_ref[...] += jnp.sum(oh, axis=0, keepdims=True)


def _outproj(merged, wo, l, x, g, wr, br, *, tm):
    m = x.shape[0]
    row = lambda i: (i, 0)
    fix = lambda i: (0, 0)
    return pl.pallas_call(
        _outproj_body, grid=(m // tm,),
        in_specs=[pl.BlockSpec((tm, D_MODEL), row),
                  pl.BlockSpec((None, D_MODEL, D_MODEL), lambda i: (l, 0, 0), pipeline_mode=pl.Buffered(1)),
                  pl.BlockSpec((tm, D_MODEL), row), pl.BlockSpec((1, D_MODEL), fix),
                  pl.BlockSpec((D_MODEL, LANES), fix), pl.BlockSpec((1, LANES), fix)],
        out_specs=[pl.BlockSpec((tm, D_MODEL), row), pl.BlockSpec((tm, D_MODEL), row),
                   pl.BlockSpec((tm, LANES), row), pl.BlockSpec((tm, LANES), row), pl.BlockSpec((1, LANES), fix)],
        out_shape=[jax.ShapeDtypeStruct((m, D_MODEL), f32), jax.ShapeDtypeStruct((m, D_MODEL), f32),
                   jax.ShapeDtypeStruct((m, LANES), f32), jax.ShapeDtypeStruct((m, LANES), f32),
                   jax.ShapeDtypeStruct((1, LANES), f32)],
        scratch_shapes=[pltpu.VMEM((D_MODEL, D_MODEL), bf16)],
        compiler_params=_params("arbitrary"), name="outproj")(merged, wo, x, g, wr, br)


def _plan_body(oh_ref, cw_ref, cnt_ref, dest_ref, wts_ref, te_ref, pad_ref, base_ref, carry_ref, *, rows, blk):
    lane = lax.broadcasted_iota(jnp.int32, (8, LANES), 1)
    is_expert = (lane >= MOE_LANE0) & (lane < MOE_LANE0 + MOE_EXPERTS)

    @pl.when(pl.program_id(0) == 0)
    def _():
        cnt = jnp.broadcast_to(cnt_ref[...], (8, LANES))
        padded = jnp.floor((cnt + (MOE_TILE - 1)) * (1.0 / MOE_TILE)) * MOE_TILE
        src = lax.broadcasted_iota(jnp.int32, (LANES, LANES), 0)
        dst = lax.broadcasted_iota(jnp.int32, (LANES, LANES), 1)
        base = _dot_exact01(padded, (src < dst).astype(bf16), 3)
        base_ref[...] = base[0:1]
        carry_ref[...] = jnp.zeros_like(carry_ref)
        end = base + padded
        total = jnp.max(jnp.where(is_expert, end, 0.0), axis=1, keepdims=True)
        end_col = jnp.transpose(jnp.where(is_expert, end, jnp.inf))[:, 0:1]
        tile_start = (lax.broadcasted_iota(jnp.int32, (LANES, LANES), 1) * MOE_TILE).astype(f32)
        tile_exp = jnp.sum((end_col <= tile_start).astype(f32), axis=0, keepdims=True)
        tile_exp = jnp.minimum(tile_exp, MOE_EXPERTS - 1.0)
        te_ref[0:1, :] = tile_exp.astype(jnp.int32)
        te_ref[1:2, :] = jnp.broadcast_to(total[0:1] * (1.0 / MOE_TILE), (1, LANES)).astype(jnp.int32)
        shift = (src == dst + MOE_LANE0).astype(bf16)
        is_tail = lane[0:1] == MOE_EXPERTS
        pad_lo = jnp.where(is_tail, total[0:1], _dot_exact01(base + cnt, shift, 3)[0:1])
        pad_hi = jnp.where(is_tail, float(rows), _dot_exact01(end, shift, 3)[0:1])
        pad_ref[0:1, :] = pad_lo.astype(jnp.int32)
        pad_ref[1:2, :] = pad_hi.astype(jnp.int32)

    n = blk
    row = lax.broadcasted_iota(jnp.int32, (n, n), 0)
    col = lax.broadcasted_iota(jnp.int32, (n, n), 1)
    earlier = (row > col).astype(bf16)
    lanes = lax.broadcasted_iota(jnp.int32, (n, LANES), 1)
    for s in range(oh_ref.shape[0] // blk):
        rs = slice(s * blk, (s + 1) * blk)
        oh = oh_ref[rs, :]
        rank = jnp.dot(earlier, oh.astype(bf16), preferred_element_type=f32)
        pos = base_ref[...] + carry_ref[...] + rank
        carry_ref[...] += jnp.sum(oh, axis=0, keepdims=True)
        chosen = oh > 0.0
        d_lo = jnp.min(jnp.where(chosen, pos, jnp.inf), axis=1, keepdims=True)
        d_hi = jnp.max(jnp.where(chosen, pos, -1.0), axis=1, keepdims=True)
        cw = cw_ref[rs, :]
        w_lo = jnp.sum(jnp.where(chosen & (pos == d_lo), cw, 0.0), axis=1, keepdims=True)
        w_hi = jnp.sum(jnp.where(chosen & (pos == d_hi), cw, 0.0), axis=1, keepdims=True)
        wts_ref[rs, :] = jnp.where(lanes == 0, w_lo, jnp.where(lanes == 1, w_hi, 0.0))
        dests = jnp.where(lanes == 0, d_lo, jnp.where(lanes == 1, d_hi, 0.0))
        dest_ref[s] = jnp.transpose(dests)[0:2, :].astype(jnp.int32)


def _token_blocks_per_step(nblk):
    return max(d for d in range(1, 6) if nblk % d == 0)


def _moe_plan(oh, cw, cnt, *, blk, rows):
    m = oh.shape[0]
    nblk = m // blk
    nsub = _token_blocks_per_step(nblk)
    fix = lambda i: (0, 0)
    return pl.pallas_call(
        functools.partial(_plan_body, rows=rows, blk=blk), grid=(nblk // nsub,),
        in_specs=[pl.BlockSpec((nsub * blk, LANES), lambda i: (i, 0)),
                  pl.BlockSpec((nsub * blk, LANES), lambda i: (i, 0)),
                  pl.BlockSpec((1, LANES), fix)],
        out_specs=[pl.BlockSpec((nsub, 2, blk), lambda i: (i, 0, 0)),
                   pl.BlockSpec((nsub * blk, LANES), lambda i: (i, 0)),
                   pl.BlockSpec((2, LANES), fix), pl.BlockSpec((2, LANES), fix)],
        out_shape=[jax.ShapeDtypeStruct((nblk, 2, blk), jnp.int32), jax.ShapeDtypeStruct((m, LANES), f32),
                   jax.ShapeDtypeStruct((2, LANES), jnp.int32), jax.ShapeDtypeStruct((2, LANES), jnp.int32)],
        scratch_shapes=[pltpu.VMEM((1, LANES), f32), pltpu.VMEM((1, LANES), f32)],
        compiler_params=_params("arbitrary"), name="moe_plan")(oh, cw, cnt)


def _row_copy(src_ref, s, dst_ref, d, sem):
    return pltpu.make_async_copy(src_ref.at[pl.ds(s, 1), :], dst_ref.at[pl.ds(d, 1), :], sem)


def _dispatch_body(dest_ref, pad_ref, u_ref, xs_ref, zero_ref, sem, *, blk, nsub):
    i = pl.program_id(0)

    def wait_rows(count, which):
        pltpu.make_async_copy(xs_ref.at[pl.ds(0, count), :], xs_ref.at[pl.ds(0, count), :], sem.at[which]).wait()

    @pl.when(i == 0)
    def _():
        zero_ref[...] = jnp.zeros_like(zero_ref)
        group = zero_ref.shape[0]

        def fill_row(r, c):
            _row_copy(zero_ref, 0, xs_ref, r, sem.at[1]).start()
            return c

        def fill_group(lo8):
            def f(r, c):
                row0 = pl.multiple_of(lo8 + r * group, group)
                pltpu.make_async_copy(zero_ref, xs_ref.at[pl.ds(row0, group), :], sem.at[1]).start()
                return c
            return f

        nrow = jnp.int32(0)
        ngroup = jnp.int32(0)
        for e in range(MOE_EXPERTS + 1):
            lo = pad_ref[0, e]
            hi = pad_ref[1, e]
            lo8 = jnp.minimum((lo + group - 1) // group * group, hi)
            lax.fori_loop(lo, lo8, fill_row, 0)
            lax.fori_loop(0, (hi - lo8) // group, fill_group(lo8), 0)
            nrow = nrow + (lo8 - lo)
            ngroup = ngroup + (hi - lo8) // group

        def wait_row(r, c):
            _row_copy(zero_ref, 0, xs_ref, 0, sem.at[1]).wait()
            return c
        lax.fori_loop(0, nrow, wait_row, 0)

        def wait_group(r, c):
            wait_rows(group, 1)
            return c
        lax.fori_loop(0, ngroup, wait_group, 0)

    for s in range(nsub):
        def send(j, c, s=s):
            _row_copy(u_ref, s * blk + j, xs_ref, dest_ref[s, 0, j], sem.at[0]).start()
            _row_copy(u_ref, s * blk + j, xs_ref, dest_ref[s, 1, j], sem.at[0]).start()
            return c
        lax.fori_loop(0, blk, send, 0)
    for _ in range(MOE_TOP_K * nsub):
        wait_rows(blk, 0)


def _moe_dispatch(u, dest, pad, *, rows, blk):
    m = u.shape[0]
    nblk = m // blk
    nsub = _token_blocks_per_step(nblk)
    return pl.pallas_call(
        functools.partial(_dispatch_body, blk=blk, nsub=nsub), grid=(nblk // nsub,),
        in_specs=[pl.BlockSpec((nsub, 2, blk), lambda i: (i, 0, 0), memory_space=pltpu.SMEM),
                  pl.BlockSpec(memory_space=pltpu.SMEM),
                  pl.BlockSpec((nsub * blk, D_MODEL), lambda i: (i, 0))],
        out_specs=pl.BlockSpec(memory_space=pl.ANY),
        out_shape=jax.ShapeDtypeStruct((rows, D_MODEL), f32),
        scratch_shapes=[pltpu.VMEM((8, D_MODEL), f32), pltpu.SemaphoreType.DMA((2,))],
        compiler_params=_params("arbitrary"), name="moe_dispatch")(dest, pad, u)


def _gmm_body(te_ref, xs_ref, wg_ref, wu_ref, wd_ref, ys_ref, wgb_ref, wub_ref, wdb_ref):
    i = pl.program_id(0)
    n_used = te_ref[1, 0]

    @pl.when(i < n_used)
    def _():
        @pl.when((i == 0) | (te_ref[0, i] != te_ref[0, jnp.maximum(i - 1, 0)]))
        def _():
            wgb_ref[...] = wg_ref[...].astype(bf16)
            wub_ref[...] = wu_ref[...].astype(bf16)
            wdb_ref[...] = wd_ref[...].astype(bf16)

        x = xs_ref[...].astype(bf16)
        hg = jnp.dot(x, wgb_ref[...], preferred_element_type=f32)
        hu = jnp.dot(x, wub_ref[...], preferred_element_type=f32)
        h = ((hg * _sig(hg)) * hu).astype(bf16)
        ys_ref[...] = jnp.dot(h, wdb_ref[...], preferred_element_type=f32)

    @pl.when(i >= n_used)
    def _():
        ys_ref[...] = jnp.zeros_like(ys_ref)


def _moe_gmm(te, xs, wg, wu, wd, l):
    rows = xs.shape[0]
    ntiles = rows // MOE_TILE
    tile = lambda i, te: (jnp.minimum(i, te[1, 0] - 1), 0)
    up = lambda i, te: (l, te[0, i], 0, 0)
    return pl.pallas_call(
        _gmm_body,
        grid_spec=pltpu.PrefetchScalarGridSpec(
            num_scalar_prefetch=1, grid=(ntiles,),
            in_specs=[pl.BlockSpec((MOE_TILE, D_MODEL), tile),
                      pl.BlockSpec((None, None, D_MODEL, MOE_D_FF), up),
                      pl.BlockSpec((None, None, D_MODEL, MOE_D_FF), up),
                      pl.BlockSpec((None, None, MOE_D_FF, D_MODEL), up)],
            out_specs=pl.BlockSpec((MOE_TILE, D_MODEL), lambda i, te: (i, 0)),
            scratch_shapes=[pltpu.VMEM((D_MODEL, MOE_D_FF), bf16), pltpu.VMEM((D_MODEL, MOE_D_FF), bf16),
                            pltpu.VMEM((MOE_D_FF, D_MODEL), bf16)]),
        out_shape=jax.ShapeDtypeStruct((rows, D_MODEL), f32),
        compiler_params=_params("arbitrary"), name="moe_gmm")(te, xs, wg, wu, wd)


def _combine_body(dcur_ref, dnext_ref, xn_ref, wts_ref, ys_ref, g_ref, o_ref, un_ref, ya_ref, yb_ref, sem,
                  *, blk, nsub, nsteps):
    i = pl.program_id(0)
    slot = i % 2

    def gather(dref, s):
        for b in range(nsub):
            def one(j, c, b=b):
                _row_copy(ys_ref, dref[b, 0, j], ya_ref.at[s], b * blk + j, sem.at[s]).start()
                _row_copy(ys_ref, dref[b, 1, j], yb_ref.at[s], b * blk + j, sem.at[s]).start()
                return c
            lax.fori_loop(0, blk, one, 0)

    @pl.when(i == 0)
    def _():
        gather(dcur_ref, 0)

    @pl.when(i + 1 < nsteps)
    def _():
        gather(dnext_ref, 1 - slot)

    for _ in range(MOE_TOP_K * nsub):
        pltpu.make_async_copy(ys_ref.at[pl.ds(0, blk), :], ya_ref.at[slot, pl.ds(0, blk), :],
                              sem.at[slot]).wait()
    w = wts_ref[...]
    x = xn_ref[...] + w[:, 0:1] * ya_ref[slot] + w[:, 1:2] * yb_ref[slot]
    o_ref[...] = x
    un_ref[...] = (x * lax.rsqrt(jnp.mean(x * x, axis=-1, keepdims=True) + EPS) * g_ref[...]).astype(bf16)


def _moe_combine(dest, xn, wts, ys, g_next, *, blk):
    m = xn.shape[0]
    nblk = m // blk
    nsub = _token_blocks_per_step(nblk)
    nsteps = nblk // nsub
    rows = nsub * blk
    return pl.pallas_call(
        functools.partial(_combine_body, blk=blk, nsub=nsub, nsteps=nsteps), grid=(nsteps,),
        in_specs=[pl.BlockSpec((nsub, 2, blk), lambda i: (i, 0, 0), memory_space=pltpu.SMEM),
                  pl.BlockSpec((nsub, 2, blk), lambda i: (jnp.minimum(i + 1, nsteps - 1), 0, 0),
                               memory_space=pltpu.SMEM),
                  pl.BlockSpec((rows, D_MODEL), lambda i: (i, 0)),
                  pl.BlockSpec((rows, LANES), lambda i: (i, 0)),
                  pl.BlockSpec(memory_space=pl.ANY),
                  pl.BlockSpec((1, D_MODEL), lambda i: (0, 0))],
        out_specs=[pl.BlockSpec((rows, D_MODEL), lambda i: (i, 0)), pl.BlockSpec((rows, D_MODEL), lambda i: (i, 0))],
        out_shape=[jax.ShapeDtypeStruct((m, D_MODEL), f32), jax.ShapeDtypeStruct((m, D_MODEL), bf16)],
        scratch_shapes=[pltpu.VMEM((2, rows, D_MODEL), f32), pltpu.VMEM((2, rows, D_MODEL), f32),
                        pltpu.SemaphoreType.DMA((2,))],
        compiler_params=_params("arbitrary"), name="moe_combine")(dest, dest, xn, wts, ys, g_next)


def _moe(u, cw, oh, cnt, xn, wg, wu, wd, l, g_next):
    m = u.shape[0]
    blk = LANES
    rows = -(-(MOE_TOP_K * m + MOE_EXPERTS * (MOE_TILE - 1)) // MOE_TILE) * MOE_TILE
    assert rows // MOE_TILE <= LANES
    dest, wts, te, pad = _moe_plan(oh, cw, cnt, blk=blk, rows=rows)
    xs = _moe_dispatch(u, dest, pad, rows=rows, blk=blk)
    ys = _moe_gmm(te, xs, wg, wu, wd, l)
    return _moe_combine(dest, xn, wts, ys, g_next.reshape(1, D_MODEL), blk=blk)


def _gla_body(q_ref, k_ref, v_ref, r_ref, sm_ref, w2_ref, bg_ref, gn_ref, s0_ref, prev_ref,
              o_ref, sout_ref, st_ref, b_ref, att_ref, *, chunk, nsteps):
    del prev_ref
    t = pl.program_id(1)

    @pl.when(t == 0)
    def _():
        st_ref[...] = s0_ref[0]

    for c0 in range(0, q_ref.shape[0], chunk):
        _gla_chunk(q_ref, k_ref, v_ref, r_ref, sm_ref, w2_ref, bg_ref, gn_ref, o_ref, st_ref, b_ref, att_ref,
                   c0=c0, chunk=chunk)

    @pl.when(t == nsteps - 1)
    def _():
        sout_ref[0] = st_ref[...]


def _gla_chunk(q_ref, k_ref, v_ref, r_ref, sm_ref, w2_ref, bg_ref, gn_ref, o_ref, st_ref, b_ref, att_ref,
               *, c0, chunk):
    cr = slice(c0, c0 + chunk)
    glr = sm_ref[cr, S_GLR:S_GLR + GLA_GATE_RANK].astype(bf16)
    pre = jnp.dot(glr, w2_ref[...].astype(bf16), preferred_element_type=f32) + bg_ref[...]
    b_ref[...] = _cumsum_rows(_log_sigmoid(pre) * (1.0 / GLA_GATE_TAU))

    sub_row = lax.broadcasted_iota(jnp.int32, (GLA_SUB, GLA_DK), 0)
    sub_col = lax.broadcasted_iota(jnp.int32, (GLA_SUB, GLA_SUB), 1)
    nt = (((1,), (1,)), ((), ()))
    tn = (((0,), (0,)), ((), ()))
    for h in range(GLA_HEADS):
        ks = slice(h * GLA_DK, (h + 1) * GLA_DK)
        vs = slice(h * GLA_DV, (h + 1) * GLA_DV)
        q = q_ref[cr, ks] * (GLA_DK ** -0.5)
        k = k_ref[cr, ks]
        vb = v_ref[cr, vs].astype(bf16)
        b = b_ref[:, ks]
        b_last = b_ref[chunk - 1:chunk, ks]
        o = jnp.dot((q * jnp.exp(b)).astype(bf16), st_ref[h].astype(bf16), preferred_element_type=f32)

        att_ref[h] = jnp.zeros((chunk, chunk), f32)
        for blk in range(chunk // GLA_SUB):
            r0 = blk * GLA_SUB
            q_i = q[r0:r0 + GLA_SUB]
            b_i = b[r0:r0 + GLA_SUB]
            if blk > 0:
                c_i = b_ref[r0 - 1:r0, ks]
                q_dec = (q_i * jnp.exp(b_i - c_i)).astype(bf16)
                k_dec = (k[:r0] * jnp.exp(c_i - b[:r0])).astype(bf16)
                att_ref[h, r0:r0 + GLA_SUB, 0:r0] = lax.dot_general(q_dec, k_dec, nt,
                                                                   preferred_element_type=f32)
            diag = jnp.zeros((GLA_SUB, GLA_SUB), f32)
            for j in range(GLA_SUB):
                b_j = b_ref[r0 + j:r0 + j + 1, ks]
                k_j = k_ref[c0 + r0 + j:c0 + r0 + j + 1, ks]
                decay = jnp.exp(jnp.where(sub_row >= j, b_i - b_j, -jnp.inf))
                a_j = jnp.sum(q_i * decay * k_j, axis=1, keepdims=True)
                diag = jnp.where(sub_col == j, a_j, diag)
            att_ref[h, r0:r0 + GLA_SUB, r0:r0 + GLA_SUB] = diag

        o = o + jnp.dot(att_ref[h].astype(bf16), vb, preferred_element_type=f32)

        k_end = (k * jnp.exp(b_last - b)).astype(bf16)
        upd = lax.dot_general(k_end, vb, tn, preferred_element_type=f32)
        dec_col = jnp.transpose(jnp.broadcast_to(jnp.exp(b_last), (8, GLA_DK)))[:, 0:1]
        st_ref[h] = st_ref[h] * dec_col + upd

        o = o * lax.rsqrt(jnp.mean(o * o, axis=1, keepdims=True) + EPS) * gn_ref[...]
        r = r_ref[cr, vs]
        o_ref[cr, vs] = (o * (r * _sig(r))).astype(o_ref.dtype)


def _gla(proj, small, w2, bg, gn, l, s0, l_s0, prev, *, row0, nseq, seqlen, chunk):
    nchunks = seqlen // chunk
    cps = 2 if nchunks % 2 == 0 else 1
    blk = cps * chunk
    nsteps = nchunks // cps
    blk0 = row0 // blk
    rows = lambda b, t: blk0 + b * nsteps + t
    qk, vw = GLA_QK, GLA_VW
    hd = (GLA_HEADS, GLA_DK, GLA_DV)
    return pl.pallas_call(
        functools.partial(_gla_body, chunk=chunk, nsteps=nsteps),
        grid=(nseq, nsteps),
        in_specs=[pl.BlockSpec((blk, qk), lambda b, t: (rows(b, t), P_Q // qk)),
                  pl.BlockSpec((blk, qk), lambda b, t: (rows(b, t), P_K // qk)),
                  pl.BlockSpec((blk, vw), lambda b, t: (rows(b, t), P_V // vw)),
                  pl.BlockSpec((blk, vw), lambda b, t: (rows(b, t), P_R // vw)),
                  pl.BlockSpec((blk, LANES), lambda b, t: (rows(b, t), 0)),
                  pl.BlockSpec((None, GLA_GATE_RANK, qk), lambda b, t: (l, 0, 0)),
                  pl.BlockSpec((None, 1, qk), lambda b, t: (l, 0, 0)),
                  pl.BlockSpec((None, 1, GLA_DV), lambda b, t: (l, 0, 0)),
                  pl.BlockSpec((None, 1) + hd, lambda b, t: (l_s0, b, 0, 0, 0)),
                  pl.BlockSpec(memory_space=pl.ANY)],
        out_specs=[pl.BlockSpec((blk, vw), lambda b, t: (rows(b, t), 0)),
                   pl.BlockSpec((1,) + hd, lambda b, t: (b, 0, 0, 0))],
        out_shape=[jax.ShapeDtypeStruct(prev.shape, bf16),
                   jax.ShapeDtypeStruct((nseq,) + hd, f32)],
        scratch_shapes=[pltpu.VMEM(hd, f32), pltpu.VMEM((chunk, qk), f32),
                        pltpu.VMEM((GLA_HEADS, chunk, chunk), f32)],
        input_output_aliases={9: 0},
        compiler_params=_params("arbitrary", "arbitrary"),
        name="gla")(proj, proj, proj, proj, small, w2, bg, gn, s0, prev)


def _ssd_body(z_ref, x_ref, bm_ref, cm_ref, sm_ref, wx_ref, wb_ref, wc_ref, bx_ref, bb_ref, bc_ref,
              sx_ref, sb_ref, sc_ref, dtb_ref, alog_ref, dskip_ref, nrm_ref, h0_ref, prev_ref,
              y_ref, hout_ref, cxo_ref, cbo_ref, cco_ref,
              h_ref, xpx_ref, xpb_ref, xpc_ref, xs_ref, *, chunk, nchunks):
    del prev_ref
    t = pl.program_id(1)
    hist = CONV_PAD - (SSM_CONV - 1)

    @pl.when(t == 0)
    def _():
        h_ref[...] = h0_ref[0]
        xpx_ref[hist:CONV_PAD, :] = sx_ref[0]
        xpb_ref[hist:CONV_PAD, :] = sb_ref[0]
        xpc_ref[hist:CONV_PAD, :] = sc_ref[0]

    def conv_silu(xp_ref, cur_ref, w_ref, bias_ref):
        x = cur_ref[...]
        taps = SSM_CONV - 1
        acc = bias_ref[...] + x * w_ref[taps:taps + 1, :]
        for s in range(1, SSM_CONV):
            acc = acc + pltpu.roll(x, s, axis=0) * w_ref[taps - s:taps - s + 1, :]
        xp_ref[CONV_PAD:2 * CONV_PAD, :] = x[0:CONV_PAD]
        head = bias_ref[...] + xp_ref[pl.ds(hist, CONV_PAD), :] * w_ref[0:1, :]
        for i in range(1, SSM_CONV):
            head = head + xp_ref[pl.ds(hist + i, CONV_PAD), :] * w_ref[i:i + 1, :]
        xp_ref[hist:CONV_PAD, :] = x[chunk - taps:chunk]
        acc = jnp.concatenate([head, acc[CONV_PAD:]], axis=0)
        return acc * _sig(acc)

    xs_ref[...] = conv_silu(xpx_ref, x_ref, wx_ref, bx_ref)
    bmat = conv_silu(xpb_ref, bm_ref, wb_ref, bb_ref).astype(bf16)
    cmat = conv_silu(xpc_ref, cm_ref, wc_ref, bc_ref).astype(bf16)

    @pl.when(t == nchunks - 1)
    def _():
        cxo_ref[0] = xpx_ref[hist:CONV_PAD, :]
        cbo_ref[0] = xpb_ref[hist:CONV_PAD, :]
        cco_ref[0] = xpc_ref[hist:CONV_PAD, :]

    dt = _softplus(sm_ref[...] + dtb_ref[...])
    cum = _cumsum_rows(dt * -jnp.exp(alog_ref[...]))
    last = cum[chunk - 1:chunk, :]
    w_end = jnp.exp(last - cum) * dt
    e_cum = jnp.exp(cum)
    e_last = jnp.exp(last)
    cum2 = cum * LOG2E
    adj2_t = ((cum - jnp.log(dt)) * LOG2E).T

    row = lax.broadcasted_iota(jnp.int32, (chunk, chunk), 0)
    col = lax.broadcasted_iota(jnp.int32, (chunk, chunk), 1)
    causal = row >= col
    head = lax.broadcasted_iota(jnp.int32, (LANES, SSM_GROUP_W), 0)
    chan = lax.broadcasted_iota(jnp.int32, (LANES, SSM_GROUP_W), 1)
    nt = (((1,), (1,)), ((), ()))
    tn = (((0,), (0,)), ((), ()))
    for g in range(SSM_GROUPS):
        cs = slice(g * SSM_GROUP_W, (g + 1) * SSM_GROUP_W)
        ns = slice(g * SSM_D_STATE, (g + 1) * SSM_D_STATE)
        xs = xs_ref[:, cs]
        xsb = xs.astype(bf16)
        b_g = bmat[:, ns]
        c_g = cmat[:, ns]
        widen = (head == g * SSM_HPG + chan // SSM_HEAD_DIM).astype(bf16)
        w_full = _dot_exact01(w_end, widen, 2)
        ecum_full = _dot_exact01(e_cum, widen, 2)

        cb = lax.dot_general(c_g, b_g, nt, preferred_element_type=f32)
        state = h_ref[g]
        y = lax.dot_general(c_g, state.astype(bf16), nt, preferred_element_type=f32) * ecum_full
        parts = []
        for j in range(SSM_HPG):
            hd = g * SSM_HPG + j
            seg = jnp.exp2(jnp.where(causal, cum2[:, hd:hd + 1] - adj2_t[hd:hd + 1, :], -jnp.inf))
            parts.append(jnp.dot((cb * seg).astype(bf16), xsb[:, j * SSM_HEAD_DIM:(j + 1) * SSM_HEAD_DIM],
                                 preferred_element_type=f32))
        y = y + jnp.concatenate(parts, axis=1)

        upd = lax.dot_general((xs * w_full).astype(bf16), b_g, tn, preferred_element_type=f32)
        for j in range(SSM_HPG):
            hd = g * SSM_HPG + j
            rs = slice(j * SSM_HEAD_DIM, (j + 1) * SSM_HEAD_DIM)
            h_ref[g, rs, :] = state[rs, :] * e_last[:, hd:hd + 1] + upd[rs, :]

        y = y + dskip_ref[:, cs] * xs
        z = z_ref[:, cs]
        y = y * (z * _sig(z))
        y = y * lax.rsqrt(jnp.mean(y * y, axis=1, keepdims=True) + EPS) * nrm_ref[:, cs]
        y_ref[:, cs] = y.astype(y_ref.dtype)

    @pl.when(t == nchunks - 1)
    def _():
        hout_ref[0] = h_ref[...]


def _ssd(proj, small, conv_w, conv_b, l, conv_s, l_cs, dtb, alog, dskip, nrm, h0, l_h0, prev,
         *, row0, nseq, seqlen, chunk):
    nchunks = seqlen // chunk
    blk0 = row0 // chunk
    di, bc = SSM_D_INNER, SSM_BC
    rows = lambda b, t: blk0 + b * nchunks + t
    taps, hist = SSM_CONV, SSM_CONV - 1
    xb, bb, cb = 0, di // bc, di // bc + 1
    hs = (SSM_GROUPS, SSM_GROUP_W, SSM_D_STATE)
    fix3 = lambda b, t: (l, 0, 0)
    in_specs = [
        pl.BlockSpec((chunk, di), lambda b, t: (rows(b, t), P_Z // di)),
        pl.BlockSpec((chunk, di), lambda b, t: (rows(b, t), P_X // di)),
        pl.BlockSpec((chunk, bc), lambda b, t: (rows(b, t), P_B // bc)),
        pl.BlockSpec((chunk, bc), lambda b, t: (rows(b, t), P_C // bc)),
        pl.BlockSpec((chunk, LANES), lambda b, t: (rows(b, t), 0)),
        pl.BlockSpec((None, taps, di), lambda b, t: (l, 0, xb)),
        pl.BlockSpec((None, taps, bc), lambda b, t: (l, 0, bb)),
        pl.BlockSpec((None, taps, bc), lambda b, t: (l, 0, cb)),
        pl.BlockSpec((None, 1, di), lambda b, t: (l, 0, xb)),
        pl.BlockSpec((None, 1, bc), lambda b, t: (l, 0, bb)),
        pl.BlockSpec((None, 1, bc), lambda b, t: (l, 0, cb)),
        pl.BlockSpec((None, 1, hist, di), lambda b, t: (l_cs, b, 0, xb)),
        pl.BlockSpec((None, 1, hist, bc), lambda b, t: (l_cs, b, 0, bb)),
        pl.BlockSpec((None, 1, hist, bc), lambda b, t: (l_cs, b, 0, cb)),
        pl.BlockSpec((None, 1, LANES), fix3),
        pl.BlockSpec((None, 1, LANES), fix3),
        pl.BlockSpec((None, 1, di), fix3),
        pl.BlockSpec((None, 1, di), fix3),
        pl.BlockSpec((None, 1) + hs, lambda b, t: (l_h0, b, 0, 0, 0)),
        pl.BlockSpec(memory_space=pl.ANY),
    ]
    return pl.pallas_call(
        functools.partial(_ssd_body, chunk=chunk, nchunks=nchunks),
        grid=(nseq, nchunks),
        in_specs=in_specs,
        out_specs=[pl.BlockSpec((chunk, di), lambda b, t: (rows(b, t), 0)),
                   pl.BlockSpec((1,) + hs, lambda b, t: (b, 0, 0, 0)),
                   pl.BlockSpec((1, hist, di), lambda b, t: (b, 0, 0)),
                   pl.BlockSpec((1, hist, bc), lambda b, t: (b, 0, 0)),
                   pl.BlockSpec((1, hist, bc), lambda b, t: (b, 0, 0))],
        out_shape=[jax.ShapeDtypeStruct(prev.shape, bf16),
                   jax.ShapeDtypeStruct((nseq,) + hs, f32),
                   jax.ShapeDtypeStruct((nseq, hist, di), f32),
                   jax.ShapeDtypeStruct((nseq, hist, bc), f32),
                   jax.ShapeDtypeStruct((nseq, hist, bc), f32)],
        scratch_shapes=[pltpu.VMEM(hs, f32), pltpu.VMEM((2 * CONV_PAD, di), f32),
                        pltpu.VMEM((2 * CONV_PAD, bc), f32), pltpu.VMEM((2 * CONV_PAD, bc), f32),
                        pltpu.VMEM((chunk, di), f32)],
        input_output_aliases={19: 0},
        compiler_params=_params("arbitrary", "arbitrary"),
        name="ssd")(proj, proj, proj, proj, small, conv_w, conv_w, conv_w, conv_b, conv_b, conv_b,
                    conv_s, conv_s, conv_s, dtb, alog, dskip, nrm, h0, prev)


def _pad_lanes(v):
    return jnp.pad(v, ((0, 0), (0, LANES - v.shape[-1])))[:, None, :]


def kernel(x_prompt, x_sample, state_gla, state_ssm, state_conv, norm_mix, w_in, gla_w_gate2, gla_b_gate,
           gla_norm, gla_w_out, ssm_conv_w, ssm_conv_b, ssm_dt_bias, ssm_a_log, ssm_d, ssm_norm, ssm_w_out,
           w_o, norm_ffn, router_w1, router_b1, router_w2, router_b2, exp_w_gate, exp_w_up, exp_w_down,
           norm_final):
    n_p, t_p = x_prompt.shape[:2]
    n_s, t_s = x_sample.shape[:2]
    assert n_p == 1
    rows_p, rows_s = n_p * t_p, n_s * t_s
    m = rows_p + rows_s
    tm = 832 if m % 832 == 0 else 128
    tm_out = 416 if m % 416 == 0 else 128
    gla_chunk_p, ssd_chunk_p = min(64, t_p), min(256, t_p)

    x = jnp.concatenate([x_prompt.reshape(rows_p, D_MODEL), x_sample.reshape(rows_s, D_MODEL)], axis=0)
    zeros_gla = jnp.zeros((1, n_p, GLA_HEADS, GLA_DK, GLA_DV), f32)
    zeros_ssm = jnp.zeros((1, n_p, SSM_GROUPS, SSM_GROUP_W, SSM_D_STATE), f32)
    zeros_conv = jnp.zeros((1, n_p, SSM_CONV - 1, SSM_CONV_DIM), f32)
    state_ssm_g = state_ssm.reshape(DEPTH, n_s, SSM_GROUPS, SSM_GROUP_W, SSM_D_STATE)

    bg3 = gla_b_gate[:, None, :]
    gn3 = gla_norm[:, None, :]
    conv_b3 = ssm_conv_b[:, None, :]
    dtb3 = _pad_lanes(ssm_dt_bias)
    alog3 = _pad_lanes(ssm_a_log)
    dskip3 = jnp.repeat(ssm_d, SSM_HEAD_DIM, axis=1)[:, None, :]
    nrm3 = ssm_norm[:, None, :]

    w_in_t = jnp.swapaxes(w_in, 1, 2)
    oa = jnp.zeros((m, GLA_VW), bf16)
    ob = jnp.zeros((m, SSM_D_INNER), bf16)

    outs = {k: [] for k in ("gla_p", "ssm_p", "conv_p", "gla_s", "ssm_s", "conv_s")}
    for l in range(DEPTH):
        w_small_t = jnp.concatenate(
            [w_in_t[l, W_DT:W_MG], w_in_t[l, W_GLR:W_Z],
             jnp.zeros((LANES - SSM_HEADS - GLA_GATE_RANK, D_MODEL), f32)], axis=0)

        if l == 0:
            u = _rmsnorm(x, norm_mix[l], rows=m, row0=0, tm=tm, out_dtype=bf16)
        proj = _inproj(u, w_in_t, l, tm=tm)
        small = _matmul_nt(u, w_small_t, tm=tm, name="inproj_small")

        gla_args = (proj, small, gla_w_gate2, bg3, gn3, l)
        oa, g_p = _gla(*gla_args, zeros_gla, 0, oa, row0=0, nseq=n_p, seqlen=t_p, chunk=gla_chunk_p)
        oa, g_s = _gla(*gla_args, state_gla, l, oa, row0=rows_p, nseq=n_s, seqlen=t_s, chunk=t_s)

        ssd_args = (proj, small, ssm_conv_w, conv_b3, l)
        ssd_tail = (dtb3, alog3, dskip3, nrm3)
        ob, h_p, *conv_p = _ssd(*ssd_args, zeros_conv, 0, *ssd_tail, zeros_ssm, 0, ob,
                                row0=0, nseq=n_p, seqlen=t_p, chunk=ssd_chunk_p)
        ob, h_s, *conv_s = _ssd(*ssd_args, state_conv, l, *ssd_tail, state_ssm_g, l, ob,
                                row0=rows_p, nseq=n_s, seqlen=t_s, chunk=t_s)

        merged = _merge(oa, ob, proj, gla_w_out, ssm_w_out, l, tm=tm_out, tn=512)

        wr = jnp.concatenate([router_w1[l], router_w2[l],
                              jnp.zeros((D_MODEL, LANES - MOE_GROUPS - MOE_EXPERTS), f32)], axis=1)
        br = jnp.pad(jnp.concatenate([router_b1[l], router_b2[l]]),
                     (0, LANES - MOE_GROUPS - MOE_EXPERTS)).reshape(1, LANES)
        xn, u2, cw, oh, cnt = _outproj(merged, w_o, l, x, norm_ffn[l].reshape(1, -1), wr, br, tm=tm_out)
        x, u = _moe(u2, cw, oh, cnt, xn, exp_w_gate, exp_w_up, exp_w_down, l, norm_mix[(l + 1) % DEPTH])

        outs["gla_p"].append(g_p)
        outs["gla_s"].append(g_s)
        outs["ssm_p"].append(h_p.reshape(n_p, SSM_HEADS, SSM_HEAD_DIM, SSM_D_STATE))
        outs["ssm_s"].append(h_s.reshape(n_s, SSM_HEADS, SSM_HEAD_DIM, SSM_D_STATE))
        outs["conv_p"].append(jnp.concatenate(conv_p, axis=-1))
        outs["conv_s"].append(jnp.concatenate(conv_s, axis=-1))

    tm_p = 512 if rows_p % 512 == 0 else 128
    tm_s = rows_s if rows_p % rows_s == 0 else 8
    y_p = _rmsnorm(x, norm_final, rows=rows_p, row0=0, tm=tm_p, out_dtype=f32)
    y_s = _rmsnorm(x, norm_final, rows=rows_s, row0=rows_p, tm=tm_s, out_dtype=f32)
    return (y_p.reshape(n_p, t_p, D_MODEL), y_s.reshape(n_s, t_s, D_MODEL),
            jnp.stack(outs["gla_p"]), jnp.stack(outs["ssm_p"]), jnp.stack(outs["conv_p"]),
            jnp.stack(outs["gla_s"]), jnp.stack(outs["ssm_s"]), jnp.stack(outs["conv_s"]))
```
